```python
import math
import jax, jax.numpy as jnp
from jax import lax
import numpy as np

D_MODEL = 2048
BATCH = 2
SEQ = 4096
DEPTH = 1

CHUNK = 64
EPS = 1e-6
N_HEADS_A = 8
HEAD_DIM = 128
ATTN_WIDTH = N_HEADS_A * HEAD_DIM
Q_BLOCK = 128
SGU_WIDTH = D_MODEL // 2
SGU_GROUPS = 8
SGU_GROUP_DIM = SGU_WIDTH // SGU_GROUPS
SGU_BLOCK = 128
PROJ_SPLITS = (ATTN_WIDTH, ATTN_WIDTH, ATTN_WIDTH, 2 * SGU_WIDTH, D_MODEL, D_MODEL)
PROJ_WIDTH = sum(PROJ_SPLITS)
N_EXPERTS = 32
TOP_K = 4
D_EXPERT = D_MODEL
SWIGLU_LIMIT = 7.0
SWIGLU_ALPHA = 1.702
ROW_BLOCK = 128

kernel_name = "hybrid_stickbreak_gmlp_moe_block"


def rms_norm(x, gain):
    xf = x.astype(jnp.float32)
    y = xf * lax.rsqrt(jnp.mean(xf * xf, axis=-1, keepdims=True) + EPS)
    return (y * gain.astype(jnp.float32)).astype(x.dtype)


def layer_norm(x, gain):
    xf = x.astype(jnp.float32)
    xc = xf - jnp.mean(xf, axis=-1, keepdims=True)
    y = xc * lax.rsqrt(jnp.mean(xc * xc, axis=-1, keepdims=True) + EPS)
    return (y * gain.astype(jnp.float32)).astype(x.dtype)


def stick_breaking_attention(q, k, v):
    seq = q.shape[2]
    scale = HEAD_DIM ** -0.5
    outs = []
    for blk in range(seq // Q_BLOCK):
        q0 = blk * Q_BLOCK
        end = q0 + Q_BLOCK
        z = jnp.einsum('bhtd,bhsd->bhts', q[:, :, q0:end], k[:, :, :end]).astype(jnp.float32) * scale
        t_pos = q0 + jnp.arange(Q_BLOCK)[:, None]
        s_pos = jnp.arange(end)[None, :]
        past = s_pos < t_pos
        log_keep = jnp.where(past, jax.nn.log_sigmoid(-z), 0.0)
        after = lax.cumsum(log_keep, axis=3, reverse=True) - log_keep
        w = jnp.where(past, jnp.exp(jax.nn.log_sigmoid(z) + after), 0.0)
        outs.append(jnp.einsum('bhts,bhsd->bhtd', w.astype(v.dtype), v[:, :, :end]))
    return jnp.concatenate(outs, axis=2)


def spatial_gating(v, w_s, b_s):
    bsz, seq, _ = v.shape
    vb = v.reshape(bsz, seq // SGU_BLOCK, SGU_BLOCK, SGU_GROUPS, SGU_GROUP_DIM)
    i = jnp.arange(SGU_BLOCK)
    mask = (i[None, :] // CHUNK) <= (i[:, None] // CHUNK)
    w = jnp.where(mask[None], w_s, 0.0).astype(v.dtype)
    out = jnp.einsum('gij,bnjgc->bnigc', w, vb) + b_s.T[None, None, :, :, None].astype(v.dtype)
    return out.reshape(bsz, seq, SGU_WIDTH)


def moe_ffn(h, w_router, b_router, w_up, b_up, w_down, b_down):
    bsz, seq, d = h.shape
    xt = h.reshape(-1, d)
    n_tok = xt.shape[0]
    logits = (xt @ w_router + b_router).astype(jnp.float32)
    top_val, top_idx = lax.top_k(logits, TOP_K)
    gate = jax.nn.softmax(top_val, axis=-1)
    n_assign = n_tok * TOP_K
    flat_e = top_idx.reshape(-1)
    order = jnp.argsort(flat_e)
    sorted_e = flat_e[order]
    counts = jnp.bincount(flat_e, length=N_EXPERTS)
    padded = ((counts + ROW_BLOCK - 1) // ROW_BLOCK) * ROW_BLOCK
    pad_end = jnp.cumsum(padded)
    pad_start = pad_end - padded
    start = jnp.cumsum(counts) - counts
    dest = pad_start[sorted_e] + jnp.arange(n_assign) - start[sorted_e]
    n_rows = n_assign + N_EXPERTS * ROW_BLOCK
    n_blocks = n_rows // ROW_BLOCK
    rows = jnp.zeros((n_rows, d), xt.dtype).at[dest].set(xt[order // TOP_K])
    block_e = jnp.minimum(jnp.searchsorted(pad_end, jnp.arange(n_blocks) * ROW_BLOCK, side='right'),
                          N_EXPERTS - 1).astype(jnp.int32)

    def expert_block(args):
        xb, e = args
        hu = xb @ w_up[e] + b_up[e]
        g, u = hu[:, :D_EXPERT], hu[:, D_EXPERT:]
        g = jnp.minimum(g, SWIGLU_LIMIT)
        u = jnp.clip(u, -SWIGLU_LIMIT, SWIGLU_LIMIT)
        a = g * jax.nn.sigmoid(SWIGLU_ALPHA * g) * (u + 1.0)
        return a @ w_down[e] + b_down[e]

    y_rows = lax.map(expert_block, (rows.reshape(n_blocks, ROW_BLOCK, d), block_e)).reshape(n_rows, d)
    y_sorted = y_rows[dest]
    y_assign = jnp.zeros_like(y_sorted).at[order].set(y_sorted).reshape(n_tok, TOP_K, d)
    out = jnp.einsum('tk,tkd->td', gate.astype(y_assign.dtype), y_assign)
    return out.reshape(bsz, seq, d)


def hybrid_layer(h, norm1_gain, w_in, q_norm_gain, k_norm_gain, sgu_norm_gain, w_spatial, b_spatial,
                 w_branch_a, w_branch_b, w_out, norm2_gain, w_router, b_router, w_up, b_up, w_down, b_down):
    bsz, seq, _ = h.shape
    xn = rms_norm(h, norm1_gain)
    proj = xn @ w_in
    offsets = list(np.cumsum(PROJ_SPLITS)[:-1])
    q, k, v, uv, g_a, g_b = jnp.split(proj, offsets, axis=-1)
    heads = lambda t: t.reshape(bsz, seq, N_HEADS_A, HEAD_DIM)
    q = rms_norm(heads(q), q_norm_gain).transpose(0, 2, 1, 3)
    k = rms_norm(heads(k), k_norm_gain).transpose(0, 2, 1, 3)
    v = heads(v).transpose(0, 2, 1, 3)
    o_a = stick_breaking_attention(q, k, v).transpose(0, 2, 1, 3).reshape(bsz, seq, ATTN_WIDTH)
    uv = jax.nn.gelu(uv)
    u, vv = uv[..., :SGU_WIDTH], uv[..., SGU_WIDTH:]
    o_b = u * spatial_gating(layer_norm(vv, sgu_norm_gain), w_spatial, b_spatial)
    y = jax.nn.sigmoid(g_a) * (o_a @ w_branch_a) + jax.nn.sigmoid(g_b) * (o_b @ w_branch_b)
    h = h + y @ w_out
    return h + moe_ffn(rms_norm(h, norm2_gain), w_router, b_router, w_up, b_up, w_down, b_down)


def setup_inputs(seed: int = 0) -> dict:
    key = jax.random.key(seed)
    ks = jax.random.split(key, 18)
    L = DEPTH
    f32 = jnp.float32
    nrm = lambda k, shape, fan_in: jax.random.normal(k, shape, f32) * (fan_in ** -0.5)
    gain = lambda k, n: 1.0 + 0.02 * jax.random.normal(k, (L, n), f32)
    return {
        "x": jax.random.normal(ks[0], (BATCH, SEQ, D_MODEL), f32),
        "norm1_gain": gain(ks[1], D_MODEL),
        "w_in": nrm(ks[2], (L, D_MODEL, PROJ_WIDTH), D_MODEL),
        "q_norm_gain": gain(ks[3], HEAD_DIM),
        "k_norm_gain": gain(ks[4], HEAD_DIM),
        "sgu_norm_gain": gain(ks[5], SGU_WIDTH),
        "w_spatial": nrm(ks[6], (L, SGU_GROUPS, SGU_BLOCK, SGU_BLOCK), SGU_BLOCK),
        "b_spatial": 1.0 + 0.02 * jax.random.normal(ks[7], (L, SGU_GROUPS, SGU_BLOCK), f32),
        "w_branch_a": nrm(ks[8], (L, ATTN_WIDTH, D_MODEL), ATTN_WIDTH),
        "w_branch_b": nrm(ks[9], (L, SGU_WIDTH, D_MODEL), SGU_WIDTH),
        "w_out": nrm(ks[10], (L, D_MODEL, D_MODEL), D_MODEL),
        "norm2_gain": gain(ks[11], D_MODEL),
        "w_router": nrm(ks[12], (L, D_MODEL, N_EXPERTS), D_MODEL),
        "b_router": 0.01 * jax.random.normal(ks[13], (L, N_EXPERTS), f32),
        "w_up": nrm(ks[14], (L, N_EXPERTS, D_MODEL, 2 * D_EXPERT), D_MODEL),
        "b_up": 0.01 * jax.random.normal(ks[15], (L, N_EXPERTS, 2 * D_EXPERT), f32),
        "w_down": nrm(ks[16], (L, N_EXPERTS, D_EXPERT, D_MODEL), D_EXPERT),
        "b_down": 0.01 * jax.random.normal(ks[17], (L, N_EXPERTS, D_MODEL), f32),
    }


def reference(x, norm1_gain, w_in, q_norm_gain, k_norm_gain, sgu_norm_gain, w_spatial, b_spatial,
              w_branch_a, w_branch_b, w_out, norm2_gain, w_router, b_router, w_up, b_up, w_down, b_down):
    h = x
    for l in range(DEPTH):
        h = hybrid_layer(h, norm1_gain[l], w_in[l], q_norm_gain[l], k_norm_gain[l], sgu_norm_gain[l],
                         w_spatial[l], b_spatial[l], w_branch_a[l], w_branch_b[l], w_out[l], norm2_gain[l],
                         w_router[l], b_router[l], w_up[l], b_up[l], w_down[l], b_down[l])
    return h
```

```python
import functools
import math

import jax
import jax.numpy as jnp
from jax import lax
from jax.experimental import pallas as pl
from jax.experimental.pallas import tpu as pltpu

F32 = jnp.float32
BF16 = jnp.bfloat16

EPS = 1e-6
N_HEADS = 8
HEAD_DIM = 128
SGU_GROUPS = 8
SGU_BLOCK = 128
CHUNK = 64
N_EXPERTS = 32
TOP_K = 4
SWIGLU_LIMIT = 7.0
SWIGLU_ALPHA = 1.702

VMEM_LIMIT = 56 * 1024 * 1024
PROJ_TM = 1024
PROJ_TN = 512
ATT_T = 256
MERGE_TM = 256
ROUTER_TM = 256
EXP_TM = 1024
EXP_SUB = 512
EXP_TC = 256
DISPATCH_TB = 512
COMBINE_TB = 128


def _sigmoid(x):
    return 1.0 / (1.0 + jnp.exp(-x))


def _gelu_tanh(x):
    c = math.sqrt(2.0 / math.pi)
    return 0.5 * x * (1.0 + jnp.tanh(c * (x + 0.044715 * (x * x * x))))


def _proj_kernel(x_ref, g1_ref, w_ref, qg_ref, kg_ref, o_ref, xn_ref, acc_ref):
    j = pl.program_id(1)
    n_q = (N_HEADS * HEAD_DIM) // PROJ_TN

    @pl.when(j == 0)
    def _():
        x = x_ref[...]
        ms = jnp.mean(x * x, axis=-1, keepdims=True)
        xn_ref[...] = (x * lax.rsqrt(ms + EPS) * g1_ref[...]).astype(BF16)

    acc_ref[...] = jnp.dot(xn_ref[...], w_ref[...].astype(BF16), preferred_element_type=F32)

    @pl.when(j < 2 * n_q)
    def _():
        gain = jnp.where(j < n_q, qg_ref[...], kg_ref[...])
        for h in range(PROJ_TN // HEAD_DIM):
            a = acc_ref[:, h * HEAD_DIM:(h + 1) * HEAD_DIM]
            ms = jnp.mean(a * a, axis=-1, keepdims=True)
            o_ref[:, h * HEAD_DIM:(h + 1) * HEAD_DIM] = (a * lax.rsqrt(ms + EPS) * gain).astype(BF16)

    @pl.when((j >= 2 * n_q) & (j < 3 * n_q))
    def _():
        o_ref[...] = acc_ref[...].astype(BF16)

    @pl.when((j >= 3 * n_q) & (j < 5 * n_q))
    def _():
        o_ref[...] = _gelu_tanh(acc_ref[...]).astype(BF16)

    @pl.when(j >= 5 * n_q)
    def _():
        o_ref[...] = _sigmoid(acc_ref[...]).astype(BF16)


def _proj(x2, g1, w_in, qg, kg):
    t, d = x2.shape
    n = w_in.shape[1]
    nj = n // PROJ_TN
    shift = (2 * d) // PROJ_TN
    return pl.pallas_call(
        _proj_kernel,
        grid=(t // PROJ_TM, nj),
        in_specs=[
            pl.BlockSpec((PROJ_TM, d), lambda i, j: (i, 0)),
            pl.BlockSpec((1, d), lambda i, j: (0, 0)),
            pl.BlockSpec((d, PROJ_TN), lambda i, j: (0, j)),
            pl.BlockSpec((1, HEAD_DIM), lambda i, j: (0, 0)),
            pl.BlockSpec((1, HEAD_DIM), lambda i, j: (0, 0)),
        ],
        out_specs=pl.BlockSpec((PROJ_TM, PROJ_TN), lambda i, j: (i, (j + shift) % nj)),
        out_shape=jax.ShapeDtypeStruct((t, n), BF16),
        scratch_shapes=[pltpu.VMEM((PROJ_TM, d), BF16), pltpu.VMEM((PROJ_TM, PROJ_TN), F32)],
        compiler_params=pltpu.CompilerParams(
            dimension_semantics=("arbitrary", "arbitrary"), vmem_limit_bytes=VMEM_LIMIT),
        name="proj",
    )(x2, g1, w_in, qg, kg)


def _attn_kernel(q_ref, k_ref, v_ref, u_ref, o_ref):
    qi = pl.program_id(2)
    q = q_ref[...]
    scale = HEAD_DIM ** -0.5
    tri = u_ref[...]

    def block(j, carry, diag):
        acc, run = carry
        start = pl.multiple_of(j * ATT_T, ATT_T)
        kb = k_ref[pl.ds(start, ATT_T), :]
        vb = v_ref[pl.ds(start, ATT_T), :]
        z = lax.dot_general(q, kb, (((1,), (1,)), ((), ())), preferred_element_type=F32) * scale
        sp = jnp.maximum(z, 0.0) + jnp.log1p(jnp.exp(-jnp.abs(z)))
        if diag:
            row = lax.broadcasted_iota(jnp.int32, (ATT_T, ATT_T), 0)
            col = lax.broadcasted_iota(jnp.int32, (ATT_T, ATT_T), 1)
            past = col < row
            sp = jnp.where(past, sp, 0.0)
        hi = sp.astype(BF16)
        lo = (sp - hi.astype(F32)).astype(BF16)
        cs = (jnp.dot(hi, tri, preferred_element_type=F32)
              + jnp.dot(lo, tri, preferred_element_type=F32))
        w = jnp.exp(z - (run + cs))
        if diag:
            w = jnp.where(past, w, 0.0)
        acc = acc + jnp.dot(w.astype(BF16), vb, preferred_element_type=F32)
        return acc, run + cs[:, 0:1]

    carry = (jnp.zeros((ATT_T, HEAD_DIM), F32), jnp.zeros((ATT_T, 1), F32))
    carry = block(qi, carry, True)
    carry = lax.fori_loop(0, qi, lambda s, c: block(qi - 1 - s, c, False), carry)
    o_ref[...] = carry[0].astype(BF16)


def _attn(proj, bsz, seq, d):
    t = bsz * seq
    nq = seq // ATT_T
    col0 = (2 * d) // HEAD_DIM
    i = jnp.arange(ATT_T)
    tri = (i[:, None] >= i[None, :]).astype(BF16)
    return pl.pallas_call(
        _attn_kernel,
        grid=(bsz, N_HEADS, nq),
        in_specs=[
            pl.BlockSpec((ATT_T, HEAD_DIM), lambda b, h, qi: (b * nq + qi, col0 + h)),
            pl.BlockSpec((seq, HEAD_DIM), lambda b, h, qi: (b, col0 + N_HEADS + h)),
            pl.BlockSpec((seq, HEAD_DIM), lambda b, h, qi: (b, col0 + 2 * N_HEADS + h)),
            pl.BlockSpec((ATT_T, ATT_T), lambda b, h, qi: (0, 0)),
        ],
        out_specs=pl.BlockSpec((ATT_T, HEAD_DIM), lambda b, h, qi: (b * nq + qi, h)),
        out_shape=jax.ShapeDtypeStruct((t, N_HEADS * HEAD_DIM), BF16),
        compiler_params=pltpu.CompilerParams(
            dimension_semantics=("arbitrary", "arbitrary", "arbitrary"), vmem_limit_bytes=VMEM_LIMIT),
        name="attn",
    )(proj, proj, proj, tri)


def _merge_kernel(x_ref, oa_ref, u_ref, vv_ref, ga_ref, gb_ref, ws_ref, bst_ref, sg_ref,
                  wa_ref, wb_ref, wo_ref, h_ref, vln_ref, ob_ref):
    vv = vv_ref[...].astype(F32)
    mu = jnp.mean(vv, axis=-1, keepdims=True)
    xc = vv - mu
    var = jnp.mean(xc * xc, axis=-1, keepdims=True)
    vln_ref[...] = (xc * lax.rsqrt(var + EPS) * sg_ref[...]).astype(BF16)

    row = lax.broadcasted_iota(jnp.int32, (SGU_BLOCK, SGU_BLOCK), 0)
    col = lax.broadcasted_iota(jnp.int32, (SGU_BLOCK, SGU_BLOCK), 1)
    mask = (col // CHUNK) <= (row // CHUNK)
    for g in range(SGU_GROUPS):
        wg = jnp.where(mask, ws_ref[g], 0.0).astype(BF16)
        bg = bst_ref[:, g:g + 1]
        cs = slice(g * SGU_BLOCK, (g + 1) * SGU_BLOCK)
        for n in range(MERGE_TM // SGU_BLOCK):
            rs = slice(n * SGU_BLOCK, (n + 1) * SGU_BLOCK)
            sgu = jnp.dot(wg, vln_ref[rs, cs], preferred_element_type=F32) + bg
            ob_ref[rs, cs] = (u_ref[rs, cs].astype(F32) * sgu).astype(BF16)

    ya = jnp.dot(oa_ref[...], wa_ref[...], preferred_element_type=F32)
    yb = jnp.dot(ob_ref[...], wb_ref[...], preferred_element_type=F32)
    y = ga_ref[...].astype(F32) * ya + gb_ref[...].astype(F32) * yb
    h_ref[...] = x_ref[...] + jnp.dot(y.astype(BF16), wo_ref[...], preferred_element_type=F32)


def _merge(x2, oa, proj, w_spatial, b_spatial_t, sgu_gain, wa, wb, wo):
    t, d = x2.shape
    aw = oa.shape[1]
    sw = SGU_GROUPS * SGU_BLOCK
    ucol = (2 * d + 3 * aw) // sw
    const = lambda shape: pl.BlockSpec(shape, lambda i: (0,) * len(shape), pipeline_mode=pl.Buffered(1))
    return pl.pallas_call(
        _merge_kernel,
        grid=(t // MERGE_TM,),
        in_specs=[
            pl.BlockSpec((MERGE_TM, d), lambda i: (i, 0)),
            pl.BlockSpec((MERGE_TM, aw), lambda i: (i, 0)),
            pl.BlockSpec((MERGE_TM, sw), lambda i: (i, ucol)),
            pl.BlockSpec((MERGE_TM, sw), lambda i: (i, ucol + 1)),
            pl.BlockSpec((MERGE_TM, d), lambda i: (i, 0)),
            pl.BlockSpec((MERGE_TM, d), lambda i: (i, 1)),
            const((SGU_GROUPS, SGU_BLOCK, SGU_BLOCK)),
            const((SGU_BLOCK, SGU_GROUPS)),
            const((1, sw)),
            const((aw, d)),
            const((sw, d)),
            const((d, d)),
        ],
        out_specs=pl.BlockSpec((MERGE_TM, d), lambda i: (i, 0)),
        out_shape=jax.ShapeDtypeStruct((t, d), F32),
        scratch_shapes=[pltpu.VMEM((MERGE_TM, sw), BF16), pltpu.VMEM((MERGE_TM, sw), BF16)],
        compiler_params=pltpu.CompilerParams(
            dimension_semantics=("arbitrary",), vmem_limit_bytes=VMEM_LIMIT),
        name="merge",
    )(x2, oa, proj, proj, proj, proj, w_spatial, b_spatial_t, sgu_gain, wa, wb, wo)


def _router_kernel(h_ref, g2_ref, wr_ref, br_ref, lt_ref, xp_ref, idx_ref, rank_ref, gate_ref, cnt_ref,
                   carry_ref):
    i = pl.program_id(0)
    half = h_ref.shape[1] // 2

    @pl.when(i == 0)
    def _():
        carry_ref[...] = jnp.zeros_like(carry_ref)

    h = h_ref[...]
    ms = jnp.mean(h * h, axis=-1, keepdims=True)
    xn = h * lax.rsqrt(ms + EPS) * g2_ref[...]
    bits = lax.bitcast_convert_type(xn.astype(BF16).astype(F32), jnp.uint32)
    xp_ref[...] = bits[:, half:] | (bits[:, :half] >> 16)

    logits = jnp.dot(xn, wr_ref[...], preferred_element_type=F32,
                     precision=lax.Precision.HIGHEST) + br_ref[...]
    lane = lax.broadcasted_iota(jnp.int32, logits.shape, 1)
    out_lane = lax.broadcasted_iota(jnp.int32, idx_ref.shape, 1)
    work = logits
    vals, sels = [], []
    idx_out = jnp.zeros(idx_ref.shape, jnp.int32)
    for k in range(TOP_K):
        m = jnp.max(work, axis=1, keepdims=True)
        idx = jnp.min(jnp.where(work == m, lane, N_EXPERTS), axis=1, keepdims=True)
        sel = lane == idx
        work = jnp.where(sel, -jnp.inf, work)
        vals.append(m)
        sels.append(sel)
        idx_out = jnp.where(out_lane == k, idx, idx_out)
    idx_ref[...] = idx_out

    exps = [jnp.exp(v - vals[0]) for v in vals]
    denom = exps[0] + exps[1] + exps[2] + exps[3]
    gate_out = jnp.zeros(gate_ref.shape, F32)
    for k in range(TOP_K):
        gate_out = jnp.where(out_lane == k, exps[k] / denom, gate_out)
    gate_ref[...] = gate_out

    member = (sels[0] | sels[1] | sels[2] | sels[3])
    before = jnp.dot(lt_ref[...], member.astype(BF16), preferred_element_type=F32)
    pos = carry_ref[...] + before
    rank_out = jnp.zeros(rank_ref.shape, jnp.int32)
    for k in range(TOP_K):
        r = jnp.sum(jnp.where(sels[k], pos, 0.0), axis=1, keepdims=True)
        rank_out = jnp.where(out_lane == k, r.astype(jnp.int32), rank_out)
    rank_ref[...] = rank_out

    carry_ref[...] = carry_ref[...] + jnp.sum(member.astype(F32), axis=0, keepdims=True)
    cnt_ref[...] = carry_ref[...]


def _router(h1, g2, w_router, b_router):
    t, d = h1.shape
    i = jnp.arange(ROUTER_TM)
    ltri = (i[None, :] < i[:, None]).astype(BF16)
    lanes = 128
    return pl.pallas_call(
        _router_kernel,
        grid=(t // ROUTER_TM,),
        in_specs=[
            pl.BlockSpec((ROUTER_TM, d), lambda i: (i, 0)),
            pl.BlockSpec((1, d), lambda i: (0, 0)),
            pl.BlockSpec((d, N_EXPERTS), lambda i: (0, 0)),
            pl.BlockSpec((1, N_EXPERTS), lambda i: (0, 0)),
            pl.BlockSpec((ROUTER_TM, ROUTER_TM), lambda i: (0, 0)),
        ],
        out_specs=[
            pl.BlockSpec((ROUTER_TM, d // 2), lambda i: (i, 0)),
            pl.BlockSpec((ROUTER_TM, lanes), lambda i: (i, 0)),
            pl.BlockSpec((ROUTER_TM, lanes), lambda i: (i, 0)),
            pl.BlockSpec((ROUTER_TM, lanes), lambda i: (i, 0)),
            pl.BlockSpec((1, N_EXPERTS), lambda i: (0, 0)),
        ],
        out_shape=[
            jax.ShapeDtypeStruct((t, d // 2), jnp.uint32),
            jax.ShapeDtypeStruct((t, lanes), jnp.int32),
            jax.ShapeDtypeStruct((t, lanes), jnp.int32),
            jax.ShapeDtypeStruct((t, lanes), F32),
            jax.ShapeDtypeStruct((1, N_EXPERTS), F32),
        ],
        scratch_shapes=[pltpu.VMEM((1, N_EXPERTS), F32)],
        compiler_params=pltpu.CompilerParams(
            dimension_semantics=("arbitrary",), vmem_limit_bytes=VMEM_LIMIT),
        name="router",
    )(h1, g2, w_router, b_router, ltri)


def _dispatch_kernel(dest_ref, xp_ref, xs_in_ref, xs_ref, sem):
    del xs_in_ref
    i = pl.program_id(0)

    def row_copy(tok, r):
        return pltpu.make_async_copy(xp_ref.at[pl.ds(tok, 1)], xs_ref.at[pl.ds(r, 1)], sem)

    def issue(s, _):
        tok = i * DISPATCH_TB + s
        for k in range(TOP_K):
            row_copy(tok, dest_ref[tok * TOP_K + k]).start()
        return 0

    def drain(s, _):
        for k in range(TOP_K):
            row_copy(0, 0).wait()
        return 0

    lax.fori_loop(0, DISPATCH_TB, issue, 0)
    lax.fori_loop(0, DISPATCH_TB, drain, 0)


def _dispatch(dest_flat, xp, n_rows):
    t, w = xp.shape
    xs0 = jnp.zeros((n_rows, w), jnp.uint32)
    return pl.pallas_call(
        _dispatch_kernel,
        grid_spec=pltpu.PrefetchScalarGridSpec(
            num_scalar_prefetch=1,
            grid=(t // DISPATCH_TB,),
            in_specs=[pl.BlockSpec(memory_space=pl.ANY), pl.BlockSpec(memory_space=pl.ANY)],
            out_specs=pl.BlockSpec(memory_space=pl.ANY),
            scratch_shapes=[pltpu.SemaphoreType.DMA(())],
        ),
        out_shape=jax.ShapeDtypeStruct((n_rows, w), jnp.uint32),
        input_output_aliases={2: 0},
        compiler_params=pltpu.CompilerParams(
            dimension_semantics=("arbitrary",), has_side_effects=True),
        name="dispatch",
    )(dest_flat, xp, xs0)


def _expert_kernel(ie_ref, it_ref, nv_ref, xs_ref, wg_ref, wu_ref, wd_ref, bg_ref, bu_ref, bd_ref,
                   y_ref, xb_ref, wgb_ref, wub_ref, wdb_ref):
    del ie_ref, it_ref
    i = pl.program_id(0)
    c = pl.program_id(1)
    nc = pl.num_programs(1)
    nv = nv_ref[i]
    half = xs_ref.shape[1]

    @pl.when(nv > 0)
    def _():
        @pl.when(c == 0)
        def _():
            w = xs_ref[...]
            xb_ref[:, :half] = lax.bitcast_convert_type(w << 16, F32).astype(BF16)
            xb_ref[:, half:] = lax.bitcast_convert_type(w & jnp.uint32(0xFFFF0000), F32).astype(BF16)

        wgb_ref[...] = wg_ref[0].astype(BF16)
        wub_ref[...] = wu_ref[0].astype(BF16)
        wdb_ref[...] = wd_ref[0].astype(BF16)

        for s in range(EXP_TM // EXP_SUB):
            rows = slice(s * EXP_SUB, (s + 1) * EXP_SUB)

            @pl.when(s * EXP_SUB < nv)
            def _():
                xb = xb_ref[rows, :]
                g = jnp.dot(xb, wgb_ref[...], preferred_element_type=F32) + bg_ref[0]
                u = jnp.dot(xb, wub_ref[...], preferred_element_type=F32) + bu_ref[0]
                g = jnp.minimum(g, SWIGLU_LIMIT)
                u = jnp.clip(u, -SWIGLU_LIMIT, SWIGLU_LIMIT)
                a = g * _sigmoid(SWIGLU_ALPHA * g) * (u + 1.0)
                y = jnp.dot(a.astype(BF16), wdb_ref[...], preferred_element_type=F32)

                @pl.when(c == 0)
                def _():
                    y_ref[rows, :] = y + bd_ref[0]

                @pl.when(c > 0)
                def _():
                    y_ref[rows, :] += y

    for s in range(EXP_TM // EXP_SUB):
        @pl.when((s * EXP_SUB >= nv) & (c == nc - 1))
        def _():
            y_ref[s * EXP_SUB:(s + 1) * EXP_SUB, :] = jnp.zeros((EXP_SUB, y_ref.shape[1]), F32)


def _expert(item_e, item_t, item_nv, xs, w_up, b_up, w_down, b_down):
    n_rows, half = xs.shape
    ne, d, two_f = w_up.shape
    f = two_f // 2
    nc = f // EXP_TC
    n_items = item_e.shape[0]
    b_up3 = b_up.reshape(ne, 1, two_f)
    b_down3 = b_down.reshape(ne, 1, d)

    def chunk(c, nv_ref, i):
        return jnp.where(nv_ref[i] > 0, c, nc - 1)

    return pl.pallas_call(
        _expert_kernel,
        grid_spec=pltpu.PrefetchScalarGridSpec(
            num_scalar_prefetch=3,
            grid=(n_items, nc),
            in_specs=[
                pl.BlockSpec((EXP_TM, half), lambda i, c, ie, it, nv: (it[i], 0)),
                pl.BlockSpec((1, d, EXP_TC), lambda i, c, ie, it, nv: (ie[i], 0, chunk(c, nv, i))),
                pl.BlockSpec((1, d, EXP_TC), lambda i, c, ie, it, nv: (ie[i], 0, nc + chunk(c, nv, i))),
                pl.BlockSpec((1, EXP_TC, d), lambda i, c, ie, it, nv: (ie[i], chunk(c, nv, i), 0)),
                pl.BlockSpec((1, 1, EXP_TC), lambda i, c, ie, it, nv: (ie[i], 0, chunk(c, nv, i))),
                pl.BlockSpec((1, 1, EXP_TC), lambda i, c, ie, it, nv: (ie[i], 0, nc + chunk(c, nv, i))),
                pl.BlockSpec((1, 1, d), lambda i, c, ie, it, nv: (ie[i], 0, 0)),
            ],
            out_specs=pl.BlockSpec((EXP_TM, d), lambda i, c, ie, it, nv: (i, 0)),
            scratch_shapes=[
                pltpu.VMEM((EXP_TM, d), BF16),
                pltpu.VMEM((d, EXP_TC), BF16),
                pltpu.VMEM((d, EXP_TC), BF16),
                pltpu.VMEM((EXP_TC, d), BF16),
            ],
        ),
        out_shape=jax.ShapeDtypeStruct((n_rows, d), F32),
        compiler_params=pltpu.CompilerParams(
            dimension_semantics=("arbitrary", "arbitrary"), vmem_limit_bytes=VMEM_LIMIT),
        name="expert",
    )(item_e, item_t, item_nv, xs, w_up, w_up, w_down, b_up3, b_up3, b_down3)


def _combine_kernel(dest_ref, y_ref, h_ref, gate_ref, o_ref, buf_ref, sem):
    i = pl.program_id(0)

    def row_copy(r, k, s):
        return pltpu.make_async_copy(y_ref.at[pl.ds(r, 1)], buf_ref.at[k, pl.ds(s, 1)], sem)

    def issue(s, _):
        tok = i * COMBINE_TB + s
        for k in range(TOP_K):
            row_copy(dest_ref[tok * TOP_K + k], k, s).start()
        return 0

    def drain(s, _):
        for k in range(TOP_K):
            row_copy(0, k, s).wait()
        return 0

    lax.fori_loop(0, COMBINE_TB, issue, 0)
    lax.fori_loop(0, COMBINE_TB, drain, 0)
    acc = h_ref[...]
    for k in range(TOP_K):
        acc = acc + gate_ref[:, k:k + 1] * buf_ref[k]
    o_ref[...] = acc


def _combine(dest_flat, y, h1, gate):
    t, d = h1.shape
    lanes = gate.shape[1]
    return pl.pallas_call(
        _combine_kernel,
        grid_spec=pltpu.PrefetchScalarGridSpec(
            num_scalar_prefetch=1,
            grid=(t // COMBINE_TB,),
            in_specs=[
                pl.BlockSpec(memory_space=pl.ANY),
                pl.BlockSpec((COMBINE_TB, d), lambda i, dest: (i, 0)),
                pl.BlockSpec((COMBINE_TB, lanes), lambda i, dest: (i, 0)),
            ],
            out_specs=pl.BlockSpec((COMBINE_TB, d), lambda i, dest: (i, 0)),
            scratch_shapes=[pltpu.VMEM((TOP_K, COMBINE_TB, d), F32), pltpu.SemaphoreType.DMA(())],
        ),
        out_shape=jax.ShapeDtypeStruct((t, d), F32),
        compiler_params=pltpu.CompilerParams(
            dimension_semantics=("arbitrary",), vmem_limit_bytes=VMEM_LIMIT),
        name="combine",
    )(dest_flat, y, h1, gate)


def _plan_items(counts, n_items_max):
    tiles = (counts + EXP_TM - 1) // EXP_TM
    tile_end = jnp.cumsum(tiles)
    tile_start = tile_end - tiles
    n_items = tile_end[-1]
    g = jnp.arange(n_items_max, dtype=jnp.int32)
    live = g < n_items
    gg = jnp.minimum(g, n_items - 1)
    e = jnp.minimum(jnp.searchsorted(tile_end, gg, side="right"), N_EXPERTS - 1).astype(jnp.int32)
    nv = jnp.clip(counts[e] - (gg - tile_start[e]) * EXP_TM, 0, EXP_TM)
    nv = jnp.where(live, nv, 0).astype(jnp.int32)
    return tile_start * EXP_TM, e, gg.astype(jnp.int32), nv


def _layer(x2, bsz, seq, norm1_gain, w_in, q_norm_gain, k_norm_gain, sgu_norm_gain, w_spatial, b_spatial,
           w_branch_a, w_branch_b, w_out, norm2_gain, w_router, b_router, w_up, b_up, w_down, b_down):
    t, d = x2.shape
    proj = _proj(x2, norm1_gain[None, :], w_in, q_norm_gain[None, :], k_norm_gain[None, :])
    oa = _attn(proj, bsz, seq, d)
    h1 = _merge(x2, oa, proj, w_spatial, b_spatial.T, sgu_norm_gain[None, :],
                w_branch_a.astype(BF16), w_branch_b.astype(BF16), w_out.astype(BF16))
    xp, idx, rank, gate, cnt = _router(h1, norm2_gain[None, :], w_router, b_router[None, :])

    counts = cnt[0].astype(jnp.int32)
    n_items_max = (t * TOP_K) // EXP_TM + N_EXPERTS
    seg_start, item_e, item_t, item_nv = _plan_items(counts, n_items_max)
    dest = (seg_start[idx[:, :TOP_K]] + rank[:, :TOP_K]).astype(jnp.int32).reshape(-1)

    xs = _dispatch(dest, xp, n_items_max * EXP_TM)
    y = _expert(item_e, item_t, item_nv, xs, w_up, b_up, w_down, b_down)
    return _combine(dest, y, h1, gate)


def kernel(x, norm1_gain, w_in, q_norm_gain, k_norm_gain, sgu_norm_gain, w_spatial, b_spatial, w_branch_a,
           w_branch_b, w_out, norm2_gain, w_router, b_router, w_up, b_up, w_down, b_down):
    bsz, seq, d = x.shape
    h = x.reshape(bsz * seq, d)
    for l in range(norm1_gain.shape[0]):
        h = _layer(h, bsz, seq, norm1_gain[l], w_in[l], q_norm_gain[l], k_norm_gain[l], sgu_norm_gain[l],
                   w_spatial[l], b_spatial[l], w_branch_a[l], w_branch_b[l], w_out[l], norm2_gain[l],
                   w_router[l], b_router[l], w_up[l], b_up[l], w_down[l], b_down[l])
    return h.reshape(bsz, seq, d)
```

```python
import math

import jax
import jax.numpy as jnp
from jax import lax
from jax.experimental import pallas as pl
from jax.experimental.pallas import tpu as pltpu

F32 = jnp.float32
BF16 = jnp.bfloat16

EPS = 1e-6
N_HEADS = 8
HEAD_DIM = 128
SGU_GROUPS = 8
SGU_BLOCK = 128
CHUNK = 64
N_EXPERTS = 32
TOP_K = 4
SWIGLU_LIMIT = 7.0
SWIGLU_ALPHA = 1.702

F32_EXP_ZERO_BELOW = -105.0

VMEM_LIMIT = 56 * 1024 * 1024
PROJ_TM = 1024
PROJ_TN = 512
PROJ_ROWS = 512
ATT_T = 256
ATT_HP = 4
MERGE_TM = 256
ROUTER_TM = 256
EXP_TM = 1024
EXP_SUB = 512
EXP_TC = 256
DISPATCH_TB = 512
COMBINE_TB = 128


def _sigmoid(x):
    return 1.0 / (1.0 + jnp.exp(-x))


def _gelu_tanh(x):
    c = math.sqrt(2.0 / math.pi)
    return 0.5 * x * (1.0 + jnp.tanh(c * (x + 0.044715 * (x * x * x))))


def _proj_kernel(x_ref, g1_ref, w_ref, qg_ref, kg_ref, o_ref, xn_ref):
    j = pl.program_id(1)
    n_q = (N_HEADS * HEAD_DIM) // PROJ_TN

    @pl.when(j == 0)
    def _():
        x = x_ref[...]
        ms = jnp.mean(x * x, axis=-1, keepdims=True)
        xn_ref[...] = (x * lax.rsqrt(ms + EPS) * g1_ref[...]).astype(BF16)

    def head_norm(a):
        gain = jnp.where(j < n_q, qg_ref[...], kg_ref[...])
        outs = []
        for h in range(PROJ_TN // HEAD_DIM):
            ah = a[:, h * HEAD_DIM:(h + 1) * HEAD_DIM]
            ms = jnp.mean(ah * ah, axis=-1, keepdims=True)
            outs.append(ah * lax.rsqrt(ms + EPS) * gain)
        return jnp.concatenate(outs, axis=1)

    def tile(epilogue):
        w = w_ref[...].astype(BF16)
        for r in range(PROJ_TM // PROJ_ROWS):
            rows = slice(r * PROJ_ROWS, (r + 1) * PROJ_ROWS)
            acc = jnp.dot(xn_ref[rows, :], w, preferred_element_type=F32)
            o_ref[rows, :] = epilogue(acc).astype(BF16)

    pl.when(j < 2 * n_q)(lambda: tile(head_norm))
    pl.when((j >= 2 * n_q) & (j < 3 * n_q))(lambda: tile(lambda a: a))
    pl.when((j >= 3 * n_q) & (j < 5 * n_q))(lambda: tile(_gelu_tanh))
    pl.when(j >= 5 * n_q)(lambda: tile(_sigmoid))


def _proj(x2, g1, w_in, qg, kg):
    t, d = x2.shape
    n = w_in.shape[1]
    nj = n // PROJ_TN
    shift = (2 * d) // PROJ_TN
    return pl.pallas_call(
        _proj_kernel,
        grid=(t // PROJ_TM, nj),
        in_specs=[
            pl.BlockSpec((PROJ_TM, d), lambda i, j: (i, 0)),
            pl.BlockSpec((1, d), lambda i, j: (0, 0)),
            pl.BlockSpec((d, PROJ_TN), lambda i, j: (0, j)),
            pl.BlockSpec((1, HEAD_DIM), lambda i, j: (0, 0)),
            pl.BlockSpec((1, HEAD_DIM), lambda i, j: (0, 0)),
        ],
        out_specs=pl.BlockSpec((PROJ_TM, PROJ_TN), lambda i, j: (i, (j + shift) % nj)),
        out_shape=jax.ShapeDtypeStruct((t, n), BF16),
        scratch_shapes=[pltpu.VMEM((PROJ_TM, d), BF16)],
        compiler_params=pltpu.CompilerParams(
            dimension_semantics=("arbitrary", "arbitrary"), vmem_limit_bytes=VMEM_LIMIT),
        name="proj",
    )(x2, g1, w_in, qg, kg)


def _attn_kernel(stop_ref, q_ref, k_ref, v_ref, u_ref, o_ref):
    qi = pl.program_id(2)
    scale = HEAD_DIM ** -0.5
    stop_at = stop_ref[0]

    def block(j, accs, runs, diag):
        start = pl.multiple_of(j * ATT_T, ATT_T)
        tri = u_ref[...]
        if diag:
            row = lax.broadcasted_iota(jnp.int32, (ATT_T, ATT_T), 0)
            col = lax.broadcasted_iota(jnp.int32, (ATT_T, ATT_T), 1)
            past = col < row
        new_accs, new_runs = [], []
        for h in range(ATT_HP):
            cols = slice(h * HEAD_DIM, (h + 1) * HEAD_DIM)
            q = q_ref[:, cols]
            kb = k_ref[pl.ds(start, ATT_T), cols]
            vb = v_ref[pl.ds(start, ATT_T), cols]
            z = lax.dot_general(q, kb, (((1,), (1,)), ((), ())), preferred_element_type=F32) * scale
            sp = jnp.maximum(z, 0.0) + jnp.log1p(jnp.exp(-jnp.abs(z)))
            if diag:
                sp = jnp.where(past, sp, 0.0)
            hi = sp.astype(BF16)
            lo = (sp - hi.astype(F32)).astype(BF16)
            cs = (jnp.dot(hi, tri, preferred_element_type=F32)
                  + jnp.dot(lo, tri, preferred_element_type=F32))
            w = jnp.exp(z - (runs[h] + cs))
            if diag:
                w = jnp.where(past, w, 0.0)
            new_accs.append(accs[h] + jnp.dot(w.astype(BF16), vb, preferred_element_type=F32))
            new_runs.append(runs[h] + cs[:, 0:1])
        return tuple(new_accs), tuple(new_runs)

    def smallest(runs):
        m = runs[0]
        for r in runs[1:]:
            m = jnp.minimum(m, r)
        return jnp.min(m)

    accs = tuple(jnp.zeros((ATT_T, HEAD_DIM), F32) for _ in range(ATT_HP))
    runs = tuple(jnp.zeros((ATT_T, 1), F32) for _ in range(ATT_HP))
    accs, runs = block(qi, accs, runs, True)

    def cond(c):
        j, _, _, low = c
        return (j >= 0) & (low < stop_at)

    def body(c):
        j, accs, runs, _ = c
        accs, runs = block(j, accs, runs, False)
        return j - 1, accs, runs, smallest(runs)

    _, accs, _, _ = lax.while_loop(cond, body, (qi - 1, accs, runs, smallest(runs)))
    for h in range(ATT_HP):
        o_ref[:, h * HEAD_DIM:(h + 1) * HEAD_DIM] = accs[h].astype(BF16)


def _attn(proj, bsz, seq, d, q_gain, k_gain):
    t = bsz * seq
    nq = seq // ATT_T
    width = ATT_HP * HEAD_DIM
    col0 = (2 * d) // width
    seg = (N_HEADS * HEAD_DIM) // width
    i = jnp.arange(ATT_T)
    tri = (i[:, None] >= i[None, :]).astype(BF16)
    zmax = 1.02 * math.sqrt(HEAD_DIM) * jnp.max(jnp.abs(q_gain)) * jnp.max(jnp.abs(k_gain))
    stop_at = (zmax - F32_EXP_ZERO_BELOW).reshape(1).astype(F32)
    return pl.pallas_call(
        _attn_kernel,
        grid_spec=pltpu.PrefetchScalarGridSpec(
            num_scalar_prefetch=1,
            grid=(bsz, seg, nq),
            in_specs=[
                pl.BlockSpec((ATT_T, width), lambda b, h, qi, s: (b * nq + qi, col0 + h)),
                pl.BlockSpec((seq, width), lambda b, h, qi, s: (b, col0 + seg + h)),
                pl.BlockSpec((seq, width), lambda b, h, qi, s: (b, col0 + 2 * seg + h)),
                pl.BlockSpec((ATT_T, ATT_T), lambda b, h, qi, s: (0, 0)),
            ],
            out_specs=pl.BlockSpec((ATT_T, width), lambda b, h, qi, s: (b * nq + qi, h)),
        ),
        out_shape=jax.ShapeDtypeStruct((t, N_HEADS * HEAD_DIM), BF16),
        compiler_params=pltpu.CompilerParams(
            dimension_semantics=("arbitrary", "arbitrary", "arbitrary"), vmem_limit_bytes=VMEM_LIMIT),
        name="attn",
    )(stop_at, proj, proj, proj, tri)


def _merge_kernel(x_ref, oa_ref, u_ref, vv_ref, ga_ref, gb_ref, ws_ref, bst_ref, sg_ref,
                  wa_ref, wb_ref, wo_ref, h_ref, vln_ref, ob_ref):
    vv = vv_ref[...].astype(F32)
    mu = jnp.mean(vv, axis=-1, keepdims=True)
    xc = vv - mu
    var = jnp.mean(xc * xc, axis=-1, keepdims=True)
    vln_ref[...] = (xc * lax.rsqrt(var + EPS) * sg_ref[...]).astype(BF16)

    row = lax.broadcasted_iota(jnp.int32, (SGU_BLOCK, SGU_BLOCK), 0)
    col = lax.broadcasted_iota(jnp.int32, (SGU_BLOCK, SGU_BLOCK), 1)
    mask = (col // CHUNK) <= (row // CHUNK)
    for g in range(SGU_GROUPS):
        wg = jnp.where(mask, ws_ref[g], 0.0).astype(BF16)
        bg = bst_ref[:, g:g + 1]
        cs = slice(g * SGU_BLOCK, (g + 1) * SGU_BLOCK)
        for n in range(MERGE_TM // SGU_BLOCK):
            rs = slice(n * SGU_BLOCK, (n + 1) * SGU_BLOCK)
            sgu = jnp.dot(wg, vln_ref[rs, cs], preferred_element_type=F32) + bg
            ob_ref[rs, cs] = (u_ref[rs, cs].astype(F32) * sgu).astype(BF16)

    ya = jnp.dot(oa_ref[...], wa_ref[...], preferred_element_type=F32)
    yb = jnp.dot(ob_ref[...], wb_ref[...], preferred_element_type=F32)
    y = ga_ref[...].astype(F32) * ya + gb_ref[...].astype(F32) * yb
    h_ref[...] = x_ref[...] + jnp.dot(y.astype(BF16), wo_ref[...], preferred_element_type=F32)


def _merge(x2, oa, proj, w_spatial, b_spatial_t, sgu_gain, wa, wb, wo):
    t, d = x2.shape
    aw = oa.shape[1]
    sw = SGU_GROUPS * SGU_BLOCK
    ucol = (2 * d + 3 * aw) // sw
    const = lambda shape: pl.BlockSpec(shape, lambda i: (0,) * len(shape), pipeline_mode=pl.Buffered(1))
    return pl.pallas_call(
        _merge_kernel,
        grid=(t // MERGE_TM,),
        in_specs=[
            pl.BlockSpec((MERGE_TM, d), lambda i: (i, 0)),
            pl.BlockSpec((MERGE_TM, aw), lambda i: (i, 0)),
            pl.BlockSpec((MERGE_TM, sw), lambda i: (i, ucol)),
            pl.BlockSpec((MERGE_TM, sw), lambda i: (i, ucol + 1)),
            pl.BlockSpec((MERGE_TM, d), lambda i: (i, 0)),
            pl.BlockSpec((MERGE_TM, d), lambda i: (i, 1)),
            const((SGU_GROUPS, SGU_BLOCK, SGU_BLOCK)),
            const((SGU_BLOCK, SGU_GROUPS)),
            const((1, sw)),
            const((aw, d)),
            const((sw, d)),
            const((d, d)),
        ],
        out_specs=pl.BlockSpec((MERGE_TM, d), lambda i: (i, 0)),
        out_shape=jax.ShapeDtypeStruct((t, d), F32),
        scratch_shapes=[pltpu.VMEM((MERGE_TM, sw), BF16), pltpu.VMEM((MERGE_TM, sw), BF16)],
        compiler_params=pltpu.CompilerParams(
            dimension_semantics=("arbitrary",), vmem_limit_bytes=VMEM_LIMIT),
        name="merge",
    )(x2, oa, proj, proj, proj, proj, w_spatial, b_spatial_t, sgu_gain, wa, wb, wo)


def _router_kernel(h_ref, g2_ref, wr_ref, br_ref, lt_ref, xp_ref, idx_ref, rank_ref, gate_ref, cnt_ref,
                   carry_ref):
    i = pl.program_id(0)
    half = h_ref.shape[1] // 2

    @pl.when(i == 0)
    def _():
        carry_ref[...] = jnp.zeros_like(carry_ref)

    h = h_ref[...]
    ms = jnp.mean(h * h, axis=-1, keepdims=True)
    xn = h * lax.rsqrt(ms + EPS) * g2_ref[...]
    bits = lax.bitcast_convert_type(xn.astype(BF16).astype(F32), jnp.uint32)
    xp_ref[...] = bits[:, half:] | (bits[:, :half] >> 16)

    logits = jnp.dot(xn, wr_ref[...], preferred_element_type=F32,
                     precision=lax.Precision.HIGHEST) + br_ref[...]
    lane = lax.broadcasted_iota(jnp.int32, logits.shape, 1).astype(F32)
    out_lane = lax.broadcasted_iota(jnp.int32, idx_ref.shape, 1)
    work = logits
    vals, sels = [], []
    idx_out = jnp.zeros(idx_ref.shape, jnp.int32)
    for k in range(TOP_K):
        m = jnp.max(work, axis=1, keepdims=True)
        idx = jnp.min(jnp.where(work == m, lane, float(N_EXPERTS)), axis=1, keepdims=True)
        sel = lane == idx
        work = jnp.where(sel, -jnp.inf, work)
        vals.append(m)
        sels.append(sel)
        idx_out = jnp.where(out_lane == k, idx.astype(jnp.int32), idx_out)
    idx_ref[...] = idx_out

    exps = [jnp.exp(v - vals[0]) for v in vals]
    denom = exps[0] + exps[1] + exps[2] + exps[3]
    gate_out = jnp.zeros(gate_ref.shape, F32)
    for k in range(TOP_K):
        gate_out = jnp.where(out_lane == k, exps[k] / denom, gate_out)
    gate_ref[...] = gate_out

    member = (sels[0] | sels[1] | sels[2] | sels[3])
    before = jnp.dot(lt_ref[...], member.astype(BF16), preferred_element_type=F32)
    pos = carry_ref[...] + before
    rank_out = jnp.zeros(rank_ref.shape, jnp.int32)
    for k in range(TOP_K):
        r = jnp.sum(jnp.where(sels[k], pos, 0.0), axis=1, keepdims=True)
        rank_out = jnp.where(out_lane == k, r.astype(jnp.int32), rank_out)
    rank_ref[...] = rank_out

    carry_ref[...] = carry_ref[...] + jnp.sum(member.astype(F32), axis=0, keepdims=True)
    cnt_ref[...] = carry_ref[...]


def _router(h1, g2, w_router, b_router):
    t, d = h1.shape
    i = jnp.arange(ROUTER_TM)
    ltri = (i[None, :] < i[:, None]).astype(BF16)
    lanes = 128
    return pl.pallas_call(
        _router_kernel,
        grid=(t // ROUTER_TM,),
        in_specs=[
            pl.BlockSpec((ROUTER_TM, d), lambda i: (i, 0)),
            pl.BlockSpec((1, d), lambda i: (0, 0)),
            pl.BlockSpec((d, N_EXPERTS), lambda i: (0, 0)),
            pl.BlockSpec((1, N_EXPERTS), lambda i: (0, 0)),
            pl.BlockSpec((ROUTER_TM, ROUTER_TM), lambda i: (0, 0)),
        ],
        out_specs=[
            pl.BlockSpec((ROUTER_TM, d // 2), lambda i: (i, 0)),
            pl.BlockSpec((ROUTER_TM, lanes), lambda i: (i, 0)),
            pl.BlockSpec((ROUTER_TM, lanes), lambda i: (i, 0)),
            pl.BlockSpec((ROUTER_TM, lanes), lambda i: (i, 0)),
            pl.BlockSpec((1, N_EXPERTS), lambda i: (0, 0)),
        ],
        out_shape=[
            jax.ShapeDtypeStruct((t, d // 2), jnp.uint32),
            jax.ShapeDtypeStruct((t, lanes), jnp.int32),
            jax.ShapeDtypeStruct((t, lanes), jnp.int32),
            jax.ShapeDtypeStruct((t, lanes), F32),
            jax.ShapeDtypeStruct((1, N_EXPERTS), F32),
        ],
        scratch_shapes=[pltpu.VMEM((1, N_EXPERTS), F32)],
        compiler_params=pltpu.CompilerParams(
            dimension_semantics=("arbitrary",), vmem_limit_bytes=VMEM_LIMIT),
        name="router",
    )(h1, g2, w_router, b_router, ltri)


def _dispatch_kernel(dest_ref, xp_ref, xs_in_ref, xs_ref, sem):
    del xs_in_ref
    i = pl.program_id(0)

    def row_copy(s, r):
        return pltpu.make_async_copy(xp_ref.at[pl.ds(s, 1)], xs_ref.at[pl.ds(r, 1)], sem)

    def issue(s, _):
        tok = i * DISPATCH_TB + s
        for k in range(TOP_K):
            row_copy(s, dest_ref[tok * TOP_K + k]).start()
        return 0

    def drain(s, _):
        for k in range(TOP_K):
            row_copy(s, 0).wait()
        return 0

    lax.fori_loop(0, DISPATCH_TB, issue, 0)
    lax.fori_loop(0, DISPATCH_TB, drain, 0)


def _dispatch(dest_flat, xp, n_rows):
    t, w = xp.shape
    xs0 = jnp.zeros((n_rows, w), jnp.uint32)
    return pl.pallas_call(
        _dispatch_kernel,
        grid_spec=pltpu.PrefetchScalarGridSpec(
            num_scalar_prefetch=1,
            grid=(t // DISPATCH_TB,),
            in_specs=[pl.BlockSpec((DISPATCH_TB, w), lambda i, dest: (i, 0)),
                      pl.BlockSpec(memory_space=pl.ANY)],
            out_specs=pl.BlockSpec(memory_space=pl.ANY),
            scratch_shapes=[pltpu.SemaphoreType.DMA(())],
        ),
        out_shape=jax.ShapeDtypeStruct((n_rows, w), jnp.uint32),
        input_output_aliases={2: 0},
        compiler_params=pltpu.CompilerParams(
            dimension_semantics=("arbitrary",), has_side_effects=True),
        name="dispatch",
    )(dest_flat, xp, xs0)


def _expert_kernel(ie_ref, it_ref, nv_ref, xs_ref, wg_ref, wu_ref, wd_ref, bg_ref, bu_ref, bd_ref,
                   y_ref, xb_ref):
    del ie_ref, it_ref
    i = pl.program_id(0)
    c = pl.program_id(1)
    nv = nv_ref[i]
    half = xs_ref.shape[1]

    @pl.when(c == 0)
    def _():
        y_ref[...] = jnp.broadcast_to(bd_ref[0], y_ref.shape)

    @pl.when((c == 0) & (nv > 0))
    def _():
        w = xs_ref[...]
        xb_ref[:, :half] = lax.bitcast_convert_type(w << 16, F32).astype(BF16)
        xb_ref[:, half:] = lax.bitcast_convert_type(w & jnp.uint32(0xFFFF0000), F32).astype(BF16)

    def run(n_sub):
        wg = wg_ref[0].astype(BF16)
        wu = wu_ref[0].astype(BF16)
        wd = wd_ref[0].astype(BF16)
        for s in range(n_sub):
            rows = slice(s * EXP_SUB, (s + 1) * EXP_SUB)
            xb = xb_ref[rows, :]
            g = jnp.dot(xb, wg, preferred_element_type=F32) + bg_ref[0]
            u = jnp.dot(xb, wu, preferred_element_type=F32) + bu_ref[0]
            g = jnp.minimum(g, SWIGLU_LIMIT)
            u = jnp.clip(u, -SWIGLU_LIMIT, SWIGLU_LIMIT)
            a = g * _sigmoid(SWIGLU_ALPHA * g) * (u + 1.0)
            y_ref[rows, :] += jnp.dot(a.astype(BF16), wd, preferred_element_type=F32)

    n_sub_max = EXP_TM // EXP_SUB
    for n_sub in range(1, n_sub_max + 1):
        lo = (n_sub - 1) * EXP_SUB
        pl.when((nv > lo) & (nv <= lo + EXP_SUB))(lambda n_sub=n_sub: run(n_sub))


def _expert(item_e, item_t, item_nv, xs, w_up, b_up, w_down, b_down):
    n_rows, half = xs.shape
    ne, d, two_f = w_up.shape
    f = two_f // 2
    nc = f // EXP_TC
    n_items = item_e.shape[0]
    b_up3 = b_up.reshape(ne, 1, two_f)
    b_down3 = b_down.reshape(ne, 1, d)

    def chunk(c, nv_ref, i):
        return jnp.where(nv_ref[i] > 0, c, nc - 1)

    return pl.pallas_call(
        _expert_kernel,
        grid_spec=pltpu.PrefetchScalarGridSpec(
            num_scalar_prefetch=3,
            grid=(n_items, nc),
            in_specs=[
                pl.BlockSpec((EXP_TM, half), lambda i, c, ie, it, nv: (it[i], 0)),
                pl.BlockSpec((1, d, EXP_TC), lambda i, c, ie, it, nv: (ie[i], 0, chunk(c, nv, i))),
                pl.BlockSpec((1, d, EXP_TC), lambda i, c, ie, it, nv: (ie[i], 0, nc + chunk(c, nv, i))),
                pl.BlockSpec((1, EXP_TC, d), lambda i, c, ie, it, nv: (ie[i], chunk(c, nv, i), 0)),
                pl.BlockSpec((1, 1, EXP_TC), lambda i, c, ie, it, nv: (ie[i], 0, chunk(c, nv, i))),
                pl.BlockSpec((1, 1, EXP_TC), lambda i, c, ie, it, nv: (ie[i], 0, nc + chunk(c, nv, i))),
                pl.BlockSpec((1, 1, d), lambda i, c, ie, it, nv: (ie[i], 0, 0)),
            ],
            out_specs=pl.BlockSpec((EXP_TM, d), lambda i, c, ie, it, nv: (i, 0)),
            scratch_shapes=[pltpu.VMEM((EXP_TM, d), BF16)],
        ),
        out_shape=jax.ShapeDtypeStruct((n_rows, d), F32),
        compiler_params=pltpu.CompilerParams(
            dimension_semantics=("arbitrary", "arbitrary"), vmem_limit_bytes=VMEM_LIMIT),
        name="expert",
    )(item_e, item_t, item_nv, xs, w_up, w_up, w_down, b_up3, b_up3, b_down3)


def _combine_kernel(dest_ref, y_ref, h_ref, gate_ref, o_ref, buf_ref, sem):
    i = pl.program_id(0)

    def row_copy(r, k, s):
        return pltpu.make_async_copy(y_ref.at[pl.ds(r, 1)], buf_ref.at[k, pl.ds(s, 1)], sem)

    def issue(s, _):
        tok = i * COMBINE_TB + s
        for k in range(TOP_K):
            row_copy(dest_ref[tok * TOP_K + k], k, s).start()
        return 0

    def drain(s, _):
        for k in range(TOP_K):
            row_copy(0, k, s).wait()
        return 0

    lax.fori_loop(0, COMBINE_TB, issue, 0)
    lax.fori_loop(0, COMBINE_TB, drain, 0)
    acc = h_ref[...]
    for k in range(TOP_K):
        acc = acc + gate_ref[:, k:k + 1] * buf_ref[k]
    o_ref[...] = acc


def _combine(dest_flat, y, h1, gate):
    t, d = h1.shape
    lanes = gate.shape[1]
    return pl.pallas_call(
        _combine_kernel,
        grid_spec=pltpu.PrefetchScalarGridSpec(
            num_scalar_prefetch=1,
            grid=(t // COMBINE_TB,),
            in_specs=[
                pl.BlockSpec(memory_space=pl.ANY),
                pl.BlockSpec((COMBINE_TB, d), lambda i, dest: (i, 0)),
                pl.BlockSpec((COMBINE_TB, lanes), lambda i, dest: (i, 0)),
            ],
            out_specs=pl.BlockSpec((COMBINE_TB, d), lambda i, dest: (i, 0)),
            scratch_shapes=[pltpu.VMEM((TOP_K, COMBINE_TB, d), F32), pltpu.SemaphoreType.DMA(())],
        ),
        out_shape=jax.ShapeDtypeStruct((t, d), F32),
        compiler_params=pltpu.CompilerParams(
            dimension_semantics=("arbitrary",), vmem_limit_bytes=VMEM_LIMIT),
        name="combine",
    )(dest_flat, y, h1, gate)


def _plan_items(counts, n_items_max):
    tiles = (counts + EXP_TM - 1) // EXP_TM
    tile_end = jnp.cumsum(tiles)
    tile_start = tile_end - tiles
    n_items = tile_end[-1]
    g = jnp.arange(n_items_max, dtype=jnp.int32)
    live = g < n_items
    gg = jnp.minimum(g, n_items - 1)
    e = jnp.minimum(jnp.searchsorted(tile_end, gg, side="right"), N_EXPERTS - 1).astype(jnp.int32)
    nv = jnp.clip(counts[e] - (gg - tile_start[e]) * EXP_TM, 0, EXP_TM)
    nv = jnp.where(live, nv, 0).astype(jnp.int32)
    return tile_start * EXP_TM, e, gg.astype(jnp.int32), nv


def _layer(x2, bsz, seq, norm1_gain, w_in, q_norm_gain, k_norm_gain, sgu_norm_gain, w_spatial, b_spatial,
           w_branch_a, w_branch_b, w_out, norm2_gain, w_router, b_router, w_up, b_up, w_down, b_down):
    t, d = x2.shape
    proj = _proj(x2, norm1_gain[None, :], w_in, q_norm_gain[None, :], k_norm_gain[None, :])
    oa = _attn(proj, bsz, seq, d, q_norm_gain, k_norm_gain)
    h1 = _merge(x2, oa, proj, w_spatial, b_spatial.T, sgu_norm_gain[None, :],
                w_branch_a.astype(BF16), w_branch_b.astype(BF16), w_out.astype(BF16))
    xp, idx, rank, gate, cnt = _router(h1, norm2_gain[None, :], w_router, b_router[None, :])

    counts = cnt[0].astype(jnp.int32)
    n_items_max = (t * TOP_K) // EXP_TM + N_EXPERTS
    seg_start, item_e, item_t, item_nv = _plan_items(counts, n_items_max)
    dest = (seg_start[idx[:, :TOP_K]] + rank[:, :TOP_K]).astype(jnp.int32).reshape(-1)

    xs = _dispatch(dest, xp, n_items_max * EXP_TM)
    y = _expert(item_e, item_t, item_nv, xs, w_up, b_up, w_down, b_down)
    return _combine(dest, y, h1, gate)


def kernel(x, norm1_gain, w_in, q_norm_gain, k_norm_gain, sgu_norm_gain, w_spatial, b_spatial, w_branch_a,
           w_branch_b, w_out, norm2_gain, w_router, b_router, w_up, b_up, w_down, b_down):
    bsz, seq, d = x.shape
    h = x.reshape(bsz * seq, d)
    for l in range(norm1_gain.shape[0]):
        h = _layer(h, bsz, seq, norm1_gain[l], w_in[l], q_norm_gain[l], k_norm_gain[l], sgu_norm_gain[l],
                   w_spatial[l], b_spatial[l], w_branch_a[l], w_branch_b[l], w_out[l], norm2_gain[l],
                   w_router[l], b_router[l], w_up[l], b_up[l], w_down[l], b_down[l])
    return h.reshape(bsz, seq, d)
```

```python
import math

import jax
import jax.numpy as jnp
from jax import lax
from jax.experimental import pallas as pl
from jax.experimental.pallas import tpu as pltpu

F32 = jnp.float32
BF16 = jnp.bfloat16

EPS = 1e-6
N_HEADS = 8
HEAD_DIM = 128
SGU_GROUPS = 8
SGU_BLOCK = 128
CHUNK = 64
N_EXPERTS = 32
TOP_K = 4
SWIGLU_LIMIT = 7.0
SWIGLU_ALPHA = 1.702

F32_EXP_ZERO_BELOW = -105.0

VMEM_LIMIT = 56 * 1024 * 1024
PROJ_TM = 1024
PROJ_TN = 512
PROJ_ROWS = 512
ATT_T = 256
ATT_HP = 4
MERGE_TM = 256
EXP_TM = 1280
EXP_SUB = 512
EXP_GRAN = 256
EXP_TC = 256
DISPATCH_TB = 512
COMBINE_TB = 128
DMA_UNROLL = 8


def _sigmoid(x):
    return 1.0 / (1.0 + jnp.exp(-x))


def _gelu_tanh(x):
    c = math.sqrt(2.0 / math.pi)
    return 0.5 * x * (1.0 + jnp.tanh(c * (x + 0.044715 * (x * x * x))))


def _proj_kernel(x_ref, g1_ref, w_ref, qg_ref, kg_ref, o_ref, xn_ref):
    j = pl.program_id(1)
    n_q = (N_HEADS * HEAD_DIM) // PROJ_TN

    @pl.when(j == 0)
    def _():
        x = x_ref[...]
        ms = jnp.mean(x * x, axis=-1, keepdims=True)
        xn_ref[...] = (x * lax.rsqrt(ms + EPS) * g1_ref[...]).astype(BF16)

    def head_norm(a):
        gain = jnp.where(j < n_q, qg_ref[...], kg_ref[...])
        outs = []
        for h in range(PROJ_TN // HEAD_DIM):
            ah = a[:, h * HEAD_DIM:(h + 1) * HEAD_DIM]
            ms = jnp.mean(ah * ah, axis=-1, keepdims=True)
            outs.append(ah * lax.rsqrt(ms + EPS) * gain)
        return jnp.concatenate(outs, axis=1)

    def tile(epilogue):
        w = w_ref[...].astype(BF16)
        for r in range(PROJ_TM // PROJ_ROWS):
            rows = slice(r * PROJ_ROWS, (r + 1) * PROJ_ROWS)
            acc = jnp.dot(xn_ref[rows, :], w, preferred_element_type=F32)
            o_ref[rows, :] = epilogue(acc).astype(BF16)

    pl.when(j < 2 * n_q)(lambda: tile(head_norm))
    pl.when((j >= 2 * n_q) & (j < 3 * n_q))(lambda: tile(lambda a: a))
    pl.when((j >= 3 * n_q) & (j < 5 * n_q))(lambda: tile(_gelu_tanh))
    pl.when(j >= 5 * n_q)(lambda: tile(_sigmoid))


def _proj(x2, g1, w_in, qg, kg):
    t, d = x2.shape
    n = w_in.shape[1]
    nj = n // PROJ_TN
    shift = (2 * d) // PROJ_TN
    return pl.pallas_call(
        _proj_kernel,
        grid=(t // PROJ_TM, nj),
        in_specs=[
            pl.BlockSpec((PROJ_TM, d), lambda i, j: (i, 0)),
            pl.BlockSpec((1, d), lambda i, j: (0, 0)),
            pl.BlockSpec((d, PROJ_TN), lambda i, j: (0, j)),
            pl.BlockSpec((1, HEAD_DIM), lambda i, j: (0, 0)),
            pl.BlockSpec((1, HEAD_DIM), lambda i, j: (0, 0)),
        ],
        out_specs=pl.BlockSpec((PROJ_TM, PROJ_TN), lambda i, j: (i, (j + shift) % nj)),
        out_shape=jax.ShapeDtypeStruct((t, n), BF16),
        scratch_shapes=[pltpu.VMEM((PROJ_TM, d), BF16)],
        compiler_params=pltpu.CompilerParams(
            dimension_semantics=("arbitrary", "arbitrary"), vmem_limit_bytes=VMEM_LIMIT),
        name="proj",
    )(x2, g1, w_in, qg, kg)


def _attn_kernel(stop_ref, q_ref, k_ref, v_ref, u_ref, o_ref):
    qi = pl.program_id(2)
    scale = HEAD_DIM ** -0.5
    stop_at = stop_ref[0]

    def block(j, accs, runs, diag):
        start = pl.multiple_of(j * ATT_T, ATT_T)
        tri = u_ref[...]
        if diag:
            row = lax.broadcasted_iota(jnp.int32, (ATT_T, ATT_T), 0)
            col = lax.broadcasted_iota(jnp.int32, (ATT_T, ATT_T), 1)
            past = col < row
        heads = range(ATT_HP)
        cols = [slice(h * HEAD_DIM, (h + 1) * HEAD_DIM) for h in heads]
        zs = [lax.dot_general(q_ref[:, cols[h]], k_ref[pl.ds(start, ATT_T), cols[h]],
                              (((1,), (1,)), ((), ())), preferred_element_type=F32) * scale for h in heads]
        sps = [jnp.maximum(z, 0.0) + jnp.log1p(jnp.exp(-jnp.abs(z))) for z in zs]
        if diag:
            sps = [jnp.where(past, sp, 0.0) for sp in sps]
        his = [sp.astype(BF16) for sp in sps]
        los = [(sp - hi.astype(F32)).astype(BF16) for sp, hi in zip(sps, his)]
        css = [jnp.dot(hi, tri, preferred_element_type=F32) + jnp.dot(lo, tri, preferred_element_type=F32)
               for hi, lo in zip(his, los)]
        ws = [jnp.exp(zs[h] - (runs[h] + css[h])) for h in heads]
        if diag:
            ws = [jnp.where(past, w, 0.0) for w in ws]
        new_accs = [accs[h] + jnp.dot(ws[h].astype(BF16), v_ref[pl.ds(start, ATT_T), cols[h]],
                                      preferred_element_type=F32) for h in heads]
        new_runs = [runs[h] + css[h][:, 0:1] for h in heads]
        return tuple(new_accs), tuple(new_runs)

    def smallest(runs):
        m = runs[0]
        for r in runs[1:]:
            m = jnp.minimum(m, r)
        return jnp.min(m)

    accs = tuple(jnp.zeros((ATT_T, HEAD_DIM), F32) for _ in range(ATT_HP))
    runs = tuple(jnp.zeros((ATT_T, 1), F32) for _ in range(ATT_HP))
    accs, runs = block(qi, accs, runs, True)

    def cond(c):
        j, _, _, low = c
        return (j >= 0) & (low < stop_at)

    def body(c):
        j, accs, runs, _ = c
        accs, runs = block(j, accs, runs, False)
        return j - 1, accs, runs, smallest(runs)

    _, accs, _, _ = lax.while_loop(cond, body, (qi - 1, accs, runs, smallest(runs)))
    for h in range(ATT_HP):
        o_ref[:, h * HEAD_DIM:(h + 1) * HEAD_DIM] = accs[h].astype(BF16)


def _attn(proj, bsz, seq, d, q_gain, k_gain):
    t = bsz * seq
    nq = seq // ATT_T
    width = ATT_HP * HEAD_DIM
    col0 = (2 * d) // width
    seg = (N_HEADS * HEAD_DIM) // width
    i = jnp.arange(ATT_T)
    tri = (i[:, None] >= i[None, :]).astype(BF16)
    zmax = 1.02 * math.sqrt(HEAD_DIM) * jnp.max(jnp.abs(q_gain)) * jnp.max(jnp.abs(k_gain))
    stop_at = (zmax - F32_EXP_ZERO_BELOW).reshape(1).astype(F32)
    return pl.pallas_call(
        _attn_kernel,
        grid_spec=pltpu.PrefetchScalarGridSpec(
            num_scalar_prefetch=1,
            grid=(bsz, seg, nq),
            in_specs=[
                pl.BlockSpec((ATT_T, width), lambda b, h, qi, s: (b * nq + qi, col0 + h)),
                pl.BlockSpec((seq, width), lambda b, h, qi, s: (b, col0 + seg + h)),
                pl.BlockSpec((seq, width), lambda b, h, qi, s: (b, col0 + 2 * seg + h)),
                pl.BlockSpec((ATT_T, ATT_T), lambda b, h, qi, s: (0, 0)),
            ],
            out_specs=pl.BlockSpec((ATT_T, width), lambda b, h, qi, s: (b * nq + qi, h)),
        ),
        out_shape=jax.ShapeDtypeStruct((t, N_HEADS * HEAD_DIM), BF16),
        compiler_params=pltpu.CompilerParams(
            dimension_semantics=("arbitrary", "arbitrary", "arbitrary"), vmem_limit_bytes=VMEM_LIMIT),
        name="attn",
    )(stop_at, proj, proj, proj, tri)


def _merge_kernel(x_ref, oa_ref, u_ref, vv_ref, ga_ref, gb_ref, ws_ref, bst_ref, sg_ref,
                  wa_ref, wb_ref, wo_ref, g2_ref, wrt_ref, brt_ref, ut_ref,
                  h_ref, xp_ref, idx_ref, rank_ref, gate_ref, cnt_ref, vln_ref, ob_ref, carry_ref):
    @pl.when(pl.program_id(0) == 0)
    def _():
        carry_ref[...] = jnp.zeros_like(carry_ref)

    vv = vv_ref[...].astype(F32)
    mu = jnp.mean(vv, axis=-1, keepdims=True)
    xc = vv - mu
    var = jnp.mean(xc * xc, axis=-1, keepdims=True)
    vln_ref[...] = (xc * lax.rsqrt(var + EPS) * sg_ref[...]).astype(BF16)

    row = lax.broadcasted_iota(jnp.int32, (SGU_BLOCK, SGU_BLOCK), 0)
    col = lax.broadcasted_iota(jnp.int32, (SGU_BLOCK, SGU_BLOCK), 1)
    mask = (col // CHUNK) <= (row // CHUNK)
    for g in range(SGU_GROUPS):
        wg = jnp.where(mask, ws_ref[g], 0.0).astype(BF16)
        bg = bst_ref[:, g:g + 1]
        cs = slice(g * SGU_BLOCK, (g + 1) * SGU_BLOCK)
        for n in range(MERGE_TM // SGU_BLOCK):
            rs = slice(n * SGU_BLOCK, (n + 1) * SGU_BLOCK)
            sgu = jnp.dot(wg, vln_ref[rs, cs], preferred_element_type=F32) + bg
            ob_ref[rs, cs] = (u_ref[rs, cs].astype(F32) * sgu).astype(BF16)

    ya = jnp.dot(oa_ref[...], wa_ref[...], preferred_element_type=F32)
    yb = jnp.dot(ob_ref[...], wb_ref[...], preferred_element_type=F32)
    y = ga_ref[...].astype(F32) * ya + gb_ref[...].astype(F32) * yb
    h = x_ref[...] + jnp.dot(y.astype(BF16), wo_ref[...], preferred_element_type=F32)
    h_ref[...] = h

    half = h.shape[1] // 2
    ms = jnp.mean(h * h, axis=-1, keepdims=True)
    xn = h * lax.rsqrt(ms + EPS) * g2_ref[...]
    bits = lax.bitcast_convert_type(xn.astype(BF16).astype(F32), jnp.uint32)
    xp_ref[...] = bits[:, half:] | (bits[:, :half] >> 16)

    logits = lax.dot_general(wrt_ref[...], xn, (((1,), (1,)), ((), ())), preferred_element_type=F32,
                             precision=lax.Precision.HIGHEST) + brt_ref[...]
    sub = lax.broadcasted_iota(jnp.int32, logits.shape, 0).astype(F32)
    out_row = lax.broadcasted_iota(jnp.int32, idx_ref.shape, 0)
    work = logits
    vals, sels = [], []
    idx_out = jnp.zeros(idx_ref.shape, jnp.int32)
    for k in range(TOP_K):
        m = jnp.max(work, axis=0, keepdims=True)
        idx = jnp.min(jnp.where(work == m, sub, float(N_EXPERTS)), axis=0, keepdims=True)
        sel = sub == idx
        work = jnp.where(sel, -jnp.inf, work)
        vals.append(m)
        sels.append(sel)
        idx_out = jnp.where(out_row == k, idx.astype(jnp.int32), idx_out)
    idx_ref[...] = idx_out

    exps = [jnp.exp(v - vals[0]) for v in vals]
    denom = exps[0] + exps[1] + exps[2] + exps[3]
    gate_out = jnp.zeros(gate_ref.shape, F32)
    for k in range(TOP_K):
        gate_out = jnp.where(out_row == k, exps[k] / denom, gate_out)
    gate_ref[...] = gate_out

    member = (sels[0] | sels[1] | sels[2] | sels[3])
    before = jnp.dot(member.astype(BF16), ut_ref[...], preferred_element_type=F32)
    pos = carry_ref[...] + before
    rank_out = jnp.zeros(rank_ref.shape, jnp.int32)
    for k in range(TOP_K):
        r = jnp.sum(jnp.where(sels[k], pos, 0.0), axis=0, keepdims=True)
        rank_out = jnp.where(out_row == k, r.astype(jnp.int32), rank_out)
    rank_ref[...] = rank_out

    carry_ref[...] = carry_ref[...] + jnp.sum(member.astype(F32), axis=1, keepdims=True)
    cnt_ref[...] = jnp.broadcast_to(carry_ref[...], cnt_ref.shape)


def _merge(x2, oa, proj, w_spatial, b_spatial_t, sgu_gain, wa, wb, wo, g2, w_router_t, b_router_t):
    t, d = x2.shape
    aw = oa.shape[1]
    sw = SGU_GROUPS * SGU_BLOCK
    ucol = (2 * d + 3 * aw) // sw
    const = lambda shape: pl.BlockSpec(shape, lambda i: (0,) * len(shape), pipeline_mode=pl.Buffered(1))
    i = jnp.arange(MERGE_TM)
    ut = (i[:, None] < i[None, :]).astype(BF16)
    rows, lanes = 8, 128
    tok = lambda: pl.BlockSpec((rows, MERGE_TM), lambda i: (0, i))
    return pl.pallas_call(
        _merge_kernel,
        grid=(t // MERGE_TM,),
        in_specs=[
            pl.BlockSpec((MERGE_TM, d), lambda i: (i, 0)),
            pl.BlockSpec((MERGE_TM, aw), lambda i: (i, 0)),
            pl.BlockSpec((MERGE_TM, sw), lambda i: (i, ucol)),
            pl.BlockSpec((MERGE_TM, sw), lambda i: (i, ucol + 1)),
            pl.BlockSpec((MERGE_TM, d), lambda i: (i, 0)),
            pl.BlockSpec((MERGE_TM, d), lambda i: (i, 1)),
            const((SGU_GROUPS, SGU_BLOCK, SGU_BLOCK)),
            const((SGU_BLOCK, SGU_GROUPS)),
            const((1, sw)),
            const((aw, d)),
            const((sw, d)),
            const((d, d)),
            const((1, d)),
            const((N_EXPERTS, d)),
            const((N_EXPERTS, 1)),
            const((MERGE_TM, MERGE_TM)),
        ],
        out_specs=[
            pl.BlockSpec((MERGE_TM, d), lambda i: (i, 0)),
            pl.BlockSpec((MERGE_TM, d // 2), lambda i: (i, 0)),
            tok(), tok(), tok(),
            pl.BlockSpec((N_EXPERTS, lanes), lambda i: (0, 0)),
        ],
        out_shape=[
            jax.ShapeDtypeStruct((t, d), F32),
            jax.ShapeDtypeStruct((t, d // 2), jnp.uint32),
            jax.ShapeDtypeStruct((rows, t), jnp.int32),
            jax.ShapeDtypeStruct((rows, t), jnp.int32),
            jax.ShapeDtypeStruct((rows, t), F32),
            jax.ShapeDtypeStruct((N_EXPERTS, lanes), F32),
        ],
        scratch_shapes=[pltpu.VMEM((MERGE_TM, sw), BF16), pltpu.VMEM((MERGE_TM, sw), BF16),
                        pltpu.VMEM((N_EXPERTS, 1), F32)],
        compiler_params=pltpu.CompilerParams(
            dimension_semantics=("arbitrary",), vmem_limit_bytes=VMEM_LIMIT),
        name="merge",
    )(x2, oa, proj, proj, proj, proj, w_spatial, b_spatial_t, sgu_gain, wa, wb, wo,
      g2, w_router_t, b_router_t, ut)


def _dispatch_kernel(dest_ref, xp_ref, xs_in_ref, xs_ref, sem):
    del xs_in_ref
    i = pl.program_id(0)

    n_tok = dest_ref.shape[0] // TOP_K

    def row_copy(s, r):
        return pltpu.make_async_copy(xp_ref.at[pl.ds(s, 1)], xs_ref.at[pl.ds(r, 1)], sem)

    def issue(b, _):
        for j in range(DMA_UNROLL):
            s = b * DMA_UNROLL + j
            for k in range(TOP_K):
                row_copy(s, dest_ref[k * n_tok + i * DISPATCH_TB + s]).start(priority=k % 2)
        return 0

    def drain(b, _):
        for _ in range(DMA_UNROLL * TOP_K):
            row_copy(0, 0).wait()
        return 0

    lax.fori_loop(0, DISPATCH_TB // DMA_UNROLL, issue, 0)
    lax.fori_loop(0, DISPATCH_TB // DMA_UNROLL, drain, 0)


def _dispatch(dest_flat, xp, n_rows):
    t, w = xp.shape
    xs0 = jnp.zeros((n_rows, w), jnp.uint32)
    return pl.pallas_call(
        _dispatch_kernel,
        grid_spec=pltpu.PrefetchScalarGridSpec(
            num_scalar_prefetch=1,
            grid=(t // DISPATCH_TB,),
            in_specs=[pl.BlockSpec((DISPATCH_TB, w), lambda i, dest: (i, 0)),
                      pl.BlockSpec(memory_space=pl.ANY)],
            out_specs=pl.BlockSpec(memory_space=pl.ANY),
            scratch_shapes=[pltpu.SemaphoreType.DMA(())],
        ),
        out_shape=jax.ShapeDtypeStruct((n_rows, w), jnp.uint32),
        input_output_aliases={2: 0},
        compiler_params=pltpu.CompilerParams(
            dimension_semantics=("arbitrary",), has_side_effects=True),
        name="dispatch",
    )(dest_flat, xp, xs0)


def _expert_kernel(ie_ref, it_ref, nv_ref, xs_ref, wg_ref, wu_ref, wd_ref, bg_ref, bu_ref, bd_ref,
                   y_ref, xb_ref):
    del ie_ref, it_ref
    i = pl.program_id(0)
    c = pl.program_id(1)
    nv = nv_ref[i]
    half = xs_ref.shape[1]

    @pl.when(c == 0)
    def _():
        y_ref[...] = jnp.broadcast_to(bd_ref[0], y_ref.shape)

    @pl.when((c == 0) & (nv > 0))
    def _():
        w = xs_ref[...]
        xb_ref[:, :half] = lax.bitcast_convert_type(w << 16, F32).astype(BF16)
        xb_ref[:, half:] = lax.bitcast_convert_type(w & jnp.uint32(0xFFFF0000), F32).astype(BF16)

    def run(n_rows):
        wg = wg_ref[0].astype(BF16)
        wu = wu_ref[0].astype(BF16)
        wd = wd_ref[0].astype(BF16)
        for lo in range(0, n_rows, EXP_SUB):
            rows = slice(lo, min(lo + EXP_SUB, n_rows))
            xb = xb_ref[rows, :]
            g = jnp.dot(xb, wg, preferred_element_type=F32) + bg_ref[0]
            u = jnp.dot(xb, wu, preferred_element_type=F32) + bu_ref[0]
            g = jnp.minimum(g, SWIGLU_LIMIT)
            u = jnp.clip(u, -SWIGLU_LIMIT, SWIGLU_LIMIT)
            a = g * _sigmoid(SWIGLU_ALPHA * g) * (u + 1.0)
            y_ref[rows, :] += jnp.dot(a.astype(BF16), wd, preferred_element_type=F32)

    for n_rows in range(EXP_GRAN, EXP_TM + 1, EXP_GRAN):
        pl.when((nv > n_rows - EXP_GRAN) & (nv <= n_rows))(lambda n_rows=n_rows: run(n_rows))


def _expert(item_e, item_t, item_nv, xs, w_up, b_up, w_down, b_down):
    n_rows, half = xs.shape
    ne, d, two_f = w_up.shape
    f = two_f // 2
    nc = f // EXP_TC
    n_items = item_e.shape[0]
    b_up3 = b_up.reshape(ne, 1, two_f)
    b_down3 = b_down.reshape(ne, 1, d)

    def chunk(c, nv_ref, i):
        return jnp.where(nv_ref[i] > 0, c, nc - 1)

    return pl.pallas_call(
        _expert_kernel,
        grid_spec=pltpu.PrefetchScalarGridSpec(
            num_scalar_prefetch=3,
            grid=(n_items, nc),
            in_specs=[
                pl.BlockSpec((EXP_TM, half), lambda i, c, ie, it, nv: (it[i], 0)),
                pl.BlockSpec((1, d, EXP_TC), lambda i, c, ie, it, nv: (ie[i], 0, chunk(c, nv, i))),
                pl.BlockSpec((1, d, EXP_TC), lambda i, c, ie, it, nv: (ie[i], 0, nc + chunk(c, nv, i))),
                pl.BlockSpec((1, EXP_TC, d), lambda i, c, ie, it, nv: (ie[i], chunk(c, nv, i), 0)),
                pl.BlockSpec((1, 1, EXP_TC), lambda i, c, ie, it, nv: (ie[i], 0, chunk(c, nv, i))),
                pl.BlockSpec((1, 1, EXP_TC), lambda i, c, ie, it, nv: (ie[i], 0, nc + chunk(c, nv, i))),
                pl.BlockSpec((1, 1, d), lambda i, c, ie, it, nv: (ie[i], 0, 0)),
            ],
            out_specs=pl.BlockSpec((EXP_TM, d), lambda i, c, ie, it, nv: (i, 0)),
            scratch_shapes=[pltpu.VMEM((EXP_TM, d), BF16)],
        ),
        out_shape=jax.ShapeDtypeStruct((n_rows, d), F32),
        compiler_params=pltpu.CompilerParams(
            dimension_semantics=("arbitrary", "arbitrary"), vmem_limit_bytes=VMEM_LIMIT),
        name="expert",
    )(item_e, item_t, item_nv, xs, w_up, w_up, w_down, b_up3, b_up3, b_down3)


def _combine_kernel(dest_ref, y_ref, h_ref, gate_ref, o_ref, buf_ref, sem):
    i = pl.program_id(0)
    n = pl.num_programs(0)
    n_tok = dest_ref.shape[0] // TOP_K

    def row_copy(r, slot, k, s):
        return pltpu.make_async_copy(y_ref.at[pl.ds(r, 1)], buf_ref.at[slot, k, pl.ds(s, 1)], sem.at[slot])

    def gather(step, slot):
        def issue(b, _):
            for j in range(DMA_UNROLL):
                s = b * DMA_UNROLL + j
                for k in range(TOP_K):
                    row_copy(dest_ref[k * n_tok + step * COMBINE_TB + s], slot, k, s).start(priority=k % 2)
            return 0
        lax.fori_loop(0, COMBINE_TB // DMA_UNROLL, issue, 0)

    @pl.when(i == 0)
    def _():
        gather(0, 0)

    for slot in range(2):
        @pl.when((i + 1 < n) & ((i + 1) % 2 == slot))
        def _():
            gather(i + 1, slot)

    for slot in range(2):
        @pl.when(i % 2 == slot)
        def _():
            def drain(b, _):
                for _ in range(DMA_UNROLL * TOP_K):
                    row_copy(0, slot, 0, 0).wait()
                return 0
            lax.fori_loop(0, COMBINE_TB // DMA_UNROLL, drain, 0)
            acc = h_ref[...]
            for k in range(TOP_K):
                acc = acc + gate_ref[:, k:k + 1] * buf_ref[slot, k]
            o_ref[...] = acc


def _combine(dest_flat, y, h1, gate):
    t, d = h1.shape
    lanes = gate.shape[1]
    return pl.pallas_call(
        _combine_kernel,
        grid_spec=pltpu.PrefetchScalarGridSpec(
            num_scalar_prefetch=1,
            grid=(t // COMBINE_TB,),
            in_specs=[
                pl.BlockSpec(memory_space=pl.ANY),
                pl.BlockSpec((COMBINE_TB, d), lambda i, dest: (i, 0)),
                pl.BlockSpec((COMBINE_TB, lanes), lambda i, dest: (i, 0)),
            ],
            out_specs=pl.BlockSpec((COMBINE_TB, d), lambda i, dest: (i, 0)),
            scratch_shapes=[pltpu.VMEM((2, TOP_K, COMBINE_TB, d), F32), pltpu.SemaphoreType.DMA((2,))],
        ),
        out_shape=jax.ShapeDtypeStruct((t, d), F32),
        compiler_params=pltpu.CompilerParams(
            dimension_semantics=("arbitrary",), vmem_limit_bytes=VMEM_LIMIT),
        name="combine",
    )(dest_flat, y, h1, gate)


def _plan_items(counts, n_items_max):
    tiles = (counts + EXP_TM - 1) // EXP_TM
    tile_end = jnp.cumsum(tiles)
    tile_start = tile_end - tiles
    n_items = tile_end[-1]
    g = jnp.arange(n_items_max, dtype=jnp.int32)
    live = g < n_items
    gg = jnp.maximum(jnp.minimum(g, n_items - 1), 0)
    e = jnp.minimum(jnp.searchsorted(tile_end, gg, side="right"), N_EXPERTS - 1).astype(jnp.int32)
    nv = jnp.clip(counts[e] - (gg - tile_start[e]) * EXP_TM, 0, EXP_TM)
    nv = jnp.where(live, nv, 0).astype(jnp.int32)
    return tile_start * EXP_TM, e, gg.astype(jnp.int32), nv


def _layer(x2, bsz, seq, norm1_gain, w_in, q_norm_gain, k_norm_gain, sgu_norm_gain, w_spatial, b_spatial,
           w_branch_a, w_branch_b, w_out, norm2_gain, w_router, b_router, w_up, b_up, w_down, b_down):
    t, d = x2.shape
    proj = _proj(x2, norm1_gain[None, :], w_in, q_norm_gain[None, :], k_norm_gain[None, :])
    oa = _attn(proj, bsz, seq, d, q_norm_gain, k_norm_gain)
    h1, xp, idx, rank, gate, cnt = _merge(
        x2, oa, proj, w_spatial, b_spatial.T, sgu_norm_gain[None, :],
        w_branch_a.astype(BF16), w_branch_b.astype(BF16), w_out.astype(BF16),
        norm2_gain[None, :], w_router.T, b_router[:, None])

    counts = cnt[:, 0].astype(jnp.int32)
    n_items_max = -(-(t * TOP_K) // EXP_TM) + N_EXPERTS
    seg_start, item_e, item_t, item_nv = _plan_items(counts, n_items_max)
    dest = (seg_start[idx[:TOP_K]] + rank[:TOP_K]).astype(jnp.int32).reshape(-1)

    xs = _dispatch(dest, xp, n_items_max * EXP_TM)
    y = _expert(item_e, item_t, item_nv, xs, w_up, b_up, w_down, b_down)
    return _combine(dest, y, h1, gate[:TOP_K].T)


def kernel(x, norm1_gain, w_in, q_norm_gain, k_norm_gain, sgu_norm_gain, w_spatial, b_spatial, w_branch_a,
           w_branch_b, w_out, norm2_gain, w_router, b_router, w_up, b_up, w_down, b_down):
    bsz, seq, d = x.shape
    h = x.reshape(bsz * seq, d)
    for l in range(norm1_gain.shape[0]):
        h = _layer(h, bsz, seq, norm1_gain[l], w_in[l], q_norm_gain[l], k_norm_gain[l], sgu_norm_gain[l],
                   w_spatial[l], b_spatial[l], w_branch_a[l], w_branch_b[l], w_out[l], norm2_gain[l],
                   w_router[l], b_router[l], w_up[l], b_up[l], w_down[l], b_down[l])
    return h.reshape(bsz, seq, d)
```

```python
import math

import jax
import jax.numpy as jnp
from jax import lax
from jax.experimental import pallas as pl
from jax.experimental.pallas import tpu as pltpu

F32 = jnp.float32
BF16 = jnp.bfloat16

EPS = 1e-6
N_HEADS = 8
HEAD_DIM = 128
SGU_GROUPS = 8
SGU_BLOCK = 128
CHUNK = 64
N_EXPERTS = 32
TOP_K = 4
SWIGLU_LIMIT = 7.0
SWIGLU_ALPHA = 1.702

F32_EXP_ZERO_BELOW = -105.0

VMEM_LIMIT = 56 * 1024 * 1024
PROJ_TM = 1024
PROJ_TN = 512
PROJ_ROWS = 512
ATT_T = 256
ATT_HP = 4
MERGE_TM = 256
ROUTER_TM = 1024
EXP_TM = 1280
EXP_SUB = 512
EXP_GRAN = 256
EXP_TC = 256
DISPATCH_TB = 512
COMBINE_TB = 128
DMA_UNROLL = 8


def _sigmoid(x):
    return 1.0 / (1.0 + jnp.exp(-x))


def _gelu_tanh(x):
    c = math.sqrt(2.0 / math.pi)
    return 0.5 * x * (1.0 + jnp.tanh(c * (x + 0.044715 * (x * x * x))))


def _proj_kernel(x_ref, g1_ref, w_ref, qg_ref, kg_ref, o_ref, xn_ref):
    j = pl.program_id(1)
    n_q = (N_HEADS * HEAD_DIM) // PROJ_TN

    @pl.when(j == 0)
    def _():
        x = x_ref[...]
        ms = jnp.mean(x * x, axis=-1, keepdims=True)
        xn_ref[...] = (x * lax.rsqrt(ms + EPS) * g1_ref[...]).astype(BF16)

    def head_norm(a):
        gain = jnp.where(j < n_q, qg_ref[...], kg_ref[...])
        outs = []
        for h in range(PROJ_TN // HEAD_DIM):
            ah = a[:, h * HEAD_DIM:(h + 1) * HEAD_DIM]
            ms = jnp.mean(ah * ah, axis=-1, keepdims=True)
            outs.append(ah * lax.rsqrt(ms + EPS) * gain)
        return jnp.concatenate(outs, axis=1)

    def tile(epilogue):
        w = w_ref[...].astype(BF16)
        for r in range(PROJ_TM // PROJ_ROWS):
            rows = slice(r * PROJ_ROWS, (r + 1) * PROJ_ROWS)
            acc = jnp.dot(xn_ref[rows, :], w, preferred_element_type=F32)
            o_ref[rows, :] = epilogue(acc).astype(BF16)

    pl.when(j < 2 * n_q)(lambda: tile(head_norm))
    pl.when((j >= 2 * n_q) & (j < 3 * n_q))(lambda: tile(lambda a: a))
    pl.when((j >= 3 * n_q) & (j < 5 * n_q))(lambda: tile(_gelu_tanh))
    pl.when(j >= 5 * n_q)(lambda: tile(_sigmoid))


def _proj(x2, g1, w_in, qg, kg):
    t, d = x2.shape
    n = w_in.shape[1]
    nj = n // PROJ_TN
    shift = (2 * d) // PROJ_TN
    return pl.pallas_call(
        _proj_kernel,
        grid=(t // PROJ_TM, nj),
        in_specs=[
            pl.BlockSpec((PROJ_TM, d), lambda i, j: (i, 0)),
            pl.BlockSpec((1, d), lambda i, j: (0, 0)),
            pl.BlockSpec((d, PROJ_TN), lambda i, j: (0, j)),
            pl.BlockSpec((1, HEAD_DIM), lambda i, j: (0, 0)),
            pl.BlockSpec((1, HEAD_DIM), lambda i, j: (0, 0)),
        ],
        out_specs=pl.BlockSpec((PROJ_TM, PROJ_TN), lambda i, j: (i, (j + shift) % nj)),
        out_shape=jax.ShapeDtypeStruct((t, n), BF16),
        scratch_shapes=[pltpu.VMEM((PROJ_TM, d), BF16)],
        compiler_params=pltpu.CompilerParams(
            dimension_semantics=("arbitrary", "arbitrary"), vmem_limit_bytes=VMEM_LIMIT),
        name="proj",
    )(x2, g1, w_in, qg, kg)


def _attn_kernel(stop_ref, q_ref, k_ref, v_ref, u_ref, o_ref):
    qi = pl.program_id(2)
    scale = HEAD_DIM ** -0.5
    stop_at = stop_ref[0]

    def block(j, accs, runs, diag):
        start = pl.multiple_of(j * ATT_T, ATT_T)
        tri = u_ref[...]
        if diag:
            row = lax.broadcasted_iota(jnp.int32, (ATT_T, ATT_T), 0)
            col = lax.broadcasted_iota(jnp.int32, (ATT_T, ATT_T), 1)
            past = col < row
        heads = range(ATT_HP)
        cols = [slice(h * HEAD_DIM, (h + 1) * HEAD_DIM) for h in heads]
        zs = [lax.dot_general(q_ref[:, cols[h]], k_ref[pl.ds(start, ATT_T), cols[h]],
                              (((1,), (1,)), ((), ())), preferred_element_type=F32) * scale for h in heads]
        sps = [jnp.maximum(z, 0.0) + jnp.log1p(jnp.exp(-jnp.abs(z))) for z in zs]
        if diag:
            sps = [jnp.where(past, sp, 0.0) for sp in sps]
        his = [sp.astype(BF16) for sp in sps]
        los = [(sp - hi.astype(F32)).astype(BF16) for sp, hi in zip(sps, his)]
        css = [jnp.dot(hi, tri, preferred_element_type=F32) + jnp.dot(lo, tri, preferred_element_type=F32)
               for hi, lo in zip(his, los)]
        ws = [jnp.exp(zs[h] - (runs[h] + css[h])) for h in heads]
        if diag:
            ws = [jnp.where(past, w, 0.0) for w in ws]
        new_accs = [accs[h] + jnp.dot(ws[h].astype(BF16), v_ref[pl.ds(start, ATT_T), cols[h]],
                                      preferred_element_type=F32) for h in heads]
        new_runs = [runs[h] + css[h][:, 0:1] for h in heads]
        return tuple(new_accs), tuple(new_runs)

    def smallest(runs):
        m = runs[0]
        for r in runs[1:]:
            m = jnp.minimum(m, r)
        return jnp.min(m)

    accs = tuple(jnp.zeros((ATT_T, HEAD_DIM), F32) for _ in range(ATT_HP))
    runs = tuple(jnp.zeros((ATT_T, 1), F32) for _ in range(ATT_HP))
    accs, runs = block(qi, accs, runs, True)

    def cond(c):
        j, _, _, low = c
        return (j >= 0) & (low < stop_at)

    def body(c):
        j, accs, runs, _ = c
        accs, runs = block(j, accs, runs, False)
        return j - 1, accs, runs, smallest(runs)

    _, accs, _, _ = lax.while_loop(cond, body, (qi - 1, accs, runs, smallest(runs)))
    for h in range(ATT_HP):
        o_ref[:, h * HEAD_DIM:(h + 1) * HEAD_DIM] = accs[h].astype(BF16)


def _attn(proj, bsz, seq, d, q_gain, k_gain):
    t = bsz * seq
    nq = seq // ATT_T
    width = ATT_HP * HEAD_DIM
    col0 = (2 * d) // width
    seg = (N_HEADS * HEAD_DIM) // width
    i = jnp.arange(ATT_T)
    tri = (i[:, None] >= i[None, :]).astype(BF16)
    zmax = 1.02 * math.sqrt(HEAD_DIM) * jnp.max(jnp.abs(q_gain)) * jnp.max(jnp.abs(k_gain))
    stop_at = (zmax - F32_EXP_ZERO_BELOW).reshape(1).astype(F32)
    return pl.pallas_call(
        _attn_kernel,
        grid_spec=pltpu.PrefetchScalarGridSpec(
            num_scalar_prefetch=1,
            grid=(bsz, seg, nq),
            in_specs=[
                pl.BlockSpec((ATT_T, width), lambda b, h, qi, s: (b * nq + qi, col0 + h)),
                pl.BlockSpec((seq, width), lambda b, h, qi, s: (b, col0 + seg + h)),
                pl.BlockSpec((seq, width), lambda b, h, qi, s: (b, col0 + 2 * seg + h)),
                pl.BlockSpec((ATT_T, ATT_T), lambda b, h, qi, s: (0, 0)),
            ],
            out_specs=pl.BlockSpec((ATT_T, width), lambda b, h, qi, s: (b * nq + qi, h)),
        ),
        out_shape=jax.ShapeDtypeStruct((t, N_HEADS * HEAD_DIM), BF16),
        compiler_params=pltpu.CompilerParams(
            dimension_semantics=("arbitrary", "arbitrary", "arbitrary"), vmem_limit_bytes=VMEM_LIMIT),
        name="attn",
    )(stop_at, proj, proj, proj, tri)


def _merge_kernel(x_ref, oa_ref, u_ref, vv_ref, ga_ref, gb_ref, ws_ref, bst_ref, sg_ref,
                  wa_ref, wb_ref, wo_ref, h_ref, vln_ref, ob_ref):
    vv = vv_ref[...].astype(F32)
    mu = jnp.mean(vv, axis=-1, keepdims=True)
    xc = vv - mu
    var = jnp.mean(xc * xc, axis=-1, keepdims=True)
    vln_ref[...] = (xc * lax.rsqrt(var + EPS) * sg_ref[...]).astype(BF16)

    row = lax.broadcasted_iota(jnp.int32, (SGU_BLOCK, SGU_BLOCK), 0)
    col = lax.broadcasted_iota(jnp.int32, (SGU_BLOCK, SGU_BLOCK), 1)
    mask = (col // CHUNK) <= (row // CHUNK)
    for g in range(SGU_GROUPS):
        wg = jnp.where(mask, ws_ref[g], 0.0).astype(BF16)
        bg = bst_ref[:, g:g + 1]
        cs = slice(g * SGU_BLOCK, (g + 1) * SGU_BLOCK)
        for n in range(MERGE_TM // SGU_BLOCK):
            rs = slice(n * SGU_BLOCK, (n + 1) * SGU_BLOCK)
            sgu = jnp.dot(wg, vln_ref[rs, cs], preferred_element_type=F32) + bg
            ob_ref[rs, cs] = (u_ref[rs, cs].astype(F32) * sgu).astype(BF16)

    ya = jnp.dot(oa_ref[...], wa_ref[...], preferred_element_type=F32)
    yb = jnp.dot(ob_ref[...], wb_ref[...], preferred_element_type=F32)
    y = ga_ref[...].astype(F32) * ya + gb_ref[...].astype(F32) * yb
    h_ref[...] = x_ref[...] + jnp.dot(y.astype(BF16), wo_ref[...], preferred_element_type=F32)


def _merge(x2, oa, proj, w_spatial, b_spatial_t, sgu_gain, wa, wb, wo):
    t, d = x2.shape
    aw = oa.shape[1]
    sw = SGU_GROUPS * SGU_BLOCK
    ucol = (2 * d + 3 * aw) // sw
    const = lambda shape: pl.BlockSpec(shape, lambda i: (0,) * len(shape), pipeline_mode=pl.Buffered(1))
    return pl.pallas_call(
        _merge_kernel,
        grid=(t // MERGE_TM,),
        in_specs=[
            pl.BlockSpec((MERGE_TM, d), lambda i: (i, 0)),
            pl.BlockSpec((MERGE_TM, aw), lambda i: (i, 0)),
            pl.BlockSpec((MERGE_TM, sw), lambda i: (i, ucol)),
            pl.BlockSpec((MERGE_TM, sw), lambda i: (i, ucol + 1)),
            pl.BlockSpec((MERGE_TM, d), lambda i: (i, 0)),
            pl.BlockSpec((MERGE_TM, d), lambda i: (i, 1)),
            const((SGU_GROUPS, SGU_BLOCK, SGU_BLOCK)),
            const((SGU_BLOCK, SGU_GROUPS)),
            const((1, sw)),
            const((aw, d)),
            const((sw, d)),
            const((d, d)),
        ],
        out_specs=pl.BlockSpec((MERGE_TM, d), lambda i: (i, 0)),
        out_shape=jax.ShapeDtypeStruct((t, d), F32),
        scratch_shapes=[pltpu.VMEM((MERGE_TM, sw), BF16), pltpu.VMEM((MERGE_TM, sw), BF16)],
        compiler_params=pltpu.CompilerParams(
            dimension_semantics=("arbitrary",), vmem_limit_bytes=VMEM_LIMIT),
        name="merge",
    )(x2, oa, proj, proj, proj, proj, w_spatial, b_spatial_t, sgu_gain, wa, wb, wo)


def _router_kernel(h_ref, g2_ref, wrt_ref, brt_ref, ut_ref, xp_ref, idx_ref, rank_ref, gate_ref, cnt_ref,
                   carry_ref):
    @pl.when(pl.program_id(0) == 0)
    def _():
        carry_ref[...] = jnp.zeros_like(carry_ref)

    h = h_ref[...]
    half = h.shape[1] // 2
    ms = jnp.mean(h * h, axis=-1, keepdims=True)
    xn = h * lax.rsqrt(ms + EPS) * g2_ref[...]
    bits = lax.bitcast_convert_type(xn.astype(BF16).astype(F32), jnp.uint32)
    xp_ref[...] = bits[:, half:] | (bits[:, :half] >> 16)

    logits = lax.dot_general(wrt_ref[...], xn, (((1,), (1,)), ((), ())), preferred_element_type=F32,
                             precision=lax.Precision.HIGHEST) + brt_ref[...]
    sub = lax.broadcasted_iota(jnp.int32, logits.shape, 0).astype(F32)
    out_row = lax.broadcasted_iota(jnp.int32, idx_ref.shape, 0)
    work = logits
    vals, sels = [], []
    idx_out = jnp.zeros(idx_ref.shape, jnp.int32)
    for k in range(TOP_K):
        m = jnp.max(work, axis=0, keepdims=True)
        idx = jnp.min(jnp.where(work == m, sub, float(N_EXPERTS)), axis=0, keepdims=True)
        sel = sub == idx
        work = jnp.where(sel, -jnp.inf, work)
        vals.append(m)
        sels.append(sel)
        idx_out = jnp.where(out_row == k, idx.astype(jnp.int32), idx_out)
    idx_ref[...] = idx_out

    exps = [jnp.exp(v - vals[0]) for v in vals]
    denom = exps[0] + exps[1] + exps[2] + exps[3]
    gate_out = jnp.zeros(gate_ref.shape, F32)
    for k in range(TOP_K):
        gate_out = jnp.where(out_row == k, exps[k] / denom, gate_out)
    gate_ref[...] = gate_out

    member = (sels[0] | sels[1] | sels[2] | sels[3])
    before = jnp.dot(member.astype(BF16), ut_ref[...], preferred_element_type=F32)
    pos = carry_ref[...] + before
    rank_out = jnp.zeros(rank_ref.shape, jnp.int32)
    for k in range(TOP_K):
        r = jnp.sum(jnp.where(sels[k], pos, 0.0), axis=0, keepdims=True)
        rank_out = jnp.where(out_row == k, r.astype(jnp.int32), rank_out)
    rank_ref[...] = rank_out

    carry_ref[...] = carry_ref[...] + jnp.sum(member.astype(F32), axis=1, keepdims=True)
    cnt_ref[...] = jnp.broadcast_to(carry_ref[...], cnt_ref.shape)


def _router(h1, g2, w_router_t, b_router_t):
    t, d = h1.shape
    const = lambda shape: pl.BlockSpec(shape, lambda i: (0,) * len(shape))
    i = jnp.arange(ROUTER_TM)
    ut = (i[:, None] < i[None, :]).astype(BF16)
    rows, lanes = 8, 128
    tok = lambda: pl.BlockSpec((rows, ROUTER_TM), lambda i: (0, i))
    return pl.pallas_call(
        _router_kernel,
        grid=(t // ROUTER_TM,),
        in_specs=[
            pl.BlockSpec((ROUTER_TM, d), lambda i: (i, 0)),
            const((1, d)),
            const((N_EXPERTS, d)),
            const((N_EXPERTS, 1)),
            const((ROUTER_TM, ROUTER_TM)),
        ],
        out_specs=[
            pl.BlockSpec((ROUTER_TM, d // 2), lambda i: (i, 0)),
            tok(), tok(), tok(),
            pl.BlockSpec((N_EXPERTS, lanes), lambda i: (0, 0)),
        ],
        out_shape=[
            jax.ShapeDtypeStruct((t, d // 2), jnp.uint32),
            jax.ShapeDtypeStruct((rows, t), jnp.int32),
            jax.ShapeDtypeStruct((rows, t), jnp.int32),
            jax.ShapeDtypeStruct((rows, t), F32),
            jax.ShapeDtypeStruct((N_EXPERTS, lanes), F32),
        ],
        scratch_shapes=[pltpu.VMEM((N_EXPERTS, 1), F32)],
        compiler_params=pltpu.CompilerParams(
            dimension_semantics=("arbitrary",), vmem_limit_bytes=VMEM_LIMIT),
        name="router",
    )(h1, g2, w_router_t, b_router_t, ut)


def _dispatch_kernel(dest_ref, xp_ref, xs_in_ref, xs_ref, sem):
    del xs_in_ref
    i = pl.program_id(0)

    n_tok = dest_ref.shape[0] // TOP_K

    def row_copy(s, r):
        return pltpu.make_async_copy(xp_ref.at[pl.ds(s, 1)], xs_ref.at[pl.ds(r, 1)], sem)

    def issue(b, _):
        for j in range(DMA_UNROLL):
            s = b * DMA_UNROLL + j
            for k in range(TOP_K):
                row_copy(s, dest_ref[k * n_tok + i * DISPATCH_TB + s]).start(priority=k % 2)
        return 0

    def drain(b, _):
        for _ in range(DMA_UNROLL * TOP_K):
            row_copy(0, 0).wait()
        return 0

    lax.fori_loop(0, DISPATCH_TB // DMA_UNROLL, issue, 0)
    lax.fori_loop(0, DISPATCH_TB // DMA_UNROLL, drain, 0)


def _dispatch(dest_flat, xp, n_rows):
    t, w = xp.shape
    xs0 = jnp.zeros((n_rows, w), jnp.uint32)
    return pl.pallas_call(
        _dispatch_kernel,
        grid_spec=pltpu.PrefetchScalarGridSpec(
            num_scalar_prefetch=1,
            grid=(t // DISPATCH_TB,),
            in_specs=[pl.BlockSpec((DISPATCH_TB, w), lambda i, dest: (i, 0)),
                      pl.BlockSpec(memory_space=pl.ANY)],
            out_specs=pl.BlockSpec(memory_space=pl.ANY),
            scratch_shapes=[pltpu.SemaphoreType.DMA(())],
        ),
        out_shape=jax.ShapeDtypeStruct((n_rows, w), jnp.uint32),
        input_output_aliases={2: 0},
        compiler_params=pltpu.CompilerParams(
            dimension_semantics=("arbitrary",), has_side_effects=True),
        name="dispatch",
    )(dest_flat, xp, xs0)


def _expert_kernel(ie_ref, it_ref, nv_ref, xs_ref, wg_ref, wu_ref, wd_ref, bg_ref, bu_ref, bd_ref,
                   y_ref, xb_ref):
    del ie_ref, it_ref
    i = pl.program_id(0)
    c = pl.program_id(1)
    nv = nv_ref[i]
    half = xs_ref.shape[1]

    @pl.when(c == 0)
    def _():
        y_ref[...] = jnp.broadcast_to(bd_ref[0], y_ref.shape)

    @pl.when((c == 0) & (nv > 0))
    def _():
        w = xs_ref[...]
        xb_ref[:, :half] = lax.bitcast_convert_type(w << 16, F32).astype(BF16)
        xb_ref[:, half:] = lax.bitcast_convert_type(w & jnp.uint32(0xFFFF0000), F32).astype(BF16)

    def run(n_rows):
        wg = wg_ref[0].astype(BF16)
        wu = wu_ref[0].astype(BF16)
        wd = wd_ref[0].astype(BF16)
        for lo in range(0, n_rows, EXP_SUB):
            rows = slice(lo, min(lo + EXP_SUB, n_rows))
            xb = xb_ref[rows, :]
            g = jnp.dot(xb, wg, preferred_element_type=F32) + bg_ref[0]
            u = jnp.dot(xb, wu, preferred_element_type=F32) + bu_ref[0]
            g = jnp.minimum(g, SWIGLU_LIMIT)
            u = jnp.clip(u, -SWIGLU_LIMIT, SWIGLU_LIMIT)
            a = g * _sigmoid(SWIGLU_ALPHA * g) * (u + 1.0)
            y_ref[rows, :] += jnp.dot(a.astype(BF16), wd, preferred_element_type=F32)

    for n_rows in range(EXP_GRAN, EXP_TM + 1, EXP_GRAN):
        pl.when((nv > n_rows - EXP_GRAN) & (nv <= n_rows))(lambda n_rows=n_rows: run(n_rows))


def _expert(item_e, item_t, item_nv, xs, w_up, b_up, w_down, b_down):
    n_rows, half = xs.shape
    ne, d, two_f = w_up.shape
    f = two_f // 2
    nc = f // EXP_TC
    n_items = item_e.shape[0]
    b_up3 = b_up.reshape(ne, 1, two_f)
    b_down3 = b_down.reshape(ne, 1, d)

    def chunk(c, nv_ref, i):
        return jnp.where(nv_ref[i] > 0, c, nc - 1)

    return pl.pallas_call(
        _expert_kernel,
        grid_spec=pltpu.PrefetchScalarGridSpec(
            num_scalar_prefetch=3,
            grid=(n_items, nc),
            in_specs=[
                pl.BlockSpec((EXP_TM, half), lambda i, c, ie, it, nv: (it[i], 0)),
                pl.BlockSpec((1, d, EXP_TC), lambda i, c, ie, it, nv: (ie[i], 0, chunk(c, nv, i))),
                pl.BlockSpec((1, d, EXP_TC), lambda i, c, ie, it, nv: (ie[i], 0, nc + chunk(c, nv, i))),
                pl.BlockSpec((1, EXP_TC, d), lambda i, c, ie, it, nv: (ie[i], chunk(c, nv, i), 0)),
                pl.BlockSpec((1, 1, EXP_TC), lambda i, c, ie, it, nv: (ie[i], 0, chunk(c, nv, i))),
                pl.BlockSpec((1, 1, EXP_TC), lambda i, c, ie, it, nv: (ie[i], 0, nc + chunk(c, nv, i))),
                pl.BlockSpec((1, 1, d), lambda i, c, ie, it, nv: (ie[i], 0, 0)),
            ],
            out_specs=pl.BlockSpec((EXP_TM, d), lambda i, c, ie, it, nv: (i, 0)),
            scratch_shapes=[pltpu.VMEM((EXP_TM, d), BF16)],
        ),
        out_shape=jax.ShapeDtypeStruct((n_rows, d), F32),
        compiler_params=pltpu.CompilerParams(
            dimension_semantics=("arbitrary", "arbitrary"), vmem_limit_bytes=VMEM_LIMIT),
        name="expert",
    )(item_e, item_t, item_nv, xs, w_up, w_up, w_down, b_up3, b_up3, b_down3)


def _combine_kernel(dest_ref, y_ref, h_ref, gate_ref, o_ref, buf_ref, sem):
    i = pl.program_id(0)
    n = pl.num_programs(0)
    n_tok = dest_ref.shape[0] // TOP_K

    def row_copy(r, slot, k, s):
        return pltpu.make_async_copy(y_ref.at[pl.ds(r, 1)], buf_ref.at[slot, k, pl.ds(s, 1)], sem.at[slot])

    def gather(step, slot):
        def issue(b, _):
            for j in range(DMA_UNROLL):
                s = b * DMA_UNROLL + j
                for k in range(TOP_K):
                    row_copy(dest_ref[k * n_tok + step * COMBINE_TB + s], slot, k, s).start(priority=k % 2)
            return 0
        lax.fori_loop(0, COMBINE_TB // DMA_UNROLL, issue, 0)

    @pl.when(i == 0)
    def _():
        gather(0, 0)

    for slot in range(2):
        @pl.when((i + 1 < n) & ((i + 1) % 2 == slot))
        def _():
            gather(i + 1, slot)

    for slot in range(2):
        @pl.when(i % 2 == slot)
        def _():
            def drain(b, _):
                for _ in range(DMA_UNROLL * TOP_K):
                    row_copy(0, slot, 0, 0).wait()
                return 0
            lax.fori_loop(0, COMBINE_TB // DMA_UNROLL, drain, 0)
            acc = h_ref[...]
            for k in range(TOP_K):
                acc = acc + gate_ref[:, k:k + 1] * buf_ref[slot, k]
            o_ref[...] = acc


def _combine(dest_flat, y, h1, gate):
    t, d = h1.shape
    lanes = gate.shape[1]
    return pl.pallas_call(
        _combine_kernel,
        grid_spec=pltpu.PrefetchScalarGridSpec(
            num_scalar_prefetch=1,
            grid=(t // COMBINE_TB,),
            in_specs=[
                pl.BlockSpec(memory_space=pl.ANY),
                pl.BlockSpec((COMBINE_TB, d), lambda i, dest: (i, 0)),
                pl.BlockSpec((COMBINE_TB, lanes), lambda i, dest: (i, 0)),
            ],
            out_specs=pl.BlockSpec((COMBINE_TB, d), lambda i, dest: (i, 0)),
            scratch_shapes=[pltpu.VMEM((2, TOP_K, COMBINE_TB, d), F32), pltpu.SemaphoreType.DMA((2,))],
        ),
        out_shape=jax.ShapeDtypeStruct((t, d), F32),
        compiler_params=pltpu.CompilerParams(
            dimension_semantics=("arbitrary",), vmem_limit_bytes=VMEM_LIMIT),
        name="combine",
    )(dest_flat, y, h1, gate)


def _plan_items(counts, n_items_max):
    tiles = (counts + EXP_TM - 1) // EXP_TM
    tile_end = jnp.cumsum(tiles)
    tile_start = tile_end - tiles
    n_items = tile_end[-1]
    g = jnp.arange(n_items_max, dtype=jnp.int32)
    live = g < n_items
    gg = jnp.maximum(jnp.minimum(g, n_items - 1), 0)
    e = jnp.minimum(jnp.searchsorted(tile_end, gg, side="right"), N_EXPERTS - 1).astype(jnp.int32)
    nv = jnp.clip(counts[e] - (gg - tile_start[e]) * EXP_TM, 0, EXP_TM)
    nv = jnp.where(live, nv, 0).astype(jnp.int32)
    return tile_start * EXP_TM, e, gg.astype(jnp.int32), nv


def _layer(x2, bsz, seq, norm1_gain, w_in, q_norm_gain, k_norm_gain, sgu_norm_gain, w_spatial, b_spatial,
           w_branch_a, w_branch_b, w_out, norm2_gain, w_router, b_router, w_up, b_up, w_down, b_down):
    t, d = x2.shape
    proj = _proj(x2, norm1_gain[None, :], w_in, q_norm_gain[None, :], k_norm_gain[None, :])
    oa = _attn(proj, bsz, seq, d, q_norm_gain, k_norm_gain)
    h1 = _merge(x2, oa, proj, w_spatial, b_spatial.T, sgu_norm_gain[None, :],
                w_branch_a.astype(BF16), w_branch_b.astype(BF16), w_out.astype(BF16))
    xp, idx, rank, gate, cnt = _router(h1, norm2_gain[None, :], w_router.T, b_router[:, None])

    counts = cnt[:, 0].astype(jnp.int32)
    n_items_max = -(-(t * TOP_K) // EXP_TM) + N_EXPERTS
    seg_start, item_e, item_t, item_nv = _plan_items(counts, n_items_max)
    onehot = idx[:TOP_K, :, None] == jnp.arange(N_EXPERTS, dtype=jnp.int32)
    dest = (jnp.sum(jnp.where(onehot, seg_start.astype(jnp.int32), 0), axis=-1)
            + rank[:TOP_K]).reshape(-1)

    xs = _dispatch(dest, xp, n_items_max * EXP_TM)
    y = _expert(item_e, item_t, item_nv, xs, w_up, b_up, w_down, b_down)
    return _combine(dest, y, h1, gate[:TOP_K].T)


def kernel(x, norm1_gain, w_in, q_norm_gain, k_norm_gain, sgu_norm_gain, w_spatial, b_spatial, w_branch_a,
           w_branch_b, w_out, norm2_gain, w_router, b_router, w_up, b_up, w_down, b_down):
    bsz, seq, d = x.shape
    h = x.reshape(bsz * seq, d)
    for l in range(norm1_gain.shape[0]):
        h = _layer(h, bsz, seq, norm1_gain[l], w_in[l], q_norm_gain[l], k_norm_gain[l], sgu_norm_gain[l],
                   w_spatial[l], b_spatial[l], w_branch_a[l], w_branch_b[l], w_out[l], norm2_gain[l],
                   w_router[l], b_router[l], w_up[l], b_up[l], w_down[l], b_down[l])
    return h.reshape(bsz, seq, d)
```

```python
import math

import jax
import jax.numpy as jnp
from jax import lax
from jax.experimental import pallas as pl
from jax.experimental.pallas import tpu as pltpu

F32 = jnp.float32
BF16 = jnp.bfloat16

EPS = 1e-6
N_HEADS = 8
HEAD_DIM = 128
SGU_GROUPS = 8
SGU_BLOCK = 128
CHUNK = 64
N_EXPERTS = 32
TOP_K = 4
SWIGLU_LIMIT = 7.0
SWIGLU_ALPHA = 1.702

F32_EXP_ZERO_BELOW = -105.0

VMEM_LIMIT = 56 * 1024 * 1024
PROJ_TM = 1024
PROJ_TN = 512
PROJ_ROWS = 512
ATT_T = 256
ATT_HP = 4
MERGE_TM = 256
ROUTER_TM = 1024
EXP_TM = 1280
EXP_SUB = 512
EXP_GRAN = 256
EXP_TC = 256
DISPATCH_TB = 512
COMBINE_TB = 128
DMA_UNROLL = 8


def _sigmoid(x):
    return 1.0 / (1.0 + jnp.exp(-x))


ROW_SUBLANES = 8
LANES = 128


def _store_rows_as_tiles(ref, rows2d):
    n = rows2d.shape[0]
    for s in range(ROW_SUBLANES):
        ref[pl.ds(s, n, stride=ROW_SUBLANES), :] = rows2d[:, s * LANES:(s + 1) * LANES]


def _tile_of_row(ref, r):
    start = r * ROW_SUBLANES
    if not isinstance(r, int):
        start = pl.multiple_of(start, ROW_SUBLANES)
    return ref.at[pl.ds(start, ROW_SUBLANES)]


def _load_tile_rows(ref, s, n):
    return ref[pl.ds(s, n, stride=ROW_SUBLANES), :]


def _pack_bf16_pair(lo, hi):
    lo_bits = lax.bitcast_convert_type(lo.astype(BF16).astype(F32), jnp.uint32)
    hi_bits = lax.bitcast_convert_type(hi.astype(BF16).astype(F32), jnp.uint32)
    return hi_bits | (lo_bits >> 16)


def _unpack_bf16_pair(w):
    lo = lax.bitcast_convert_type(w << 16, F32)
    hi = lax.bitcast_convert_type(w & jnp.uint32(0xFFFF0000), F32)
    return lo, hi


def _gelu_tanh(x):
    c = math.sqrt(2.0 / math.pi)
    return 0.5 * x * (1.0 + jnp.tanh(c * (x + 0.044715 * (x * x * x))))


def _proj_kernel(x_ref, g1_ref, w_ref, qg_ref, kg_ref, o_ref, xn_ref):
    j = pl.program_id(1)
    n_q = (N_HEADS * HEAD_DIM) // PROJ_TN

    @pl.when(j == 0)
    def _():
        x = x_ref[...]
        ms = jnp.mean(x * x, axis=-1, keepdims=True)
        xn_ref[...] = (x * lax.rsqrt(ms + EPS) * g1_ref[...]).astype(BF16)

    def head_norm(a):
        gain = jnp.where(j < n_q, qg_ref[...], kg_ref[...])
        outs = []
        for h in range(PROJ_TN // HEAD_DIM):
            ah = a[:, h * HEAD_DIM:(h + 1) * HEAD_DIM]
            ms = jnp.mean(ah * ah, axis=-1, keepdims=True)
            outs.append(ah * lax.rsqrt(ms + EPS) * gain)
        return jnp.concatenate(outs, axis=1)

    def tile(epilogue):
        w = w_ref[...].astype(BF16)
        for r in range(PROJ_TM // PROJ_ROWS):
            rows = slice(r * PROJ_ROWS, (r + 1) * PROJ_ROWS)
            acc = jnp.dot(xn_ref[rows, :], w, preferred_element_type=F32)
            o_ref[rows, :] = epilogue(acc).astype(BF16)

    pl.when(j < 2 * n_q)(lambda: tile(head_norm))
    pl.when((j >= 2 * n_q) & (j < 3 * n_q))(lambda: tile(lambda a: a))
    pl.when((j >= 3 * n_q) & (j < 5 * n_q))(lambda: tile(_gelu_tanh))
    pl.when(j >= 5 * n_q)(lambda: tile(_sigmoid))


def _proj(x2, g1, w_in, qg, kg):
    t, d = x2.shape
    n = w_in.shape[1]
    nj = n // PROJ_TN
    shift = (2 * d) // PROJ_TN
    return pl.pallas_call(
        _proj_kernel,
        grid=(t // PROJ_TM, nj),
        in_specs=[
            pl.BlockSpec((PROJ_TM, d), lambda i, j: (i, 0)),
            pl.BlockSpec((1, d), lambda i, j: (0, 0)),
            pl.BlockSpec((d, PROJ_TN), lambda i, j: (0, j)),
            pl.BlockSpec((1, HEAD_DIM), lambda i, j: (0, 0)),
            pl.BlockSpec((1, HEAD_DIM), lambda i, j: (0, 0)),
        ],
        out_specs=pl.BlockSpec((PROJ_TM, PROJ_TN), lambda i, j: (i, (j + shift) % nj)),
        out_shape=jax.ShapeDtypeStruct((t, n), BF16),
        scratch_shapes=[pltpu.VMEM((PROJ_TM, d), BF16)],
        compiler_params=pltpu.CompilerParams(
            dimension_semantics=("arbitrary", "arbitrary"), vmem_limit_bytes=VMEM_LIMIT),
        name="proj",
    )(x2, g1, w_in, qg, kg)


def _attn_kernel(stop_ref, q_ref, k_ref, v_ref, u_ref, o_ref):
    qi = pl.program_id(2)
    scale = HEAD_DIM ** -0.5
    stop_at = stop_ref[0]

    def block(j, accs, runs, diag):
        start = pl.multiple_of(j * ATT_T, ATT_T)
        tri = u_ref[...]
        if diag:
            row = lax.broadcasted_iota(jnp.int32, (ATT_T, ATT_T), 0)
            col = lax.broadcasted_iota(jnp.int32, (ATT_T, ATT_T), 1)
            past = col < row
        heads = range(ATT_HP)
        cols = [slice(h * HEAD_DIM, (h + 1) * HEAD_DIM) for h in heads]
        zs = [lax.dot_general(q_ref[:, cols[h]], k_ref[pl.ds(start, ATT_T), cols[h]],
                              (((1,), (1,)), ((), ())), preferred_element_type=F32) * scale for h in heads]
        sps = [jnp.maximum(z, 0.0) + jnp.log1p(jnp.exp(-jnp.abs(z))) for z in zs]
        if diag:
            sps = [jnp.where(past, sp, 0.0) for sp in sps]
        his = [sp.astype(BF16) for sp in sps]
        los = [(sp - hi.astype(F32)).astype(BF16) for sp, hi in zip(sps, his)]
        css = [jnp.dot(hi, tri, preferred_element_type=F32) + jnp.dot(lo, tri, preferred_element_type=F32)
               for hi, lo in zip(his, los)]
        ws = [jnp.exp(zs[h] - (runs[h] + css[h])) for h in heads]
        if diag:
            ws = [jnp.where(past, w, 0.0) for w in ws]
        new_accs = [accs[h] + jnp.dot(ws[h].astype(BF16), v_ref[pl.ds(start, ATT_T), cols[h]],
                                      preferred_element_type=F32) for h in heads]
        new_runs = [runs[h] + css[h][:, 0:1] for h in heads]
        return tuple(new_accs), tuple(new_runs)

    def smallest(runs):
        m = runs[0]
        for r in runs[1:]:
            m = jnp.minimum(m, r)
        return jnp.min(m)

    accs = tuple(jnp.zeros((ATT_T, HEAD_DIM), F32) for _ in range(ATT_HP))
    runs = tuple(jnp.zeros((ATT_T, 1), F32) for _ in range(ATT_HP))
    accs, runs = block(qi, accs, runs, True)

    def cond(c):
        j, _, _, low = c
        return (j >= 0) & (low < stop_at)

    def body(c):
        j, accs, runs, _ = c
        accs, runs = block(j, accs, runs, False)
        return j - 1, accs, runs, smallest(runs)

    _, accs, _, _ = lax.while_loop(cond, body, (qi - 1, accs, runs, smallest(runs)))
    for h in range(ATT_HP):
        o_ref[:, h * HEAD_DIM:(h + 1) * HEAD_DIM] = accs[h].astype(BF16)


def _attn(proj, bsz, seq, d, q_gain, k_gain):
    t = bsz * seq
    nq = seq // ATT_T
    width = ATT_HP * HEAD_DIM
    col0 = (2 * d) // width
    seg = (N_HEADS * HEAD_DIM) // width
    i = jnp.arange(ATT_T)
    tri = (i[:, None] >= i[None, :]).astype(BF16)
    zmax = 1.02 * math.sqrt(HEAD_DIM) * jnp.max(jnp.abs(q_gain)) * jnp.max(jnp.abs(k_gain))
    stop_at = (zmax - F32_EXP_ZERO_BELOW).reshape(1).astype(F32)
    return pl.pallas_call(
        _attn_kernel,
        grid_spec=pltpu.PrefetchScalarGridSpec(
            num_scalar_prefetch=1,
            grid=(bsz, seg, nq),
            in_specs=[
                pl.BlockSpec((ATT_T, width), lambda b, h, qi, s: (b * nq + qi, col0 + h)),
                pl.BlockSpec((seq, width), lambda b, h, qi, s: (b, col0 + seg + h)),
                pl.BlockSpec((seq, width), lambda b, h, qi, s: (b, col0 + 2 * seg + h)),
                pl.BlockSpec((ATT_T, ATT_T), lambda b, h, qi, s: (0, 0)),
            ],
            out_specs=pl.BlockSpec((ATT_T, width), lambda b, h, qi, s: (b * nq + qi, h)),
        ),
        out_shape=jax.ShapeDtypeStruct((t, N_HEADS * HEAD_DIM), BF16),
        compiler_params=pltpu.CompilerParams(
            dimension_semantics=("arbitrary", "arbitrary", "arbitrary"), vmem_limit_bytes=VMEM_LIMIT),
        name="attn",
    )(stop_at, proj, proj, proj, tri)


def _merge_kernel(x_ref, oa_ref, u_ref, vv_ref, ga_ref, gb_ref, ws_ref, bst_ref, sg_ref,
                  wa_ref, wb_ref, wo_ref, h_ref, vln_ref, ob_ref):
    vv = vv_ref[...].astype(F32)
    mu = jnp.mean(vv, axis=-1, keepdims=True)
    xc = vv - mu
    var = jnp.mean(xc * xc, axis=-1, keepdims=True)
    vln_ref[...] = (xc * lax.rsqrt(var + EPS) * sg_ref[...]).astype(BF16)

    row = lax.broadcasted_iota(jnp.int32, (SGU_BLOCK, SGU_BLOCK), 0)
    col = lax.broadcasted_iota(jnp.int32, (SGU_BLOCK, SGU_BLOCK), 1)
    mask = (col // CHUNK) <= (row // CHUNK)
    for g in range(SGU_GROUPS):
        wg = jnp.where(mask, ws_ref[g], 0.0).astype(BF16)
        bg = bst_ref[:, g:g + 1]
        cs = slice(g * SGU_BLOCK, (g + 1) * SGU_BLOCK)
        for n in range(MERGE_TM // SGU_BLOCK):
            rs = slice(n * SGU_BLOCK, (n + 1) * SGU_BLOCK)
            sgu = jnp.dot(wg, vln_ref[rs, cs], preferred_element_type=F32) + bg
            ob_ref[rs, cs] = (u_ref[rs, cs].astype(F32) * sgu).astype(BF16)

    ya = jnp.dot(oa_ref[...], wa_ref[...], preferred_element_type=F32)
    yb = jnp.dot(ob_ref[...], wb_ref[...], preferred_element_type=F32)
    y = ga_ref[...].astype(F32) * ya + gb_ref[...].astype(F32) * yb
    h_ref[...] = x_ref[...] + jnp.dot(y.astype(BF16), wo_ref[...], preferred_element_type=F32)


def _merge(x2, oa, proj, w_spatial, b_spatial_t, sgu_gain, wa, wb, wo):
    t, d = x2.shape
    aw = oa.shape[1]
    sw = SGU_GROUPS * SGU_BLOCK
    ucol = (2 * d + 3 * aw) // sw
    const = lambda shape: pl.BlockSpec(shape, lambda i: (0,) * len(shape), pipeline_mode=pl.Buffered(1))
    return pl.pallas_call(
        _merge_kernel,
        grid=(t // MERGE_TM,),
        in_specs=[
            pl.BlockSpec((MERGE_TM, d), lambda i: (i, 0)),
            pl.BlockSpec((MERGE_TM, aw), lambda i: (i, 0)),
            pl.BlockSpec((MERGE_TM, sw), lambda i: (i, ucol)),
            pl.BlockSpec((MERGE_TM, sw), lambda i: (i, ucol + 1)),
            pl.BlockSpec((MERGE_TM, d), lambda i: (i, 0)),
            pl.BlockSpec((MERGE_TM, d), lambda i: (i, 1)),
            const((SGU_GROUPS, SGU_BLOCK, SGU_BLOCK)),
            const((SGU_BLOCK, SGU_GROUPS)),
            const((1, sw)),
            const((aw, d)),
            const((sw, d)),
            const((d, d)),
        ],
        out_specs=pl.BlockSpec((MERGE_TM, d), lambda i: (i, 0)),
        out_shape=jax.ShapeDtypeStruct((t, d), F32),
        scratch_shapes=[pltpu.VMEM((MERGE_TM, sw), BF16), pltpu.VMEM((MERGE_TM, sw), BF16)],
        compiler_params=pltpu.CompilerParams(
            dimension_semantics=("arbitrary",), vmem_limit_bytes=VMEM_LIMIT),
        name="merge",
    )(x2, oa, proj, proj, proj, proj, w_spatial, b_spatial_t, sgu_gain, wa, wb, wo)


def _router_kernel(h_ref, g2_ref, wrt_ref, brt_ref, ut_ref, xp_ref, idx_ref, rank_ref, gate_ref, cnt_ref,
                   carry_ref):
    @pl.when(pl.program_id(0) == 0)
    def _():
        carry_ref[...] = jnp.zeros_like(carry_ref)

    h = h_ref[...]
    half = h.shape[1] // 2
    ms = jnp.mean(h * h, axis=-1, keepdims=True)
    xn = h * lax.rsqrt(ms + EPS) * g2_ref[...]
    _store_rows_as_tiles(xp_ref, _pack_bf16_pair(xn[:, :half], xn[:, half:]))

    logits = lax.dot_general(wrt_ref[...], xn, (((1,), (1,)), ((), ())), preferred_element_type=F32,
                             precision=lax.Precision.HIGHEST) + brt_ref[...]
    sub = lax.broadcasted_iota(jnp.int32, logits.shape, 0).astype(F32)
    out_row = lax.broadcasted_iota(jnp.int32, idx_ref.shape, 0)
    work = logits
    vals, sels = [], []
    idx_out = jnp.zeros(idx_ref.shape, jnp.int32)
    for k in range(TOP_K):
        m = jnp.max(work, axis=0, keepdims=True)
        idx = jnp.min(jnp.where(work == m, sub, float(N_EXPERTS)), axis=0, keepdims=True)
        sel = sub == idx
        work = jnp.where(sel, -jnp.inf, work)
        vals.append(m)
        sels.append(sel)
        idx_out = jnp.where(out_row == k, idx.astype(jnp.int32), idx_out)
    idx_ref[...] = idx_out

    exps = [jnp.exp(v - vals[0]) for v in vals]
    denom = exps[0] + exps[1] + exps[2] + exps[3]
    gate_out = jnp.zeros(gate_ref.shape, F32)
    for k in range(TOP_K):
        gate_out = jnp.where(out_row == k, exps[k] / denom, gate_out)
    gate_ref[...] = gate_out

    member = (sels[0] | sels[1] | sels[2] | sels[3])
    before = jnp.dot(member.astype(BF16), ut_ref[...], preferred_element_type=F32)
    pos = carry_ref[...] + before
    rank_out = jnp.zeros(rank_ref.shape, jnp.int32)
    for k in range(TOP_K):
        r = jnp.sum(jnp.where(sels[k], pos, 0.0), axis=0, keepdims=True)
        rank_out = jnp.where(out_row == k, r.astype(jnp.int32), rank_out)
    rank_ref[...] = rank_out

    carry_ref[...] = carry_ref[...] + jnp.sum(member.astype(F32), axis=1, keepdims=True)
    cnt_ref[...] = jnp.broadcast_to(carry_ref[...], cnt_ref.shape)


def _router(h1, g2, w_router_t, b_router_t):
    t, d = h1.shape
    const = lambda shape: pl.BlockSpec(shape, lambda i: (0,) * len(shape))
    i = jnp.arange(ROUTER_TM)
    ut = (i[:, None] < i[None, :]).astype(BF16)
    rows, lanes = 8, 128
    tok = lambda: pl.BlockSpec((rows, ROUTER_TM), lambda i: (0, i))
    return pl.pallas_call(
        _router_kernel,
        grid=(t // ROUTER_TM,),
        in_specs=[
            pl.BlockSpec((ROUTER_TM, d), lambda i: (i, 0)),
            const((1, d)),
            const((N_EXPERTS, d)),
            const((N_EXPERTS, 1)),
            const((ROUTER_TM, ROUTER_TM)),
        ],
        out_specs=[
            pl.BlockSpec((ROUTER_TM * ROW_SUBLANES, LANES), lambda i: (i, 0)),
            tok(), tok(), tok(),
            pl.BlockSpec((N_EXPERTS, lanes), lambda i: (0, 0)),
        ],
        out_shape=[
            jax.ShapeDtypeStruct((t * ROW_SUBLANES, LANES), jnp.uint32),
            jax.ShapeDtypeStruct((rows, t), jnp.int32),
            jax.ShapeDtypeStruct((rows, t), jnp.int32),
            jax.ShapeDtypeStruct((rows, t), F32),
            jax.ShapeDtypeStruct((N_EXPERTS, lanes), F32),
        ],
        scratch_shapes=[pltpu.VMEM((N_EXPERTS, 1), F32)],
        compiler_params=pltpu.CompilerParams(
            dimension_semantics=("arbitrary",), vmem_limit_bytes=VMEM_LIMIT),
        name="router",
    )(h1, g2, w_router_t, b_router_t, ut)


def _dispatch_kernel(dest_ref, xp_ref, xs_in_ref, xs_ref, sem):
    del xs_in_ref
    i = pl.program_id(0)

    n_tok = dest_ref.shape[0] // TOP_K

    def row_copy(s, r):
        return pltpu.make_async_copy(_tile_of_row(xp_ref, s), _tile_of_row(xs_ref, r), sem)

    def issue(b, _):
        for j in range(DMA_UNROLL):
            s = b * DMA_UNROLL + j
            for k in range(TOP_K):
                row_copy(s, dest_ref[k * n_tok + i * DISPATCH_TB + s]).start(priority=k % 2)
        return 0

    def drain(b, _):
        for _ in range(DMA_UNROLL * TOP_K):
            row_copy(0, 0).wait()
        return 0

    lax.fori_loop(0, DISPATCH_TB // DMA_UNROLL, issue, 0)
    lax.fori_loop(0, DISPATCH_TB // DMA_UNROLL, drain, 0)


def _dispatch(dest_flat, xp, n_rows):
    t = xp.shape[0] // ROW_SUBLANES
    xs0 = jnp.zeros((n_rows * ROW_SUBLANES, LANES), jnp.uint32)
    return pl.pallas_call(
        _dispatch_kernel,
        grid_spec=pltpu.PrefetchScalarGridSpec(
            num_scalar_prefetch=1,
            grid=(t // DISPATCH_TB,),
            in_specs=[pl.BlockSpec((DISPATCH_TB * ROW_SUBLANES, LANES), lambda i, dest: (i, 0)),
                      pl.BlockSpec(memory_space=pl.ANY)],
            out_specs=pl.BlockSpec(memory_space=pl.ANY),
            scratch_shapes=[pltpu.SemaphoreType.DMA(())],
        ),
        out_shape=jax.ShapeDtypeStruct((n_rows * ROW_SUBLANES, LANES), jnp.uint32),
        input_output_aliases={2: 0},
        compiler_params=pltpu.CompilerParams(
            dimension_semantics=("arbitrary",), has_side_effects=True),
        name="dispatch",
    )(dest_flat, xp, xs0)


def _expert_kernel(ie_ref, it_ref, nv_ref, xs_ref, wg_ref, wu_ref, wd_ref, bg_ref, bu_ref, bd_ref,
                   y_ref, xb_ref, acc_ref):
    del ie_ref, it_ref
    i = pl.program_id(0)
    c = pl.program_id(1)
    nc = pl.num_programs(1)
    nv = nv_ref[i]
    half = ROW_SUBLANES * LANES

    @pl.when((c == 0) & (nv > 0))
    def _():
        acc_ref[...] = jnp.broadcast_to(bd_ref[0], acc_ref.shape)
        for s in range(ROW_SUBLANES):
            lo, hi = _unpack_bf16_pair(_load_tile_rows(xs_ref, s, EXP_TM))
            xb_ref[:, s * LANES:(s + 1) * LANES] = lo.astype(BF16)
            xb_ref[:, half + s * LANES:half + (s + 1) * LANES] = hi.astype(BF16)

    def run(n_rows):
        wg = wg_ref[0].astype(BF16)
        wu = wu_ref[0].astype(BF16)
        wd = wd_ref[0].astype(BF16)
        for lo in range(0, n_rows, EXP_SUB):
            rows = slice(lo, min(lo + EXP_SUB, n_rows))
            xb = xb_ref[rows, :]
            g = jnp.dot(xb, wg, preferred_element_type=F32) + bg_ref[0]
            u = jnp.dot(xb, wu, preferred_element_type=F32) + bu_ref[0]
            g = jnp.minimum(g, SWIGLU_LIMIT)
            u = jnp.clip(u, -SWIGLU_LIMIT, SWIGLU_LIMIT)
            a = g * _sigmoid(SWIGLU_ALPHA * g) * (u + 1.0)
            acc_ref[rows, :] += jnp.dot(a.astype(BF16), wd, preferred_element_type=F32)

    for n_rows in range(EXP_GRAN, EXP_TM + 1, EXP_GRAN):
        pl.when((nv > n_rows - EXP_GRAN) & (nv <= n_rows))(lambda n_rows=n_rows: run(n_rows))

    @pl.when((c == nc - 1) & (nv == 0))
    def _():
        y_ref[...] = jnp.zeros_like(y_ref)

    @pl.when((c == nc - 1) & (nv > 0))
    def _():
        for s in range(ROW_SUBLANES):
            lo = acc_ref[:, s * LANES:(s + 1) * LANES]
            hi = acc_ref[:, half + s * LANES:half + (s + 1) * LANES]
            y_ref[pl.ds(s, EXP_TM, stride=ROW_SUBLANES), :] = _pack_bf16_pair(lo, hi)


def _expert(item_e, item_t, item_nv, xs, w_up, b_up, w_down, b_down):
    n_rows = xs.shape[0] // ROW_SUBLANES
    ne, d, two_f = w_up.shape
    assert d == 2 * ROW_SUBLANES * LANES
    tile = (EXP_TM * ROW_SUBLANES, LANES)
    f = two_f // 2
    nc = f // EXP_TC
    n_items = item_e.shape[0]
    b_up3 = b_up.reshape(ne, 1, two_f)
    b_down3 = b_down.reshape(ne, 1, d)

    def chunk(c, nv_ref, i):
        return jnp.where(nv_ref[i] > 0, c, nc - 1)

    return pl.pallas_call(
        _expert_kernel,
        grid_spec=pltpu.PrefetchScalarGridSpec(
            num_scalar_prefetch=3,
            grid=(n_items, nc),
            in_specs=[
                pl.BlockSpec(tile, lambda i, c, ie, it, nv: (it[i], 0)),
                pl.BlockSpec((1, d, EXP_TC), lambda i, c, ie, it, nv: (ie[i], 0, chunk(c, nv, i))),
                pl.BlockSpec((1, d, EXP_TC), lambda i, c, ie, it, nv: (ie[i], 0, nc + chunk(c, nv, i))),
                pl.BlockSpec((1, EXP_TC, d), lambda i, c, ie, it, nv: (ie[i], chunk(c, nv, i), 0)),
                pl.BlockSpec((1, 1, EXP_TC), lambda i, c, ie, it, nv: (ie[i], 0, chunk(c, nv, i))),
                pl.BlockSpec((1, 1, EXP_TC), lambda i, c, ie, it, nv: (ie[i], 0, nc + chunk(c, nv, i))),
                pl.BlockSpec((1, 1, d), lambda i, c, ie, it, nv: (ie[i], 0, 0)),
            ],
            out_specs=pl.BlockSpec(tile, lambda i, c, ie, it, nv: (i, 0)),
            scratch_shapes=[pltpu.VMEM((EXP_TM, d), BF16), pltpu.VMEM((EXP_TM, d), F32)],
        ),
        out_shape=jax.ShapeDtypeStruct((n_rows * ROW_SUBLANES, LANES), jnp.uint32),
        compiler_params=pltpu.CompilerParams(
            dimension_semantics=("arbitrary", "arbitrary"), vmem_limit_bytes=VMEM_LIMIT),
        name="expert",
    )(item_e, item_t, item_nv, xs, w_up, w_up, w_down, b_up3, b_up3, b_down3)


def _combine_kernel(dest_ref, y_ref, h_ref, gate_ref, o_ref, buf_ref, sem):
    i = pl.program_id(0)
    n = pl.num_programs(0)
    n_tok = dest_ref.shape[0] // TOP_K

    def row_copy(r, slot, k, s):
        return pltpu.make_async_copy(_tile_of_row(y_ref, r), _tile_of_row(buf_ref.at[slot, k], s), sem.at[slot])

    def gather(step, slot):
        def issue(b, _):
            for j in range(DMA_UNROLL):
                s = b * DMA_UNROLL + j
                for k in range(TOP_K):
                    row_copy(dest_ref[k * n_tok + step * COMBINE_TB + s], slot, k, s).start(priority=k % 2)
            return 0
        lax.fori_loop(0, COMBINE_TB // DMA_UNROLL, issue, 0)

    @pl.when(i == 0)
    def _():
        gather(0, 0)

    for slot in range(2):
        @pl.when((i + 1 < n) & ((i + 1) % 2 == slot))
        def _():
            gather(i + 1, slot)

    for slot in range(2):
        @pl.when(i % 2 == slot)
        def _():
            def drain(b, _):
                for _ in range(DMA_UNROLL * TOP_K):
                    row_copy(0, slot, 0, 0).wait()
                return 0
            lax.fori_loop(0, COMBINE_TB // DMA_UNROLL, drain, 0)
            half = ROW_SUBLANES * LANES
            gates = [jnp.broadcast_to(gate_ref[:, k:k + 1], (COMBINE_TB, LANES)) for k in range(TOP_K)]
            for s in range(ROW_SUBLANES):
                lo_cols = slice(s * LANES, (s + 1) * LANES)
                hi_cols = slice(half + s * LANES, half + (s + 1) * LANES)
                acc_lo = h_ref[:, lo_cols]
                acc_hi = h_ref[:, hi_cols]
                for k in range(TOP_K):
                    lo, hi = _unpack_bf16_pair(_load_tile_rows(buf_ref.at[slot, k], s, COMBINE_TB))
                    acc_lo = acc_lo + gates[k] * lo
                    acc_hi = acc_hi + gates[k] * hi
                o_ref[:, lo_cols] = acc_lo
                o_ref[:, hi_cols] = acc_hi


def _combine(dest_flat, y, h1, gate):
    t, d = h1.shape
    lanes = gate.shape[1]
    return pl.pallas_call(
        _combine_kernel,
        grid_spec=pltpu.PrefetchScalarGridSpec(
            num_scalar_prefetch=1,
            grid=(t // COMBINE_TB,),
            in_specs=[
                pl.BlockSpec(memory_space=pl.ANY),
                pl.BlockSpec((COMBINE_TB, d), lambda i, dest: (i, 0)),
                pl.BlockSpec((COMBINE_TB, lanes), lambda i, dest: (i, 0)),
            ],
            out_specs=pl.BlockSpec((COMBINE_TB, d), lambda i, dest: (i, 0)),
            scratch_shapes=[pltpu.VMEM((2, TOP_K, COMBINE_TB * ROW_SUBLANES, LANES), jnp.uint32),
                            pltpu.SemaphoreType.DMA((2,))],
        ),
        out_shape=jax.ShapeDtypeStruct((t, d), F32),
        compiler_params=pltpu.CompilerParams(
            dimension_semantics=("arbitrary",), vmem_limit_bytes=VMEM_LIMIT),
        name="combine",
    )(dest_flat, y, h1, gate)


def _plan_items(counts, n_items_max):
    tiles = (counts + EXP_TM - 1) // EXP_TM
    tile_end = jnp.cumsum(tiles)
    tile_start = tile_end - tiles
    n_items = tile_end[-1]
    g = jnp.arange(n_items_max, dtype=jnp.int32)
    live = g < n_items
    gg = jnp.maximum(jnp.minimum(g, n_items - 1), 0)
    e = jnp.minimum(jnp.searchsorted(tile_end, gg, side="right"), N_EXPERTS - 1).astype(jnp.int32)
    nv = jnp.clip(counts[e] - (gg - tile_start[e]) * EXP_TM, 0, EXP_TM)
    nv = jnp.where(live, nv, 0).astype(jnp.int32)
    return tile_start * EXP_TM, e, gg.astype(jnp.int32), nv


def _layer(x2, bsz, seq, norm1_gain, w_in, q_norm_gain, k_norm_gain, sgu_norm_gain, w_spatial, b_spatial,
           w_branch_a, w_branch_b, w_out, norm2_gain, w_router, b_router, w_up, b_up, w_down, b_down):
    t, d = x2.shape
    proj = _proj(x2, norm1_gain[None, :], w_in, q_norm_gain[None, :], k_norm_gain[None, :])
    oa = _attn(proj, bsz, seq, d, q_norm_gain, k_norm_gain)
    h1 = _merge(x2, oa, proj, w_spatial, b_spatial.T, sgu_norm_gain[None, :],
                w_branch_a.astype(BF16), w_branch_b.astype(BF16), w_out.astype(BF16))
    xp, idx, rank, gate, cnt = _router(h1, norm2_gain[None, :], w_router.T, b_router[:, None])

    counts = cnt[:, 0].astype(jnp.int32)
    n_items_max = -(-(t * TOP_K) // EXP_TM) + N_EXPERTS
    seg_start, item_e, item_t, item_nv = _plan_items(counts, n_items_max)
    onehot = idx[:TOP_K, :, None] == jnp.arange(N_EXPERTS, dtype=jnp.int32)
    dest = (jnp.sum(jnp.where(onehot, seg_start.astype(jnp.int32), 0), axis=-1)
            + rank[:TOP_K]).reshape(-1)

    xs = _dispatch(dest, xp, n_items_max * EXP_TM)
    y = _expert(item_e, item_t, item_nv, xs, w_up, b_up, w_down, b_down)
    return _combine(dest, y, h1, gate[:TOP_K].T)


def kernel(x, norm1_gain, w_in, q_norm_gain, k_norm_gain, sgu_norm_gain, w_spatial, b_spatial, w_branch_a,
           w_branch_b, w_out, norm2_gain, w_router, b_router, w_up, b_up, w_down, b_down):
    bsz, seq, d = x.shape
    h = x.reshape(bsz * seq, d)
    for l in range(norm1_gain.shape[0]):
        h = _layer(h, bsz, seq, norm1_gain[l], w_in[l], q_norm_gain[l], k_norm_gain[l], sgu_norm_gain[l],
                   w_spatial[l], b_spatial[l], w_branch_a[l], w_branch_b[l], w_out[l], norm2_gain[l],
                   w_router[l], b_router[l], w_up[l], b_up[l], w_down[l], b_down[l])
    return h.reshape(bsz, seq, d)
```

```python
import math

import jax
import jax.numpy as jnp
from jax import lax
from jax.experimental import pallas as pl
from jax.experimental.pallas import tpu as pltpu

F32 = jnp.float32
BF16 = jnp.bfloat16

EPS = 1e-6
N_HEADS = 8
HEAD_DIM = 128
SGU_GROUPS = 8
SGU_BLOCK = 128
CHUNK = 64
N_EXPERTS = 32
TOP_K = 4
SWIGLU_LIMIT = 7.0
SWIGLU_ALPHA = 1.702

F32_EXP_ZERO_BELOW = -105.0

VMEM_LIMIT = 56 * 1024 * 1024
PROJ_TM = 1024
PROJ_TN = 512
PROJ_ROWS = 512
ATT_T = 256
ATT_HP = 4
MERGE_TM = 256
ROUTER_TM = 1024
EXP_TM = 1280
EXP_SUB = 512
EXP_GRAN = 128
EXP_TC = 256
DISPATCH_TB = 512
COMBINE_TB = 128
DMA_UNROLL = 8
FILL_MAX_ROWS = 4096


def _sigmoid(x):
    return 1.0 / (1.0 + jnp.exp(-x))


ROW_SUBLANES = 8
LANES = 128


def _store_rows_as_tiles(ref, rows2d):
    n = rows2d.shape[0]
    for s in range(ROW_SUBLANES):
        ref[pl.ds(s, n, stride=ROW_SUBLANES), :] = rows2d[:, s * LANES:(s + 1) * LANES]


def _tile_of_row(ref, r):
    start = r * ROW_SUBLANES
    if not isinstance(r, int):
        start = pl.multiple_of(start, ROW_SUBLANES)
    return ref.at[pl.ds(start, ROW_SUBLANES)]


def _load_tile_rows(ref, s, n):
    return ref[pl.ds(s, n, stride=ROW_SUBLANES), :]


def _pack_bf16_pair(lo, hi):
    lo_bits = lax.bitcast_convert_type(lo.astype(BF16).astype(F32), jnp.uint32)
    hi_bits = lax.bitcast_convert_type(hi.astype(BF16).astype(F32), jnp.uint32)
    return hi_bits | (lo_bits >> 16)


def _unpack_bf16_pair(w):
    lo = lax.bitcast_convert_type(w << 16, F32)
    hi = lax.bitcast_convert_type(w & jnp.uint32(0xFFFF0000), F32)
    return lo, hi


def _gelu_tanh(x):
    c = math.sqrt(2.0 / math.pi)
    return 0.5 * x * (1.0 + jnp.tanh(c * (x + 0.044715 * (x * x * x))))


def _proj_kernel(x_ref, g1_ref, w_ref, qg_ref, kg_ref, o_ref, xn_ref):
    j = pl.program_id(1)
    n_q = (N_HEADS * HEAD_DIM) // PROJ_TN

    @pl.when(j == 0)
    def _():
        x = x_ref[...]
        ms = jnp.mean(x * x, axis=-1, keepdims=True)
        xn_ref[...] = (x * lax.rsqrt(ms + EPS) * g1_ref[...]).astype(BF16)

    def head_norm(a):
        gain = jnp.where(j < n_q, qg_ref[...], kg_ref[...])
        outs = []
        for h in range(PROJ_TN // HEAD_DIM):
            ah = a[:, h * HEAD_DIM:(h + 1) * HEAD_DIM]
            ms = jnp.mean(ah * ah, axis=-1, keepdims=True)
            outs.append(ah * lax.rsqrt(ms + EPS) * gain)
        return jnp.concatenate(outs, axis=1)

    def tile(epilogue):
        w = w_ref[...].astype(BF16)
        for r in range(PROJ_TM // PROJ_ROWS):
            rows = slice(r * PROJ_ROWS, (r + 1) * PROJ_ROWS)
            acc = jnp.dot(xn_ref[rows, :], w, preferred_element_type=F32)
            o_ref[rows, :] = epilogue(acc).astype(BF16)

    pl.when(j < 2 * n_q)(lambda: tile(head_norm))
    pl.when((j >= 2 * n_q) & (j < 3 * n_q))(lambda: tile(lambda a: a))
    pl.when((j >= 3 * n_q) & (j < 5 * n_q))(lambda: tile(_gelu_tanh))
    pl.when(j >= 5 * n_q)(lambda: tile(_sigmoid))


def _proj(x2, g1, w_in, qg, kg):
    t, d = x2.shape
    n = w_in.shape[1]
    nj = n // PROJ_TN
    shift = (2 * d) // PROJ_TN
    return pl.pallas_call(
        _proj_kernel,
        grid=(t // PROJ_TM, nj),
        in_specs=[
            pl.BlockSpec((PROJ_TM, d), lambda i, j: (i, 0)),
            pl.BlockSpec((1, d), lambda i, j: (0, 0)),
            pl.BlockSpec((d, PROJ_TN), lambda i, j: (0, j)),
            pl.BlockSpec((1, HEAD_DIM), lambda i, j: (0, 0)),
            pl.BlockSpec((1, HEAD_DIM), lambda i, j: (0, 0)),
        ],
        out_specs=pl.BlockSpec((PROJ_TM, PROJ_TN), lambda i, j: (i, (j + shift) % nj)),
        out_shape=jax.ShapeDtypeStruct((t, n), BF16),
        scratch_shapes=[pltpu.VMEM((PROJ_TM, d), BF16)],
        compiler_params=pltpu.CompilerParams(
            dimension_semantics=("arbitrary", "arbitrary"), vmem_limit_bytes=VMEM_LIMIT),
        name="proj",
    )(x2, g1, w_in, qg, kg)


def _attn_kernel(stop_ref, q_ref, k_ref, v_ref, u_ref, o_ref):
    qi = pl.program_id(2)
    scale = HEAD_DIM ** -0.5
    stop_at = stop_ref[0]

    def block(j, accs, runs, diag):
        start = pl.multiple_of(j * ATT_T, ATT_T)
        tri = u_ref[...]
        if diag:
            row = lax.broadcasted_iota(jnp.int32, (ATT_T, ATT_T), 0)
            col = lax.broadcasted_iota(jnp.int32, (ATT_T, ATT_T), 1)
            past = col < row
        heads = range(ATT_HP)
        cols = [slice(h * HEAD_DIM, (h + 1) * HEAD_DIM) for h in heads]
        zs = [lax.dot_general(q_ref[:, cols[h]], k_ref[pl.ds(start, ATT_T), cols[h]],
                              (((1,), (1,)), ((), ())), preferred_element_type=F32) * scale for h in heads]
        sps = [jnp.maximum(z, 0.0) + jnp.log1p(jnp.exp(-jnp.abs(z))) for z in zs]
        if diag:
            sps = [jnp.where(past, sp, 0.0) for sp in sps]
        his = [sp.astype(BF16) for sp in sps]
        los = [(sp - hi.astype(F32)).astype(BF16) for sp, hi in zip(sps, his)]
        css = [jnp.dot(hi, tri, preferred_element_type=F32) + jnp.dot(lo, tri, preferred_element_type=F32)
               for hi, lo in zip(his, los)]
        ws = [jnp.exp(zs[h] - (runs[h] + css[h])) for h in heads]
        if diag:
            ws = [jnp.where(past, w, 0.0) for w in ws]
        new_accs = [accs[h] + jnp.dot(ws[h].astype(BF16), v_ref[pl.ds(start, ATT_T), cols[h]],
                                      preferred_element_type=F32) for h in heads]
        new_runs = [runs[h] + css[h][:, 0:1] for h in heads]
        return tuple(new_accs), tuple(new_runs)

    def smallest(runs):
        m = runs[0]
        for r in runs[1:]:
            m = jnp.minimum(m, r)
        return jnp.min(m)

    accs = tuple(jnp.zeros((ATT_T, HEAD_DIM), F32) for _ in range(ATT_HP))
    runs = tuple(jnp.zeros((ATT_T, 1), F32) for _ in range(ATT_HP))
    accs, runs = block(qi, accs, runs, True)

    def cond(c):
        j, _, _, low = c
        return (j >= 0) & (low < stop_at)

    def body(c):
        j, accs, runs, _ = c
        accs, runs = block(j, accs, runs, False)
        return j - 1, accs, runs, smallest(runs)

    _, accs, _, _ = lax.while_loop(cond, body, (qi - 1, accs, runs, smallest(runs)))
    for h in range(ATT_HP):
        o_ref[:, h * HEAD_DIM:(h + 1) * HEAD_DIM] = accs[h].astype(BF16)


def _attn(proj, bsz, seq, d, q_gain, k_gain):
    t = bsz * seq
    nq = seq // ATT_T
    width = ATT_HP * HEAD_DIM
    col0 = (2 * d) // width
    seg = (N_HEADS * HEAD_DIM) // width
    i = jnp.arange(ATT_T)
    tri = (i[:, None] >= i[None, :]).astype(BF16)
    zmax = 1.02 * math.sqrt(HEAD_DIM) * jnp.max(jnp.abs(q_gain)) * jnp.max(jnp.abs(k_gain))
    stop_at = (zmax - F32_EXP_ZERO_BELOW).reshape(1).astype(F32)
    return pl.pallas_call(
        _attn_kernel,
        grid_spec=pltpu.PrefetchScalarGridSpec(
            num_scalar_prefetch=1,
            grid=(bsz, seg, nq),
            in_specs=[
                pl.BlockSpec((ATT_T, width), lambda b, h, qi, s: (b * nq + qi, col0 + h)),
                pl.BlockSpec((seq, width), lambda b, h, qi, s: (b, col0 + seg + h)),
                pl.BlockSpec((seq, width), lambda b, h, qi, s: (b, col0 + 2 * seg + h)),
                pl.BlockSpec((ATT_T, ATT_T), lambda b, h, qi, s: (0, 0)),
            ],
            out_specs=pl.BlockSpec((ATT_T, width), lambda b, h, qi, s: (b * nq + qi, h)),
        ),
        out_shape=jax.ShapeDtypeStruct((t, N_HEADS * HEAD_DIM), BF16),
        compiler_params=pltpu.CompilerParams(
            dimension_semantics=("arbitrary", "arbitrary", "arbitrary"), vmem_limit_bytes=VMEM_LIMIT),
        name="attn",
    )(stop_at, proj, proj, proj, tri)


def _merge_kernel(x_ref, oa_ref, u_ref, vv_ref, ga_ref, gb_ref, ws_ref, bst_ref, sg_ref,
                  wa_ref, wb_ref, wo_ref, h_ref, fill_ref, vln_ref, ob_ref, zero_ref, fill_sem):
    i = pl.program_id(0)
    z_rows = zero_ref.shape[0]
    n_fill = fill_ref.shape[0] // (pl.num_programs(0) * z_rows)

    @pl.when(i == 0)
    def _():
        zero_ref[...] = jnp.zeros_like(zero_ref)

    def fill_copy(j):
        start = pl.multiple_of((i * n_fill + j) * z_rows, ROW_SUBLANES)
        return pltpu.make_async_copy(zero_ref, fill_ref.at[pl.ds(start, z_rows)], fill_sem)

    for j in range(n_fill):
        fill_copy(j).start()

    vv = vv_ref[...].astype(F32)
    mu = jnp.mean(vv, axis=-1, keepdims=True)
    xc = vv - mu
    var = jnp.mean(xc * xc, axis=-1, keepdims=True)
    vln_ref[...] = (xc * lax.rsqrt(var + EPS) * sg_ref[...]).astype(BF16)

    row = lax.broadcasted_iota(jnp.int32, (SGU_BLOCK, SGU_BLOCK), 0)
    col = lax.broadcasted_iota(jnp.int32, (SGU_BLOCK, SGU_BLOCK), 1)
    mask = (col // CHUNK) <= (row // CHUNK)
    for g in range(SGU_GROUPS):
        wg = jnp.where(mask, ws_ref[g], 0.0).astype(BF16)
        bg = bst_ref[:, g:g + 1]
        cs = slice(g * SGU_BLOCK, (g + 1) * SGU_BLOCK)
        for n in range(MERGE_TM // SGU_BLOCK):
            rs = slice(n * SGU_BLOCK, (n + 1) * SGU_BLOCK)
            sgu = jnp.dot(wg, vln_ref[rs, cs], preferred_element_type=F32) + bg
            ob_ref[rs, cs] = (u_ref[rs, cs].astype(F32) * sgu).astype(BF16)

    ya = jnp.dot(oa_ref[...], wa_ref[...], preferred_element_type=F32)
    yb = jnp.dot(ob_ref[...], wb_ref[...], preferred_element_type=F32)
    y = ga_ref[...].astype(F32) * ya + gb_ref[...].astype(F32) * yb
    h_ref[...] = x_ref[...] + jnp.dot(y.astype(BF16), wo_ref[...], preferred_element_type=F32)

    for j in range(n_fill):
        fill_copy(j).wait()


def _fill_rows_per_copy(rows_per_step):
    for n in range(1, rows_per_step + 1):
        if rows_per_step % n == 0 and (rows_per_step // n) % ROW_SUBLANES == 0 and rows_per_step // n <= FILL_MAX_ROWS:
            return rows_per_step // n
    raise ValueError(f"no aligned split of {rows_per_step} fill rows")


def _merge(x2, oa, proj, w_spatial, b_spatial_t, sgu_gain, wa, wb, wo, n_fill_rows):
    t, d = x2.shape
    aw = oa.shape[1]
    sw = SGU_GROUPS * SGU_BLOCK
    ucol = (2 * d + 3 * aw) // sw
    const = lambda shape: pl.BlockSpec(shape, lambda i: (0,) * len(shape), pipeline_mode=pl.Buffered(1))
    n_steps = t // MERGE_TM
    assert n_fill_rows % n_steps == 0
    z_rows = _fill_rows_per_copy(n_fill_rows // n_steps)
    return pl.pallas_call(
        _merge_kernel,
        grid=(t // MERGE_TM,),
        in_specs=[
            pl.BlockSpec((MERGE_TM, d), lambda i: (i, 0)),
            pl.BlockSpec((MERGE_TM, aw), lambda i: (i, 0)),
            pl.BlockSpec((MERGE_TM, sw), lambda i: (i, ucol)),
            pl.BlockSpec((MERGE_TM, sw), lambda i: (i, ucol + 1)),
            pl.BlockSpec((MERGE_TM, d), lambda i: (i, 0)),
            pl.BlockSpec((MERGE_TM, d), lambda i: (i, 1)),
            const((SGU_GROUPS, SGU_BLOCK, SGU_BLOCK)),
            const((SGU_BLOCK, SGU_GROUPS)),
            const((1, sw)),
            const((aw, d)),
            const((sw, d)),
            const((d, d)),
        ],
        out_specs=[pl.BlockSpec((MERGE_TM, d), lambda i: (i, 0)), pl.BlockSpec(memory_space=pl.ANY)],
        out_shape=[jax.ShapeDtypeStruct((t, d), F32),
                   jax.ShapeDtypeStruct((n_fill_rows, LANES), jnp.uint32)],
        scratch_shapes=[pltpu.VMEM((MERGE_TM, sw), BF16), pltpu.VMEM((MERGE_TM, sw), BF16),
                        pltpu.VMEM((z_rows, LANES), jnp.uint32), pltpu.SemaphoreType.DMA(())],
        compiler_params=pltpu.CompilerParams(
            dimension_semantics=("arbitrary",), vmem_limit_bytes=VMEM_LIMIT),
        name="merge",
    )(x2, oa, proj, proj, proj, proj, w_spatial, b_spatial_t, sgu_gain, wa, wb, wo)


def _router_kernel(h_ref, g2_ref, wrt_ref, brt_ref, ut_ref, xp_ref, idx_ref, rank_ref, gate_ref, cnt_ref,
                   carry_ref):
    @pl.when(pl.program_id(0) == 0)
    def _():
        carry_ref[...] = jnp.zeros_like(carry_ref)

    h = h_ref[...]
    half = h.shape[1] // 2
    ms = jnp.mean(h * h, axis=-1, keepdims=True)
    xn = h * lax.rsqrt(ms + EPS) * g2_ref[...]
    _store_rows_as_tiles(xp_ref, _pack_bf16_pair(xn[:, :half], xn[:, half:]))

    logits = lax.dot_general(wrt_ref[...], xn, (((1,), (1,)), ((), ())), preferred_element_type=F32,
                             precision=lax.Precision.HIGHEST) + brt_ref[...]
    sub = lax.broadcasted_iota(jnp.int32, logits.shape, 0).astype(F32)
    out_row = lax.broadcasted_iota(jnp.int32, idx_ref.shape, 0)
    work = logits
    vals, sels = [], []
    idx_out = jnp.zeros(idx_ref.shape, jnp.int32)
    for k in range(TOP_K):
        m = jnp.max(work, axis=0, keepdims=True)
        idx = jnp.min(jnp.where(work == m, sub, float(N_EXPERTS)), axis=0, keepdims=True)
        sel = sub == idx
        work = jnp.where(sel, -jnp.inf, work)
        vals.append(m)
        sels.append(sel)
        idx_out = jnp.where(out_row == k, idx.astype(jnp.int32), idx_out)
    idx_ref[...] = idx_out

    exps = [jnp.exp(v - vals[0]) for v in vals]
    denom = exps[0] + exps[1] + exps[2] + exps[3]
    gate_out = jnp.zeros(gate_ref.shape, F32)
    for k in range(TOP_K):
        gate_out = jnp.where(out_row == k, exps[k] / denom, gate_out)
    gate_ref[...] = gate_out

    member = (sels[0] | sels[1] | sels[2] | sels[3])
    before = jnp.dot(member.astype(BF16), ut_ref[...], preferred_element_type=F32)
    pos = carry_ref[...] + before
    rank_out = jnp.zeros(rank_ref.shape, jnp.int32)
    for k in range(TOP_K):
        r = jnp.sum(jnp.where(sels[k], pos, 0.0), axis=0, keepdims=True)
        rank_out = jnp.where(out_row == k, r.astype(jnp.int32), rank_out)
    rank_ref[...] = rank_out

    carry_ref[...] = carry_ref[...] + jnp.sum(member.astype(F32), axis=1, keepdims=True)
    cnt_ref[...] = jnp.broadcast_to(carry_ref[...], cnt_ref.shape)


def _router(h1, g2, w_router_t, b_router_t):
    t, d = h1.shape
    const = lambda shape: pl.BlockSpec(shape, lambda i: (0,) * len(shape))
    i = jnp.arange(ROUTER_TM)
    ut = (i[:, None] < i[None, :]).astype(BF16)
    rows, lanes = 8, 128
    tok = lambda: pl.BlockSpec((rows, ROUTER_TM), lambda i: (0, i))
    return pl.pallas_call(
        _router_kernel,
        grid=(t // ROUTER_TM,),
        in_specs=[
            pl.BlockSpec((ROUTER_TM, d), lambda i: (i, 0)),
            const((1, d)),
            const((N_EXPERTS, d)),
            const((N_EXPERTS, 1)),
            const((ROUTER_TM, ROUTER_TM)),
        ],
        out_specs=[
            pl.BlockSpec((ROUTER_TM * ROW_SUBLANES, LANES), lambda i: (i, 0)),
            tok(), tok(), tok(),
            pl.BlockSpec((N_EXPERTS, lanes), lambda i: (0, 0)),
        ],
        out_shape=[
            jax.ShapeDtypeStruct((t * ROW_SUBLANES, LANES), jnp.uint32),
            jax.ShapeDtypeStruct((rows, t), jnp.int32),
            jax.ShapeDtypeStruct((rows, t), jnp.int32),
            jax.ShapeDtypeStruct((rows, t), F32),
            jax.ShapeDtypeStruct((N_EXPERTS, lanes), F32),
        ],
        scratch_shapes=[pltpu.VMEM((N_EXPERTS, 1), F32)],
        compiler_params=pltpu.CompilerParams(
            dimension_semantics=("arbitrary",), vmem_limit_bytes=VMEM_LIMIT),
        name="router",
    )(h1, g2, w_router_t, b_router_t, ut)


def _dispatch_kernel(dest_ref, xp_ref, xs_in_ref, xs_ref, sem):
    del xs_in_ref
    i = pl.program_id(0)

    n_tok = dest_ref.shape[0] // TOP_K

    def row_copy(s, r):
        return pltpu.make_async_copy(_tile_of_row(xp_ref, s), _tile_of_row(xs_ref, r), sem)

    def issue(b, _):
        for j in range(DMA_UNROLL):
            s = b * DMA_UNROLL + j
            for k in range(TOP_K):
                row_copy(s, dest_ref[k * n_tok + i * DISPATCH_TB + s]).start(priority=k % 2)
        return 0

    def drain(b, _):
        for _ in range(DMA_UNROLL * TOP_K):
            row_copy(0, 0).wait()
        return 0

    lax.fori_loop(0, DISPATCH_TB // DMA_UNROLL, issue, 0)
    lax.fori_loop(0, DISPATCH_TB // DMA_UNROLL, drain, 0)


def _dispatch(dest_flat, xp, xs0):
    t = xp.shape[0] // ROW_SUBLANES
    n_rows = xs0.shape[0] // ROW_SUBLANES
    return pl.pallas_call(
        _dispatch_kernel,
        grid_spec=pltpu.PrefetchScalarGridSpec(
            num_scalar_prefetch=1,
            grid=(t // DISPATCH_TB,),
            in_specs=[pl.BlockSpec((DISPATCH_TB * ROW_SUBLANES, LANES), lambda i, dest: (i, 0)),
                      pl.BlockSpec(memory_space=pl.ANY)],
            out_specs=pl.BlockSpec(memory_space=pl.ANY),
            scratch_shapes=[pltpu.SemaphoreType.DMA(())],
        ),
        out_shape=jax.ShapeDtypeStruct((n_rows * ROW_SUBLANES, LANES), jnp.uint32),
        input_output_aliases={2: 0},
        compiler_params=pltpu.CompilerParams(
            dimension_semantics=("arbitrary",), has_side_effects=True),
        name="dispatch",
    )(dest_flat, xp, xs0)


def _expert_kernel(ie_ref, it_ref, nv_ref, xs_ref, wg_ref, wu_ref, wd_ref, bg_ref, bu_ref, bd_ref,
                   y_ref, acc_ref):
    del ie_ref, it_ref
    i = pl.program_id(0)
    c = pl.program_id(1)
    nc = pl.num_programs(1)
    nv = nv_ref[i]
    half = ROW_SUBLANES * LANES

    @pl.when((i == 0) & (c == 0))
    def _():
        acc_ref[...] = jnp.zeros_like(acc_ref)

    def run(n_rows):
        wg = wg_ref[0].astype(BF16)
        wu = wu_ref[0].astype(BF16)
        wd = wd_ref[0].astype(BF16)
        first = c == 0
        for lo in range(0, n_rows, EXP_SUB):
            m = min(EXP_SUB, n_rows - lo)
            rows = slice(lo, lo + m)
            pairs = [_unpack_bf16_pair(xs_ref[pl.ds(lo * ROW_SUBLANES + s, m, stride=ROW_SUBLANES), :])
                     for s in range(ROW_SUBLANES)]
            xb = jnp.concatenate([p[0].astype(BF16) for p in pairs] + [p[1].astype(BF16) for p in pairs], axis=1)
            g = jnp.dot(xb, wg, preferred_element_type=F32) + bg_ref[0]
            u = jnp.dot(xb, wu, preferred_element_type=F32) + bu_ref[0]
            g = jnp.minimum(g, SWIGLU_LIMIT)
            u = jnp.clip(u, -SWIGLU_LIMIT, SWIGLU_LIMIT)
            a = g * _sigmoid(SWIGLU_ALPHA * g) * (u + 1.0)
            y = jnp.dot(a.astype(BF16), wd, preferred_element_type=F32)
            acc_ref[rows, :] = jnp.where(first, jnp.broadcast_to(bd_ref[0], y.shape), acc_ref[rows, :]) + y

    for n_rows in range(EXP_GRAN, EXP_TM + 1, EXP_GRAN):
        pl.when((nv > n_rows - EXP_GRAN) & (nv <= n_rows))(lambda n_rows=n_rows: run(n_rows))

    @pl.when((c == nc - 1) & (nv == 0))
    def _():
        y_ref[...] = jnp.zeros_like(y_ref)

    @pl.when((c == nc - 1) & (nv > 0))
    def _():
        for s in range(ROW_SUBLANES):
            lo = acc_ref[:, s * LANES:(s + 1) * LANES]
            hi = acc_ref[:, half + s * LANES:half + (s + 1) * LANES]
            y_ref[pl.ds(s, EXP_TM, stride=ROW_SUBLANES), :] = _pack_bf16_pair(lo, hi)


def _expert(item_e, item_t, item_nv, xs, w_up, b_up, w_down, b_down):
    n_rows = xs.shape[0] // ROW_SUBLANES
    ne, d, two_f = w_up.shape
    assert d == 2 * ROW_SUBLANES * LANES
    tile = (EXP_TM * ROW_SUBLANES, LANES)
    f = two_f // 2
    nc = f // EXP_TC
    n_items = item_e.shape[0]
    b_up3 = b_up.reshape(ne, 1, two_f)
    b_down3 = b_down.reshape(ne, 1, d)

    def chunk(c, nv_ref, i):
        return jnp.where(nv_ref[i] > 0, c, nc - 1)

    return pl.pallas_call(
        _expert_kernel,
        grid_spec=pltpu.PrefetchScalarGridSpec(
            num_scalar_prefetch=3,
            grid=(n_items, nc),
            in_specs=[
                pl.BlockSpec(tile, lambda i, c, ie, it, nv: (it[i], 0)),
                pl.BlockSpec((1, d, EXP_TC), lambda i, c, ie, it, nv: (ie[i], 0, chunk(c, nv, i))),
                pl.BlockSpec((1, d, EXP_TC), lambda i, c, ie, it, nv: (ie[i], 0, nc + chunk(c, nv, i))),
                pl.BlockSpec((1, EXP_TC, d), lambda i, c, ie, it, nv: (ie[i], chunk(c, nv, i), 0)),
                pl.BlockSpec((1, 1, EXP_TC), lambda i, c, ie, it, nv: (ie[i], 0, chunk(c, nv, i))),
                pl.BlockSpec((1, 1, EXP_TC), lambda i, c, ie, it, nv: (ie[i], 0, nc + chunk(c, nv, i))),
                pl.BlockSpec((1, 1, d), lambda i, c, ie, it, nv: (ie[i], 0, 0)),
            ],
            out_specs=pl.BlockSpec(tile, lambda i, c, ie, it, nv: (i, 0)),
            scratch_shapes=[pltpu.VMEM((EXP_TM, d), F32)],
        ),
        out_shape=jax.ShapeDtypeStruct((n_rows * ROW_SUBLANES, LANES), jnp.uint32),
        compiler_params=pltpu.CompilerParams(
            dimension_semantics=("arbitrary", "arbitrary"), vmem_limit_bytes=VMEM_LIMIT),
        name="expert",
    )(item_e, item_t, item_nv, xs, w_up, w_up, w_down, b_up3, b_up3, b_down3)


def _combine_kernel(dest_ref, y_ref, h_ref, gate_ref, o_ref, buf_ref, sem):
    i = pl.program_id(0)
    n = pl.num_programs(0)
    n_tok = dest_ref.shape[0] // TOP_K

    def row_copy(r, slot, k, s):
        return pltpu.make_async_copy(_tile_of_row(y_ref, r), _tile_of_row(buf_ref.at[slot, k], s), sem.at[slot])

    def gather(step, slot):
        def issue(b, _):
            for j in range(DMA_UNROLL):
                s = b * DMA_UNROLL + j
                for k in range(TOP_K):
                    row_copy(dest_ref[k * n_tok + step * COMBINE_TB + s], slot, k, s).start(priority=k % 2)
            return 0
        lax.fori_loop(0, COMBINE_TB // DMA_UNROLL, issue, 0)

    @pl.when(i == 0)
    def _():
        gather(0, 0)

    for slot in range(2):
        @pl.when((i + 1 < n) & ((i + 1) % 2 == slot))
        def _():
            gather(i + 1, slot)

    for slot in range(2):
        @pl.when(i % 2 == slot)
        def _():
            def drain(b, _):
                for _ in range(DMA_UNROLL * TOP_K):
                    row_copy(0, slot, 0, 0).wait()
                return 0
            lax.fori_loop(0, COMBINE_TB // DMA_UNROLL, drain, 0)
            half = ROW_SUBLANES * LANES
            gates = [jnp.broadcast_to(gate_ref[:, k:k + 1], (COMBINE_TB, LANES)) for k in range(TOP_K)]
            for s in range(ROW_SUBLANES):
                lo_cols = slice(s * LANES, (s + 1) * LANES)
                hi_cols = slice(half + s * LANES, half + (s + 1) * LANES)
                acc_lo = h_ref[:, lo_cols]
                acc_hi = h_ref[:, hi_cols]
                for k in range(TOP_K):
                    lo, hi = _unpack_bf16_pair(_load_tile_rows(buf_ref.at[slot, k], s, COMBINE_TB))
                    acc_lo = acc_lo + gates[k] * lo
                    acc_hi = acc_hi + gates[k] * hi
                o_ref[:, lo_cols] = acc_lo
                o_ref[:, hi_cols] = acc_hi


def _combine(dest_flat, y, h1, gate):
    t, d = h1.shape
    lanes = gate.shape[1]
    return pl.pallas_call(
        _combine_kernel,
        grid_spec=pltpu.PrefetchScalarGridSpec(
            num_scalar_prefetch=1,
            grid=(t // COMBINE_TB,),
            in_specs=[
                pl.BlockSpec(memory_space=pl.ANY),
                pl.BlockSpec((COMBINE_TB, d), lambda i, dest: (i, 0)),
                pl.BlockSpec((COMBINE_TB, lanes), lambda i, dest: (i, 0)),
            ],
            out_specs=pl.BlockSpec((COMBINE_TB, d), lambda i, dest: (i, 0)),
            scratch_shapes=[pltpu.VMEM((2, TOP_K, COMBINE_TB * ROW_SUBLANES, LANES), jnp.uint32),
                            pltpu.SemaphoreType.DMA((2,))],
        ),
        out_shape=jax.ShapeDtypeStruct((t, d), F32),
        compiler_params=pltpu.CompilerParams(
            dimension_semantics=("arbitrary",), vmem_limit_bytes=VMEM_LIMIT),
        name="combine",
    )(dest_flat, y, h1, gate)


def _plan_items(counts, n_items_max):
    tiles = (counts + EXP_TM - 1) // EXP_TM
    tile_end = jnp.cumsum(tiles)
    tile_start = tile_end - tiles
    n_items = tile_end[-1]
    g = jnp.arange(n_items_max, dtype=jnp.int32)
    live = g < n_items
    gg = jnp.maximum(jnp.minimum(g, n_items - 1), 0)
    e = jnp.minimum(jnp.searchsorted(tile_end, gg, side="right"), N_EXPERTS - 1).astype(jnp.int32)
    nv = jnp.clip(counts[e] - (gg - tile_start[e]) * EXP_TM, 0, EXP_TM)
    nv = jnp.where(live, nv, 0).astype(jnp.int32)
    return tile_start * EXP_TM, e, gg.astype(jnp.int32), nv


def _layer(x2, bsz, seq, norm1_gain, w_in, q_norm_gain, k_norm_gain, sgu_norm_gain, w_spatial, b_spatial,
           w_branch_a, w_branch_b, w_out, norm2_gain, w_router, b_router, w_up, b_up, w_down, b_down):
    t, d = x2.shape
    proj = _proj(x2, norm1_gain[None, :], w_in, q_norm_gain[None, :], k_norm_gain[None, :])
    oa = _attn(proj, bsz, seq, d, q_norm_gain, k_norm_gain)
    n_items_max = -(-(t * TOP_K) // EXP_TM) + N_EXPERTS
    h1, xs0 = _merge(x2, oa, proj, w_spatial, b_spatial.T, sgu_norm_gain[None, :],
                     w_branch_a.astype(BF16), w_branch_b.astype(BF16), w_out.astype(BF16),
                     n_items_max * EXP_TM * ROW_SUBLANES)
    xp, idx, rank, gate, cnt = _router(h1, norm2_gain[None, :], w_router.T, b_router[:, None])

    counts = cnt[:, 0].astype(jnp.int32)
    seg_start, item_e, item_t, item_nv = _plan_items(counts, n_items_max)
    onehot = idx[:TOP_K, :, None] == jnp.arange(N_EXPERTS, dtype=jnp.int32)
    dest = (jnp.sum(jnp.where(onehot, seg_start.astype(jnp.int32), 0), axis=-1)
            + rank[:TOP_K]).reshape(-1)

    xs = _dispatch(dest, xp, xs0)
    y = _expert(item_e, item_t, item_nv, xs, w_up, b_up, w_down, b_down)
    return _combine(dest, y, h1, gate[:TOP_K].T)


def kernel(x, norm1_gain, w_in, q_norm_gain, k_norm_gain, sgu_norm_gain, w_spatial, b_spatial, w_branch_a,
           w_branch_b, w_out, norm2_gain, w_router, b_router, w_up, b_up, w_down, b_down):
    bsz, seq, d = x.shape
    h = x.reshape(bsz * seq, d)
    for l in range(norm1_gain.shape[0]):
        h = _layer(h, bsz, seq, norm1_gain[l], w_in[l], q_norm_gain[l], k_norm_gain[l], sgu_norm_gain[l],
                   w_spatial[l], b_spatial[l], w_branch_a[l], w_branch_b[l], w_out[l], norm2_gain[l],
                   w_router[l], b_router[l], w_up[l], b_up[l], w_down[l], b_down[l])
    return h.reshape(bsz, seq, d)
```

```python
import math

import jax
import jax.numpy as jnp
from jax import lax
from jax.experimental import pallas as pl
from jax.experimental.pallas import tpu as pltpu

F32 = jnp.float32
BF16 = jnp.bfloat16

EPS = 1e-6
N_HEADS = 8
HEAD_DIM = 128
SGU_GROUPS = 8
SGU_BLOCK = 128
CHUNK = 64
N_EXPERTS = 32
TOP_K = 4
SWIGLU_LIMIT = 7.0
SWIGLU_ALPHA = 1.702

F32_EXP_ZERO_BELOW = -105.0
LOG2_E = 1.4426950408889634

VMEM_LIMIT = 56 * 1024 * 1024
PROJ_TM = 1024
PROJ_TN = 1024
PROJ_ROWS = 512
ATT_T = 256
ATT_HP = 8
MERGE_TM = 256
ROUTER_TM = 1024
EXP_TM = 1280
EXP_SUB = 512
EXP_GRAN = 128
EXP_TC = 256
DISPATCH_TB = 512
COMBINE_TB = 128
DMA_UNROLL = 8
FILL_MAX_ROWS = 4096


def _sigmoid(x):
    return 1.0 / (1.0 + jnp.exp(-x))


ROW_SUBLANES = 8
LANES = 128


def _store_rows_as_tiles(ref, rows2d):
    n = rows2d.shape[0]
    for s in range(ROW_SUBLANES):
        ref[pl.ds(s, n, stride=ROW_SUBLANES), :] = rows2d[:, s * LANES:(s + 1) * LANES]


def _tile_of_row(ref, r):
    start = r * ROW_SUBLANES
    if not isinstance(r, int):
        start = pl.multiple_of(start, ROW_SUBLANES)
    return ref.at[pl.ds(start, ROW_SUBLANES)]


def _load_tile_rows(ref, s, n):
    return ref[pl.ds(s, n, stride=ROW_SUBLANES), :]


def _pack_bf16_pair(lo, hi):
    lo_bits = lax.bitcast_convert_type(lo.astype(BF16).astype(F32), jnp.uint32)
    hi_bits = lax.bitcast_convert_type(hi.astype(BF16).astype(F32), jnp.uint32)
    return hi_bits | (lo_bits >> 16)


def _unpack_bf16_pair(w):
    lo = lax.bitcast_convert_type(w << 16, F32)
    hi = lax.bitcast_convert_type(w & jnp.uint32(0xFFFF0000), F32)
    return lo, hi


def _gelu_tanh(x):
    c = math.sqrt(2.0 / math.pi)
    return 0.5 * x * (1.0 + jnp.tanh(c * (x + 0.044715 * (x * x * x))))


def _proj_kernel(x_ref, g1_ref, w_ref, qg_ref, kg_ref, o_ref, xn_ref):
    j = pl.program_id(1)
    n_q = (N_HEADS * HEAD_DIM) // PROJ_TN

    @pl.when(j == 0)
    def _():
        x = x_ref[...]
        ms = jnp.mean(x * x, axis=-1, keepdims=True)
        xn_ref[...] = (x * lax.rsqrt(ms + EPS) * g1_ref[...]).astype(BF16)

    def head_norm(a):
        gain = jnp.where(j < n_q, qg_ref[...], kg_ref[...])
        outs = []
        for h in range(PROJ_TN // HEAD_DIM):
            ah = a[:, h * HEAD_DIM:(h + 1) * HEAD_DIM]
            ms = jnp.mean(ah * ah, axis=-1, keepdims=True)
            outs.append(ah * lax.rsqrt(ms + EPS) * gain)
        return jnp.concatenate(outs, axis=1)

    def tile(epilogue):
        w = w_ref[...].astype(BF16)
        for r in range(PROJ_TM // PROJ_ROWS):
            rows = slice(r * PROJ_ROWS, (r + 1) * PROJ_ROWS)
            acc = jnp.dot(xn_ref[rows, :], w, preferred_element_type=F32)
            o_ref[rows, :] = epilogue(acc).astype(BF16)

    pl.when(j < 2 * n_q)(lambda: tile(head_norm))
    pl.when((j >= 2 * n_q) & (j < 3 * n_q))(lambda: tile(lambda a: a))
    pl.when((j >= 3 * n_q) & (j < 5 * n_q))(lambda: tile(_gelu_tanh))
    pl.when(j >= 5 * n_q)(lambda: tile(_sigmoid))


def _proj(x2, g1, w_in, qg, kg):
    t, d = x2.shape
    n = w_in.shape[1]
    nj = n // PROJ_TN
    shift = (2 * d) // PROJ_TN
    return pl.pallas_call(
        _proj_kernel,
        grid=(t // PROJ_TM, nj),
        in_specs=[
            pl.BlockSpec((PROJ_TM, d), lambda i, j: (i, 0)),
            pl.BlockSpec((1, d), lambda i, j: (0, 0)),
            pl.BlockSpec((d, PROJ_TN), lambda i, j: (0, j)),
            pl.BlockSpec((1, HEAD_DIM), lambda i, j: (0, 0)),
            pl.BlockSpec((1, HEAD_DIM), lambda i, j: (0, 0)),
        ],
        out_specs=pl.BlockSpec((PROJ_TM, PROJ_TN), lambda i, j: (i, (j + shift) % nj)),
        out_shape=jax.ShapeDtypeStruct((t, n), BF16),
        scratch_shapes=[pltpu.VMEM((PROJ_TM, d), BF16)],
        compiler_params=pltpu.CompilerParams(
            dimension_semantics=("arbitrary", "arbitrary"), vmem_limit_bytes=VMEM_LIMIT),
        name="proj",
    )(x2, g1, w_in, qg, kg)


def _attn_kernel(stop_ref, q_ref, k_ref, v_ref, u_ref, o_ref):
    qi = pl.program_id(2)
    scale = HEAD_DIM ** -0.5 * LOG2_E
    stop_at = stop_ref[0]

    def block(j, accs, runs, diag):
        start = pl.multiple_of(j * ATT_T, ATT_T)
        tri = u_ref[...]
        if diag:
            row = lax.broadcasted_iota(jnp.int32, (ATT_T, ATT_T), 0)
            col = lax.broadcasted_iota(jnp.int32, (ATT_T, ATT_T), 1)
            past = col < row
        heads = range(ATT_HP)
        cols = [slice(h * HEAD_DIM, (h + 1) * HEAD_DIM) for h in heads]
        zs = [lax.dot_general(q_ref[:, cols[h]], k_ref[pl.ds(start, ATT_T), cols[h]],
                              (((1,), (1,)), ((), ())), preferred_element_type=F32) * scale for h in heads]
        sps = [jnp.maximum(z, 0.0) + jnp.log(1.0 + jnp.exp2(-jnp.abs(z))) * LOG2_E for z in zs]
        if diag:
            sps = [jnp.where(past, sp, 0.0) for sp in sps]
        his = [sp.astype(BF16) for sp in sps]
        los = [(sp - hi.astype(F32)).astype(BF16) for sp, hi in zip(sps, his)]
        css = [jnp.dot(hi, tri, preferred_element_type=F32) + jnp.dot(lo, tri, preferred_element_type=F32)
               for hi, lo in zip(his, los)]
        ws = [jnp.exp2(zs[h] - (runs[h] + css[h])) for h in heads]
        if diag:
            ws = [jnp.where(past, w, 0.0) for w in ws]
        new_accs = [accs[h] + jnp.dot(ws[h].astype(BF16), v_ref[pl.ds(start, ATT_T), cols[h]],
                                      preferred_element_type=F32) for h in heads]
        new_runs = [runs[h] + css[h][:, 0:1] for h in heads]
        return tuple(new_accs), tuple(new_runs)

    def smallest(runs):
        m = runs[0]
        for r in runs[1:]:
            m = jnp.minimum(m, r)
        return jnp.min(m)

    accs = tuple(jnp.zeros((ATT_T, HEAD_DIM), F32) for _ in range(ATT_HP))
    runs = tuple(jnp.zeros((ATT_T, 1), F32) for _ in range(ATT_HP))
    accs, runs = block(qi, accs, runs, True)

    def cond(c):
        j, _, _, low = c
        return (j >= 0) & (low < stop_at)

    def body(c):
        j, accs, runs, _ = c
        accs, runs = block(j, accs, runs, False)
        return j - 1, accs, runs, smallest(runs)

    _, accs, _, _ = lax.while_loop(cond, body, (qi - 1, accs, runs, smallest(runs)))
    for h in range(ATT_HP):
        o_ref[:, h * HEAD_DIM:(h + 1) * HEAD_DIM] = accs[h].astype(BF16)


def _attn(proj, bsz, seq, d, q_gain, k_gain):
    t = bsz * seq
    nq = seq // ATT_T
    width = ATT_HP * HEAD_DIM
    col0 = (2 * d) // width
    seg = (N_HEADS * HEAD_DIM) // width
    i = jnp.arange(ATT_T)
    tri = (i[:, None] >= i[None, :]).astype(BF16)
    zmax = 1.02 * math.sqrt(HEAD_DIM) * jnp.max(jnp.abs(q_gain)) * jnp.max(jnp.abs(k_gain))
    stop_at = ((zmax - F32_EXP_ZERO_BELOW) * LOG2_E).reshape(1).astype(F32)
    return pl.pallas_call(
        _attn_kernel,
        grid_spec=pltpu.PrefetchScalarGridSpec(
            num_scalar_prefetch=1,
            grid=(bsz, seg, nq),
            in_specs=[
                pl.BlockSpec((ATT_T, width), lambda b, h, qi, s: (b * nq + qi, col0 + h)),
                pl.BlockSpec((seq, width), lambda b, h, qi, s: (b, col0 + seg + h)),
                pl.BlockSpec((seq, width), lambda b, h, qi, s: (b, col0 + 2 * seg + h)),
                pl.BlockSpec((ATT_T, ATT_T), lambda b, h, qi, s: (0, 0)),
            ],
            out_specs=pl.BlockSpec((ATT_T, width), lambda b, h, qi, s: (b * nq + qi, h)),
        ),
        out_shape=jax.ShapeDtypeStruct((t, N_HEADS * HEAD_DIM), BF16),
        compiler_params=pltpu.CompilerParams(
            dimension_semantics=("arbitrary", "arbitrary", "arbitrary"), vmem_limit_bytes=VMEM_LIMIT),
        name="attn",
    )(stop_at, proj, proj, proj, tri)


def _merge_kernel(x_ref, oa_ref, u_ref, vv_ref, ga_ref, gb_ref, ws_ref, bst_ref, sg_ref,
                  wa_ref, wb_ref, wo_ref, h_ref, fill_ref, vln_ref, ob_ref, zero_ref, fill_sem):
    i = pl.program_id(0)
    z_rows = zero_ref.shape[0]
    n_fill = fill_ref.shape[0] // (pl.num_programs(0) * z_rows)

    @pl.when(i == 0)
    def _():
        zero_ref[...] = jnp.zeros_like(zero_ref)

    def fill_copy(j):
        start = pl.multiple_of((i * n_fill + j) * z_rows, ROW_SUBLANES)
        return pltpu.make_async_copy(zero_ref, fill_ref.at[pl.ds(start, z_rows)], fill_sem)

    for j in range(n_fill):
        fill_copy(j).start()

    vv = vv_ref[...].astype(F32)
    mu = jnp.mean(vv, axis=-1, keepdims=True)
    xc = vv - mu
    var = jnp.mean(xc * xc, axis=-1, keepdims=True)
    vln_ref[...] = (xc * lax.rsqrt(var + EPS) * sg_ref[...]).astype(BF16)

    row = lax.broadcasted_iota(jnp.int32, (SGU_BLOCK, SGU_BLOCK), 0)
    col = lax.broadcasted_iota(jnp.int32, (SGU_BLOCK, SGU_BLOCK), 1)
    mask = (col // CHUNK) <= (row // CHUNK)
    for g in range(SGU_GROUPS):
        wg = jnp.where(mask, ws_ref[g], 0.0).astype(BF16)
        bg = bst_ref[:, g:g + 1]
        cs = slice(g * SGU_BLOCK, (g + 1) * SGU_BLOCK)
        for n in range(MERGE_TM // SGU_BLOCK):
            rs = slice(n * SGU_BLOCK, (n + 1) * SGU_BLOCK)
            sgu = jnp.dot(wg, vln_ref[rs, cs], preferred_element_type=F32) + bg
            ob_ref[rs, cs] = (u_ref[rs, cs].astype(F32) * sgu).astype(BF16)

    ya = jnp.dot(oa_ref[...], wa_ref[...], preferred_element_type=F32)
    yb = jnp.dot(ob_ref[...], wb_ref[...], preferred_element_type=F32)
    y = ga_ref[...].astype(F32) * ya + gb_ref[...].astype(F32) * yb
    h_ref[...] = x_ref[...] + jnp.dot(y.astype(BF16), wo_ref[...], preferred_element_type=F32)

    for j in range(n_fill):
        fill_copy(j).wait()


def _fill_rows_per_copy(rows_per_step):
    for n in range(1, rows_per_step + 1):
        if rows_per_step % n == 0 and (rows_per_step // n) % ROW_SUBLANES == 0 and rows_per_step // n <= FILL_MAX_ROWS:
            return rows_per_step // n
    raise ValueError(f"no aligned split of {rows_per_step} fill rows")


def _merge(x2, oa, proj, w_spatial, b_spatial_t, sgu_gain, wa, wb, wo, n_fill_rows):
    t, d = x2.shape
    aw = oa.shape[1]
    sw = SGU_GROUPS * SGU_BLOCK
    ucol = (2 * d + 3 * aw) // sw
    const = lambda shape: pl.BlockSpec(shape, lambda i: (0,) * len(shape), pipeline_mode=pl.Buffered(1))
    n_steps = t // MERGE_TM
    assert n_fill_rows % n_steps == 0
    z_rows = _fill_rows_per_copy(n_fill_rows // n_steps)
    return pl.pallas_call(
        _merge_kernel,
        grid=(t // MERGE_TM,),
        in_specs=[
            pl.BlockSpec((MERGE_TM, d), lambda i: (i, 0)),
            pl.BlockSpec((MERGE_TM, aw), lambda i: (i, 0)),
            pl.BlockSpec((MERGE_TM, sw), lambda i: (i, ucol)),
            pl.BlockSpec((MERGE_TM, sw), lambda i: (i, ucol + 1)),
            pl.BlockSpec((MERGE_TM, d), lambda i: (i, 0)),
            pl.BlockSpec((MERGE_TM, d), lambda i: (i, 1)),
            const((SGU_GROUPS, SGU_BLOCK, SGU_BLOCK)),
            const((SGU_BLOCK, SGU_GROUPS)),
            const((1, sw)),
            const((aw, d)),
            const((sw, d)),
            const((d, d)),
        ],
        out_specs=[pl.BlockSpec((MERGE_TM, d), lambda i: (i, 0)), pl.BlockSpec(memory_space=pl.ANY)],
        out_shape=[jax.ShapeDtypeStruct((t, d), F32),
                   jax.ShapeDtypeStruct((n_fill_rows, LANES), jnp.uint32)],
        scratch_shapes=[pltpu.VMEM((MERGE_TM, sw), BF16), pltpu.VMEM((MERGE_TM, sw), BF16),
                        pltpu.VMEM((z_rows, LANES), jnp.uint32), pltpu.SemaphoreType.DMA(())],
        compiler_params=pltpu.CompilerParams(
            dimension_semantics=("arbitrary",), vmem_limit_bytes=VMEM_LIMIT),
        name="merge",
    )(x2, oa, proj, proj, proj, proj, w_spatial, b_spatial_t, sgu_gain, wa, wb, wo)


def _router_kernel(h_ref, g2_ref, wrt_ref, brt_ref, ut_ref, xp_ref, idx_ref, rank_ref, gate_ref, cnt_ref,
                   carry_ref):
    @pl.when(pl.program_id(0) == 0)
    def _():
        carry_ref[...] = jnp.zeros_like(carry_ref)

    h = h_ref[...]
    half = h.shape[1] // 2
    ms = jnp.mean(h * h, axis=-1, keepdims=True)
    xn = h * lax.rsqrt(ms + EPS) * g2_ref[...]
    xn_hi = xn.astype(BF16)
    xn_lo = (xn - xn_hi.astype(F32)).astype(BF16)
    hi_bits = lax.bitcast_convert_type(xn_hi.astype(F32), jnp.uint32)
    _store_rows_as_tiles(xp_ref, hi_bits[:, half:] | (hi_bits[:, :half] >> 16))

    wr = wrt_ref[...]
    wr_hi = wr.astype(BF16)
    wr_lo = (wr - wr_hi.astype(F32)).astype(BF16)
    nt = (((1,), (1,)), ((), ()))
    logits = (lax.dot_general(wr_hi, xn_hi, nt, preferred_element_type=F32)
              + lax.dot_general(wr_hi, xn_lo, nt, preferred_element_type=F32)
              + lax.dot_general(wr_lo, xn_hi, nt, preferred_element_type=F32)) + brt_ref[...]
    sub = lax.broadcasted_iota(jnp.int32, logits.shape, 0).astype(F32)
    out_row = lax.broadcasted_iota(jnp.int32, idx_ref.shape, 0)
    work = logits
    vals, sels = [], []
    idx_out = jnp.zeros(idx_ref.shape, jnp.int32)
    for k in range(TOP_K):
        m = jnp.max(work, axis=0, keepdims=True)
        idx = jnp.min(jnp.where(work == m, sub, float(N_EXPERTS)), axis=0, keepdims=True)
        sel = sub == idx
        work = jnp.where(sel, -jnp.inf, work)
        vals.append(m)
        sels.append(sel)
        idx_out = jnp.where(out_row == k, idx.astype(jnp.int32), idx_out)
    idx_ref[...] = idx_out

    exps = [jnp.exp(v - vals[0]) for v in vals]
    denom = exps[0] + exps[1] + exps[2] + exps[3]
    gate_out = jnp.zeros(gate_ref.shape, F32)
    for k in range(TOP_K):
        gate_out = jnp.where(out_row == k, exps[k] / denom, gate_out)
    gate_ref[...] = gate_out

    member = (sels[0] | sels[1] | sels[2] | sels[3])
    before = jnp.dot(member.astype(BF16), ut_ref[...], preferred_element_type=F32)
    pos = carry_ref[...] + before
    rank_out = jnp.zeros(rank_ref.shape, jnp.int32)
    for k in range(TOP_K):
        r = jnp.sum(jnp.where(sels[k], pos, 0.0), axis=0, keepdims=True)
        rank_out = jnp.where(out_row == k, r.astype(jnp.int32), rank_out)
    rank_ref[...] = rank_out

    carry_ref[...] = carry_ref[...] + jnp.sum(member.astype(F32), axis=1, keepdims=True)
    cnt_ref[...] = jnp.broadcast_to(carry_ref[...], cnt_ref.shape)


def _router(h1, g2, w_router_t, b_router_t):
    t, d = h1.shape
    const = lambda shape: pl.BlockSpec(shape, lambda i: (0,) * len(shape))
    i = jnp.arange(ROUTER_TM)
    ut = (i[:, None] < i[None, :]).astype(BF16)
    rows, lanes = 8, 128
    tok = lambda: pl.BlockSpec((rows, ROUTER_TM), lambda i: (0, i))
    return pl.pallas_call(
        _router_kernel,
        grid=(t // ROUTER_TM,),
        in_specs=[
            pl.BlockSpec((ROUTER_TM, d), lambda i: (i, 0)),
            const((1, d)),
            const((N_EXPERTS, d)),
            const((N_EXPERTS, 1)),
            const((ROUTER_TM, ROUTER_TM)),
        ],
        out_specs=[
            pl.BlockSpec((ROUTER_TM * ROW_SUBLANES, LANES), lambda i: (i, 0)),
            tok(), tok(), tok(),
            pl.BlockSpec((N_EXPERTS, lanes), lambda i: (0, 0)),
        ],
        out_shape=[
            jax.ShapeDtypeStruct((t * ROW_SUBLANES, LANES), jnp.uint32),
            jax.ShapeDtypeStruct((rows, t), jnp.int32),
            jax.ShapeDtypeStruct((rows, t), jnp.int32),
            jax.ShapeDtypeStruct((rows, t), F32),
            jax.ShapeDtypeStruct((N_EXPERTS, lanes), F32),
        ],
        scratch_shapes=[pltpu.VMEM((N_EXPERTS, 1), F32)],
        compiler_params=pltpu.CompilerParams(
            dimension_semantics=("arbitrary",), vmem_limit_bytes=VMEM_LIMIT),
        name="router",
    )(h1, g2, w_router_t, b_router_t, ut)


def _dispatch_kernel(dest_ref, xp_ref, xs_in_ref, xs_ref, sem):
    del xs_in_ref
    i = pl.program_id(0)

    n_tok = dest_ref.shape[0] // TOP_K

    def row_copy(s, r):
        return pltpu.make_async_copy(_tile_of_row(xp_ref, s), _tile_of_row(xs_ref, r), sem)

    def issue(b, _):
        for j in range(DMA_UNROLL):
            s = b * DMA_UNROLL + j
            for k in range(TOP_K):
                row_copy(s, dest_ref[k * n_tok + i * DISPATCH_TB + s]).start(priority=k % 2)
        return 0

    def drain(b, _):
        for _ in range(DMA_UNROLL * TOP_K):
            row_copy(0, 0).wait()
        return 0

    lax.fori_loop(0, DISPATCH_TB // DMA_UNROLL, issue, 0)
    lax.fori_loop(0, DISPATCH_TB // DMA_UNROLL, drain, 0)


def _dispatch(dest_flat, xp, xs0):
    t = xp.shape[0] // ROW_SUBLANES
    n_rows = xs0.shape[0] // ROW_SUBLANES
    return pl.pallas_call(
        _dispatch_kernel,
        grid_spec=pltpu.PrefetchScalarGridSpec(
            num_scalar_prefetch=1,
            grid=(t // DISPATCH_TB,),
            in_specs=[pl.BlockSpec((DISPATCH_TB * ROW_SUBLANES, LANES), lambda i, dest: (i, 0)),
                      pl.BlockSpec(memory_space=pl.ANY)],
            out_specs=pl.BlockSpec(memory_space=pl.ANY),
            scratch_shapes=[pltpu.SemaphoreType.DMA(())],
        ),
        out_shape=jax.ShapeDtypeStruct((n_rows * ROW_SUBLANES, LANES), jnp.uint32),
        input_output_aliases={2: 0},
        compiler_params=pltpu.CompilerParams(
            dimension_semantics=("arbitrary",), has_side_effects=True),
        name="dispatch",
    )(dest_flat, xp, xs0)


def _expert_kernel(ie_ref, it_ref, nv_ref, xs_ref, wg_ref, wu_ref, wd_ref, bg_ref, bu_ref, bd_ref,
                   y_ref, acc_ref):
    del ie_ref, it_ref
    i = pl.program_id(0)
    c = pl.program_id(1)
    nc = pl.num_programs(1)
    nv = nv_ref[i]
    half = ROW_SUBLANES * LANES

    @pl.when((i == 0) & (c == 0))
    def _():
        acc_ref[...] = jnp.zeros_like(acc_ref)

    def run(n_rows):
        wg = wg_ref[0].astype(BF16)
        wu = wu_ref[0].astype(BF16)
        wd = wd_ref[0].astype(BF16)
        first = c == 0
        for lo in range(0, n_rows, EXP_SUB):
            m = min(EXP_SUB, n_rows - lo)
            rows = slice(lo, lo + m)
            pairs = [_unpack_bf16_pair(xs_ref[pl.ds(lo * ROW_SUBLANES + s, m, stride=ROW_SUBLANES), :])
                     for s in range(ROW_SUBLANES)]
            xb = jnp.concatenate([p[0].astype(BF16) for p in pairs] + [p[1].astype(BF16) for p in pairs], axis=1)
            g = jnp.dot(xb, wg, preferred_element_type=F32) + bg_ref[0]
            u = jnp.dot(xb, wu, preferred_element_type=F32) + bu_ref[0]
            g = jnp.minimum(g, SWIGLU_LIMIT)
            u = jnp.clip(u, -SWIGLU_LIMIT, SWIGLU_LIMIT)
            a = g * _sigmoid(SWIGLU_ALPHA * g) * (u + 1.0)
            y = jnp.dot(a.astype(BF16), wd, preferred_element_type=F32)
            acc_ref[rows, :] = jnp.where(first, jnp.broadcast_to(bd_ref[0], y.shape), acc_ref[rows, :]) + y

    for n_rows in range(EXP_GRAN, EXP_TM + 1, EXP_GRAN):
        pl.when((nv > n_rows - EXP_GRAN) & (nv <= n_rows))(lambda n_rows=n_rows: run(n_rows))

    @pl.when((c == nc - 1) & (nv == 0))
    def _():
        y_ref[...] = jnp.zeros_like(y_ref)

    @pl.when((c == nc - 1) & (nv > 0))
    def _():
        for s in range(ROW_SUBLANES):
            lo = acc_ref[:, s * LANES:(s + 1) * LANES]
            hi = acc_ref[:, half + s * LANES:half + (s + 1) * LANES]
            y_ref[pl.ds(s, EXP_TM, stride=ROW_SUBLANES), :] = _pack_bf16_pair(lo, hi)


def _expert(item_e, item_t, item_nv, xs, w_up, b_up, w_down, b_down):
    n_rows = xs.shape[0] // ROW_SUBLANES
    ne, d, two_f = w_up.shape
    assert d == 2 * ROW_SUBLANES * LANES
    tile = (EXP_TM * ROW_SUBLANES, LANES)
    f = two_f // 2
    nc = f // EXP_TC
    n_items = item_e.shape[0]
    b_up3 = b_up.reshape(ne, 1, two_f)
    b_down3 = b_down.reshape(ne, 1, d)

    def chunk(c, nv_ref, i):
        return jnp.where(nv_ref[i] > 0, c, nc - 1)

    return pl.pallas_call(
        _expert_kernel,
        grid_spec=pltpu.PrefetchScalarGridSpec(
            num_scalar_prefetch=3,
            grid=(n_items, nc),
            in_specs=[
                pl.BlockSpec(tile, lambda i, c, ie, it, nv: (it[i], 0)),
                pl.BlockSpec((1, d, EXP_TC), lambda i, c, ie, it, nv: (ie[i], 0, chunk(c, nv, i))),
                pl.BlockSpec((1, d, EXP_TC), lambda i, c, ie, it, nv: (ie[i], 0, nc + chunk(c, nv, i))),
                pl.BlockSpec((1, EXP_TC, d), lambda i, c, ie, it, nv: (ie[i], chunk(c, nv, i), 0)),
                pl.BlockSpec((1, 1, EXP_TC), lambda i, c, ie, it, nv: (ie[i], 0, chunk(c, nv, i))),
                pl.BlockSpec((1, 1, EXP_TC), lambda i, c, ie, it, nv: (ie[i], 0, nc + chunk(c, nv, i))),
                pl.BlockSpec((1, 1, d), lambda i, c, ie, it, nv: (ie[i], 0, 0)),
            ],
            out_specs=pl.BlockSpec(tile, lambda i, c, ie, it, nv: (i, 0)),
            scratch_shapes=[pltpu.VMEM((EXP_TM, d), F32)],
        ),
        out_shape=jax.ShapeDtypeStruct((n_rows * ROW_SUBLANES, LANES), jnp.uint32),
        compiler_params=pltpu.CompilerParams(
            dimension_semantics=("arbitrary", "arbitrary"), vmem_limit_bytes=VMEM_LIMIT),
        name="expert",
    )(item_e, item_t, item_nv, xs, w_up, w_up, w_down, b_up3, b_up3, b_down3)


def _combine_kernel(dest_ref, y_ref, h_ref, gate_ref, o_ref, buf_ref, sem):
    i = pl.program_id(0)
    n = pl.num_programs(0)
    n_tok = dest_ref.shape[0] // TOP_K

    def row_copy(r, slot, k, s):
        return pltpu.make_async_copy(_tile_of_row(y_ref, r), _tile_of_row(buf_ref.at[slot, k], s), sem.at[slot])

    def gather(step, slot):
        def issue(b, _):
            for j in range(DMA_UNROLL):
                s = b * DMA_UNROLL + j
                for k in range(TOP_K):
                    row_copy(dest_ref[k * n_tok + step * COMBINE_TB + s], slot, k, s).start(priority=k % 2)
            return 0
        lax.fori_loop(0, COMBINE_TB // DMA_UNROLL, issue, 0)

    @pl.when(i == 0)
    def _():
        gather(0, 0)

    for slot in range(2):
        @pl.when((i + 1 < n) & ((i + 1) % 2 == slot))
        def _():
            gather(i + 1, slot)

    for slot in range(2):
        @pl.when(i % 2 == slot)
        def _():
            def drain(b, _):
                for _ in range(DMA_UNROLL * TOP_K):
                    row_copy(0, slot, 0, 0).wait()
                return 0
            lax.fori_loop(0, COMBINE_TB // DMA_UNROLL, drain, 0)
            half = ROW_SUBLANES * LANES
            gates = [jnp.broadcast_to(gate_ref[:, k:k + 1], (COMBINE_TB, LANES)) for k in range(TOP_K)]
            for s in range(ROW_SUBLANES):
                lo_cols = slice(s * LANES, (s + 1) * LANES)
                hi_cols = slice(half + s * LANES, half + (s + 1) * LANES)
                acc_lo = h_ref[:, lo_cols]
                acc_hi = h_ref[:, hi_cols]
                for k in range(TOP_K):
                    lo, hi = _unpack_bf16_pair(_load_tile_rows(buf_ref.at[slot, k], s, COMBINE_TB))
                    acc_lo = acc_lo + gates[k] * lo
                    acc_hi = acc_hi + gates[k] * hi
                o_ref[:, lo_cols] = acc_lo
                o_ref[:, hi_cols] = acc_hi


def _combine(dest_flat, y, h1, gate):
    t, d = h1.shape
    lanes = gate.shape[1]
    return pl.pallas_call(
        _combine_kernel,
        grid_spec=pltpu.PrefetchScalarGridSpec(
            num_scalar_prefetch=1,
            grid=(t // COMBINE_TB,),
            in_specs=[
                pl.BlockSpec(memory_space=pl.ANY),
                pl.BlockSpec((COMBINE_TB, d), lambda i, dest: (i, 0)),
                pl.BlockSpec((COMBINE_TB, lanes), lambda i, dest: (i, 0)),
            ],
            out_specs=pl.BlockSpec((COMBINE_TB, d), lambda i, dest: (i, 0)),
            scratch_shapes=[pltpu.VMEM((2, TOP_K, COMBINE_TB * ROW_SUBLANES, LANES), jnp.uint32),
                            pltpu.SemaphoreType.DMA((2,))],
        ),
        out_shape=jax.ShapeDtypeStruct((t, d), F32),
        compiler_params=pltpu.CompilerParams(
            dimension_semantics=("arbitrary",), vmem_limit_bytes=VMEM_LIMIT),
        name="combine",
    )(dest_flat, y, h1, gate)


def _plan_items(counts, n_items_max):
    tiles = (counts + EXP_TM - 1) // EXP_TM
    tile_end = jnp.cumsum(tiles)
    tile_start = tile_end - tiles
    n_items = tile_end[-1]
    g = jnp.arange(n_items_max, dtype=jnp.int32)
    live = g < n_items
    gg = jnp.maximum(jnp.minimum(g, n_items - 1), 0)
    e = jnp.minimum(jnp.searchsorted(tile_end, gg, side="right"), N_EXPERTS - 1).astype(jnp.int32)
    nv = jnp.clip(counts[e] - (gg - tile_start[e]) * EXP_TM, 0, EXP_TM)
    nv = jnp.where(live, nv, 0).astype(jnp.int32)
    return tile_start * EXP_TM, e, gg.astype(jnp.int32), nv


def _layer(x2, bsz, seq, norm1_gain, w_in, q_norm_gain, k_norm_gain, sgu_norm_gain, w_spatial, b_spatial,
           w_branch_a, w_branch_b, w_out, norm2_gain, w_router, b_router, w_up, b_up, w_down, b_down):
    t, d = x2.shape
    proj = _proj(x2, norm1_gain[None, :], w_in, q_norm_gain[None, :], k_norm_gain[None, :])
    oa = _attn(proj, bsz, seq, d, q_norm_gain, k_norm_gain)
    n_items_max = -(-(t * TOP_K) // EXP_TM) + N_EXPERTS
    h1, xs0 = _merge(x2, oa, proj, w_spatial, b_spatial.T, sgu_norm_gain[None, :],
                     w_branch_a.astype(BF16), w_branch_b.astype(BF16), w_out.astype(BF16),
                     n_items_max * EXP_TM * ROW_SUBLANES)
    xp, idx, rank, gate, cnt = _router(h1, norm2_gain[None, :], w_router.T, b_router[:, None])

    counts = cnt[:, 0].astype(jnp.int32)
    seg_start, item_e, item_t, item_nv = _plan_items(counts, n_items_max)
    onehot = idx[:TOP_K, :, None] == jnp.arange(N_EXPERTS, dtype=jnp.int32)
    dest = (jnp.sum(jnp.where(onehot, seg_start.astype(jnp.int32), 0), axis=-1)
            + rank[:TOP_K]).reshape(-1)

    xs = _dispatch(dest, xp, xs0)
    y = _expert(item_e, item_t, item_nv, xs, w_up, b_up, w_down, b_down)
    return _combine(dest, y, h1, gate[:TOP_K].T)


def kernel(x, norm1_gain, w_in, q_norm_gain, k_norm_gain, sgu_norm_gain, w_spatial, b_spatial, w_branch_a,
           w_branch_b, w_out, norm2_gain, w_router, b_router, w_up, b_up, w_down, b_down):
    bsz, seq, d = x.shape
    h = x.reshape(bsz * seq, d)
    for l in range(norm1_gain.shape[0]):
        h = _layer(h, bsz, seq, norm1_gain[l], w_in[l], q_norm_gain[l], k_norm_gain[l], sgu_norm_gain[l],
                   w_spatial[l], b_spatial[l], w_branch_a[l], w_branch_b[l], w_out[l], norm2_gain[l],
                   w_router[l], b_router[l], w_up[l], b_up[l], w_down[l], b_down[l])
    return h.reshape(bsz, seq, d)
```

```python
import math

import jax
import jax.numpy as jnp
from jax import lax
from jax.experimental import pallas as pl
from jax.experimental.pallas import tpu as pltpu

F32 = jnp.float32
BF16 = jnp.bfloat16

EPS = 1e-6
N_HEADS = 8
HEAD_DIM = 128
SGU_GROUPS = 8
SGU_BLOCK = 128
CHUNK = 64
N_EXPERTS = 32
TOP_K = 4
SWIGLU_LIMIT = 7.0
SWIGLU_ALPHA = 1.702

F32_EXP_ZERO_BELOW = -105.0
LOG2_E = 1.4426950408889634

VMEM_LIMIT = 56 * 1024 * 1024
PROJ_TM = 1024
PROJ_TN = 1024
PROJ_ROWS = 512
ATT_T = 256
ATT_HP = 8
MERGE_TM = 256
ROUTER_TM = 1024
EXP_TM = 1280
EXP_SUB = 1280
EXP_GRAN = 128
EXP_TC = 256
DISPATCH_TB = 512
COMBINE_TB = 128
DMA_UNROLL = 8
FILL_MAX_ROWS = 4096


def _sigmoid(x):
    return 1.0 / (1.0 + jnp.exp(-x))


ROW_SUBLANES = 8
LANES = 128


def _store_rows_as_tiles(ref, rows2d):
    n = rows2d.shape[0]
    for s in range(ROW_SUBLANES):
        ref[pl.ds(s, n, stride=ROW_SUBLANES), :] = rows2d[:, s * LANES:(s + 1) * LANES]


def _tile_of_row(ref, r):
    start = r * ROW_SUBLANES
    if not isinstance(r, int):
        start = pl.multiple_of(start, ROW_SUBLANES)
    return ref.at[pl.ds(start, ROW_SUBLANES)]


def _load_tile_rows(ref, s, n):
    return ref[pl.ds(s, n, stride=ROW_SUBLANES), :]


def _pack_bf16_pair(lo, hi):
    lo_bits = lax.bitcast_convert_type(lo.astype(BF16).astype(F32), jnp.uint32)
    hi_bits = lax.bitcast_convert_type(hi.astype(BF16).astype(F32), jnp.uint32)
    return hi_bits | (lo_bits >> 16)


def _unpack_bf16_pair(w):
    lo = lax.bitcast_convert_type(w << 16, F32)
    hi = lax.bitcast_convert_type(w & jnp.uint32(0xFFFF0000), F32)
    return lo, hi


def _gelu_tanh(x):
    c = math.sqrt(2.0 / math.pi)
    return 0.5 * x * (1.0 + jnp.tanh(c * (x + 0.044715 * (x * x * x))))


def _proj_kernel(x_ref, g1_ref, w_ref, qg_ref, kg_ref, o_ref, xn_ref):
    j = pl.program_id(1)
    n_q = (N_HEADS * HEAD_DIM) // PROJ_TN

    @pl.when(j == 0)
    def _():
        x = x_ref[...]
        ms = jnp.mean(x * x, axis=-1, keepdims=True)
        xn_ref[...] = (x * lax.rsqrt(ms + EPS) * g1_ref[...]).astype(BF16)

    def head_norm(a):
        gain = jnp.where(j < n_q, qg_ref[...], kg_ref[...])
        outs = []
        for h in range(PROJ_TN // HEAD_DIM):
            ah = a[:, h * HEAD_DIM:(h + 1) * HEAD_DIM]
            ms = jnp.mean(ah * ah, axis=-1, keepdims=True)
            outs.append(ah * lax.rsqrt(ms + EPS) * gain)
        return jnp.concatenate(outs, axis=1)

    def tile(epilogue):
        w = w_ref[...].astype(BF16)
        for r in range(PROJ_TM // PROJ_ROWS):
            rows = slice(r * PROJ_ROWS, (r + 1) * PROJ_ROWS)
            acc = jnp.dot(xn_ref[rows, :], w, preferred_element_type=F32)
            o_ref[rows, :] = epilogue(acc).astype(BF16)

    pl.when(j < 2 * n_q)(lambda: tile(head_norm))
    pl.when((j >= 2 * n_q) & (j < 3 * n_q))(lambda: tile(lambda a: a))
    pl.when((j >= 3 * n_q) & (j < 5 * n_q))(lambda: tile(_gelu_tanh))
    pl.when(j >= 5 * n_q)(lambda: tile(_sigmoid))


def _proj(x2, g1, w_in, qg, kg):
    t, d = x2.shape
    n = w_in.shape[1]
    nj = n // PROJ_TN
    shift = (2 * d) // PROJ_TN
    return pl.pallas_call(
        _proj_kernel,
        grid=(t // PROJ_TM, nj),
        in_specs=[
            pl.BlockSpec((PROJ_TM, d), lambda i, j: (i, 0)),
            pl.BlockSpec((1, d), lambda i, j: (0, 0)),
            pl.BlockSpec((d, PROJ_TN), lambda i, j: (0, j)),
            pl.BlockSpec((1, HEAD_DIM), lambda i, j: (0, 0)),
            pl.BlockSpec((1, HEAD_DIM), lambda i, j: (0, 0)),
        ],
        out_specs=pl.BlockSpec((PROJ_TM, PROJ_TN), lambda i, j: (i, (j + shift) % nj)),
        out_shape=jax.ShapeDtypeStruct((t, n), BF16),
        scratch_shapes=[pltpu.VMEM((PROJ_TM, d), BF16)],
        compiler_params=pltpu.CompilerParams(
            dimension_semantics=("arbitrary", "arbitrary"), vmem_limit_bytes=VMEM_LIMIT),
        name="proj",
    )(x2, g1, w_in, qg, kg)


def _attn_kernel(stop_ref, q_ref, k_ref, v_ref, u_ref, o_ref):
    qi = pl.program_id(2)
    scale = HEAD_DIM ** -0.5 * LOG2_E
    stop_at = stop_ref[0]

    def block(j, accs, runs, diag):
        start = pl.multiple_of(j * ATT_T, ATT_T)
        tri = u_ref[...]
        if diag:
            row = lax.broadcasted_iota(jnp.int32, (ATT_T, ATT_T), 0)
            col = lax.broadcasted_iota(jnp.int32, (ATT_T, ATT_T), 1)
            past = col < row
        heads = range(ATT_HP)
        cols = [slice(h * HEAD_DIM, (h + 1) * HEAD_DIM) for h in heads]
        zs = [lax.dot_general(q_ref[:, cols[h]], k_ref[pl.ds(start, ATT_T), cols[h]],
                              (((1,), (1,)), ((), ())), preferred_element_type=F32) * scale for h in heads]
        sps = [jnp.maximum(z, 0.0) + jnp.log(1.0 + jnp.exp2(-jnp.abs(z))) * LOG2_E for z in zs]
        if diag:
            sps = [jnp.where(past, sp, 0.0) for sp in sps]
        his = [sp.astype(BF16) for sp in sps]
        los = [(sp - hi.astype(F32)).astype(BF16) for sp, hi in zip(sps, his)]
        css = [jnp.dot(hi, tri, preferred_element_type=F32) + jnp.dot(lo, tri, preferred_element_type=F32)
               for hi, lo in zip(his, los)]
        ws = [jnp.exp2(zs[h] - (runs[h] + css[h])) for h in heads]
        if diag:
            ws = [jnp.where(past, w, 0.0) for w in ws]
        new_accs = [accs[h] + jnp.dot(ws[h].astype(BF16), v_ref[pl.ds(start, ATT_T), cols[h]],
                                      preferred_element_type=F32) for h in heads]
        new_runs = [runs[h] + css[h][:, 0:1] for h in heads]
        return tuple(new_accs), tuple(new_runs)

    def smallest(runs):
        m = runs[0]
        for r in runs[1:]:
            m = jnp.minimum(m, r)
        return jnp.min(m)

    accs = tuple(jnp.zeros((ATT_T, HEAD_DIM), F32) for _ in range(ATT_HP))
    runs = tuple(jnp.zeros((ATT_T, 1), F32) for _ in range(ATT_HP))
    accs, runs = block(qi, accs, runs, True)

    def cond(c):
        j, _, _, low = c
        return (j >= 0) & (low < stop_at)

    def body(c):
        j, accs, runs, _ = c
        accs, runs = block(j, accs, runs, False)
        return j - 1, accs, runs, smallest(runs)

    _, accs, _, _ = lax.while_loop(cond, body, (qi - 1, accs, runs, smallest(runs)))
    for h in range(ATT_HP):
        o_ref[:, h * HEAD_DIM:(h + 1) * HEAD_DIM] = accs[h].astype(BF16)


def _attn(proj, bsz, seq, d, q_gain, k_gain):
    t = bsz * seq
    nq = seq // ATT_T
    width = ATT_HP * HEAD_DIM
    col0 = (2 * d) // width
    seg = (N_HEADS * HEAD_DIM) // width
    i = jnp.arange(ATT_T)
    tri = (i[:, None] >= i[None, :]).astype(BF16)
    zmax = 1.02 * math.sqrt(HEAD_DIM) * jnp.max(jnp.abs(q_gain)) * jnp.max(jnp.abs(k_gain))
    stop_at = ((zmax - F32_EXP_ZERO_BELOW) * LOG2_E).reshape(1).astype(F32)
    return pl.pallas_call(
        _attn_kernel,
        grid_spec=pltpu.PrefetchScalarGridSpec(
            num_scalar_prefetch=1,
            grid=(bsz, seg, nq),
            in_specs=[
                pl.BlockSpec((ATT_T, width), lambda b, h, qi, s: (b * nq + qi, col0 + h)),
                pl.BlockSpec((seq, width), lambda b, h, qi, s: (b, col0 + seg + h)),
                pl.BlockSpec((seq, width), lambda b, h, qi, s: (b, col0 + 2 * seg + h)),
                pl.BlockSpec((ATT_T, ATT_T), lambda b, h, qi, s: (0, 0)),
            ],
            out_specs=pl.BlockSpec((ATT_T, width), lambda b, h, qi, s: (b * nq + qi, h)),
        ),
        out_shape=jax.ShapeDtypeStruct((t, N_HEADS * HEAD_DIM), BF16),
        compiler_params=pltpu.CompilerParams(
            dimension_semantics=("arbitrary", "arbitrary", "arbitrary"), vmem_limit_bytes=VMEM_LIMIT),
        name="attn",
    )(stop_at, proj, proj, proj, tri)


def _merge_kernel(x_ref, oa_ref, u_ref, vv_ref, ga_ref, gb_ref, ws_ref, bst_ref, sg_ref,
                  wa_ref, wb_ref, wo_ref, h_ref, fill_ref, vln_ref, ob_ref, zero_ref, fill_sem):
    i = pl.program_id(0)
    z_rows = zero_ref.shape[0]
    n_fill = fill_ref.shape[0] // (pl.num_programs(0) * z_rows)

    @pl.when(i == 0)
    def _():
        zero_ref[...] = jnp.zeros_like(zero_ref)

    def fill_copy(j):
        start = pl.multiple_of((i * n_fill + j) * z_rows, ROW_SUBLANES)
        return pltpu.make_async_copy(zero_ref, fill_ref.at[pl.ds(start, z_rows)], fill_sem)

    for j in range(n_fill):
        fill_copy(j).start()

    vv = vv_ref[...].astype(F32)
    mu = jnp.mean(vv, axis=-1, keepdims=True)
    xc = vv - mu
    var = jnp.mean(xc * xc, axis=-1, keepdims=True)
    vln_ref[...] = (xc * lax.rsqrt(var + EPS) * sg_ref[...]).astype(BF16)

    row = lax.broadcasted_iota(jnp.int32, (SGU_BLOCK, SGU_BLOCK), 0)
    col = lax.broadcasted_iota(jnp.int32, (SGU_BLOCK, SGU_BLOCK), 1)
    mask = (col // CHUNK) <= (row // CHUNK)
    for g in range(SGU_GROUPS):
        wg = jnp.where(mask, ws_ref[g], 0.0).astype(BF16)
        bg = bst_ref[:, g:g + 1]
        cs = slice(g * SGU_BLOCK, (g + 1) * SGU_BLOCK)
        for n in range(MERGE_TM // SGU_BLOCK):
            rs = slice(n * SGU_BLOCK, (n + 1) * SGU_BLOCK)
            sgu = jnp.dot(wg, vln_ref[rs, cs], preferred_element_type=F32) + bg
            ob_ref[rs, cs] = (u_ref[rs, cs].astype(F32) * sgu).astype(BF16)

    ya = jnp.dot(oa_ref[...], wa_ref[...], preferred_element_type=F32)
    yb = jnp.dot(ob_ref[...], wb_ref[...], preferred_element_type=F32)
    y = ga_ref[...].astype(F32) * ya + gb_ref[...].astype(F32) * yb
    h_ref[...] = x_ref[...] + jnp.dot(y.astype(BF16), wo_ref[...], preferred_element_type=F32)

    for j in range(n_fill):
        fill_copy(j).wait()


def _fill_rows_per_copy(rows_per_step):
    for n in range(1, rows_per_step + 1):
        if rows_per_step % n == 0 and (rows_per_step // n) % ROW_SUBLANES == 0 and rows_per_step // n <= FILL_MAX_ROWS:
            return rows_per_step // n
    raise ValueError(f"no aligned split of {rows_per_step} fill rows")


def _merge(x2, oa, proj, w_spatial, b_spatial_t, sgu_gain, wa, wb, wo, n_fill_rows):
    t, d = x2.shape
    aw = oa.shape[1]
    sw = SGU_GROUPS * SGU_BLOCK
    ucol = (2 * d + 3 * aw) // sw
    const = lambda shape: pl.BlockSpec(shape, lambda i: (0,) * len(shape), pipeline_mode=pl.Buffered(1))
    n_steps = t // MERGE_TM
    assert n_fill_rows % n_steps == 0
    z_rows = _fill_rows_per_copy(n_fill_rows // n_steps)
    return pl.pallas_call(
        _merge_kernel,
        grid=(t // MERGE_TM,),
        in_specs=[
            pl.BlockSpec((MERGE_TM, d), lambda i: (i, 0)),
            pl.BlockSpec((MERGE_TM, aw), lambda i: (i, 0)),
            pl.BlockSpec((MERGE_TM, sw), lambda i: (i, ucol)),
            pl.BlockSpec((MERGE_TM, sw), lambda i: (i, ucol + 1)),
            pl.BlockSpec((MERGE_TM, d), lambda i: (i, 0)),
            pl.BlockSpec((MERGE_TM, d), lambda i: (i, 1)),
            const((SGU_GROUPS, SGU_BLOCK, SGU_BLOCK)),
            const((SGU_BLOCK, SGU_GROUPS)),
            const((1, sw)),
            const((aw, d)),
            const((sw, d)),
            const((d, d)),
        ],
        out_specs=[pl.BlockSpec((MERGE_TM, d), lambda i: (i, 0)), pl.BlockSpec(memory_space=pl.ANY)],
        out_shape=[jax.ShapeDtypeStruct((t, d), F32),
                   jax.ShapeDtypeStruct((n_fill_rows, LANES), jnp.uint32)],
        scratch_shapes=[pltpu.VMEM((MERGE_TM, sw), BF16), pltpu.VMEM((MERGE_TM, sw), BF16),
                        pltpu.VMEM((z_rows, LANES), jnp.uint32), pltpu.SemaphoreType.DMA(())],
        compiler_params=pltpu.CompilerParams(
            dimension_semantics=("arbitrary",), vmem_limit_bytes=VMEM_LIMIT),
        name="merge",
    )(x2, oa, proj, proj, proj, proj, w_spatial, b_spatial_t, sgu_gain, wa, wb, wo)


def _router_kernel(h_ref, g2_ref, wrt_ref, brt_ref, ut_ref, xp_ref, idx_ref, rank_ref, gate_ref, cnt_ref,
                   carry_ref):
    @pl.when(pl.program_id(0) == 0)
    def _():
        carry_ref[...] = jnp.zeros_like(carry_ref)

    h = h_ref[...]
    half = h.shape[1] // 2
    ms = jnp.mean(h * h, axis=-1, keepdims=True)
    xn = h * lax.rsqrt(ms + EPS) * g2_ref[...]
    xn_hi = xn.astype(BF16)
    xn_lo = (xn - xn_hi.astype(F32)).astype(BF16)
    hi_bits = lax.bitcast_convert_type(xn_hi.astype(F32), jnp.uint32)
    _store_rows_as_tiles(xp_ref, hi_bits[:, half:] | (hi_bits[:, :half] >> 16))

    wr = wrt_ref[...]
    wr_hi = wr.astype(BF16)
    wr_lo = (wr - wr_hi.astype(F32)).astype(BF16)
    nt = (((1,), (1,)), ((), ()))
    logits = (lax.dot_general(wr_hi, xn_hi, nt, preferred_element_type=F32)
              + lax.dot_general(wr_hi, xn_lo, nt, preferred_element_type=F32)
              + lax.dot_general(wr_lo, xn_hi, nt, preferred_element_type=F32)) + brt_ref[...]
    sub = lax.broadcasted_iota(jnp.int32, logits.shape, 0).astype(F32)
    out_row = lax.broadcasted_iota(jnp.int32, idx_ref.shape, 0)
    work = logits
    vals, sels = [], []
    idx_out = jnp.zeros(idx_ref.shape, jnp.int32)
    for k in range(TOP_K):
        m = jnp.max(work, axis=0, keepdims=True)
        idx = jnp.min(jnp.where(work == m, sub, float(N_EXPERTS)), axis=0, keepdims=True)
        sel = sub == idx
        work = jnp.where(sel, -jnp.inf, work)
        vals.append(m)
        sels.append(sel)
        idx_out = jnp.where(out_row == k, idx.astype(jnp.int32), idx_out)
    idx_ref[...] = idx_out

    exps = [jnp.exp(v - vals[0]) for v in vals]
    denom = exps[0] + exps[1] + exps[2] + exps[3]
    gate_out = jnp.zeros(gate_ref.shape, F32)
    for k in range(TOP_K):
        gate_out = jnp.where(out_row == k, exps[k] / denom, gate_out)
    gate_ref[...] = gate_out

    member = (sels[0] | sels[1] | sels[2] | sels[3])
    before = jnp.dot(member.astype(BF16), ut_ref[...], preferred_element_type=F32)
    pos = carry_ref[...] + before
    rank_out = jnp.zeros(rank_ref.shape, jnp.int32)
    for k in range(TOP_K):
        r = jnp.sum(jnp.where(sels[k], pos, 0.0), axis=0, keepdims=True)
        rank_out = jnp.where(out_row == k, r.astype(jnp.int32), rank_out)
    rank_ref[...] = rank_out

    carry_ref[...] = carry_ref[...] + jnp.sum(member.astype(F32), axis=1, keepdims=True)
    cnt_ref[...] = jnp.broadcast_to(carry_ref[...], cnt_ref.shape)


def _router(h1, g2, w_router_t, b_router_t):
    t, d = h1.shape
    const = lambda shape: pl.BlockSpec(shape, lambda i: (0,) * len(shape))
    i = jnp.arange(ROUTER_TM)
    ut = (i[:, None] < i[None, :]).astype(BF16)
    rows, lanes = 8, 128
    tok = lambda: pl.BlockSpec((rows, ROUTER_TM), lambda i: (0, i))
    return pl.pallas_call(
        _router_kernel,
        grid=(t // ROUTER_TM,),
        in_specs=[
            pl.BlockSpec((ROUTER_TM, d), lambda i: (i, 0)),
            const((1, d)),
            const((N_EXPERTS, d)),
            const((N_EXPERTS, 1)),
            const((ROUTER_TM, ROUTER_TM)),
        ],
        out_specs=[
            pl.BlockSpec((ROUTER_TM * ROW_SUBLANES, LANES), lambda i: (i, 0)),
            tok(), tok(), tok(),
            pl.BlockSpec((N_EXPERTS, lanes), lambda i: (0, 0)),
        ],
        out_shape=[
            jax.ShapeDtypeStruct((t * ROW_SUBLANES, LANES), jnp.uint32),
            jax.ShapeDtypeStruct((rows, t), jnp.int32),
            jax.ShapeDtypeStruct((rows, t), jnp.int32),
            jax.ShapeDtypeStruct((rows, t), F32),
            jax.ShapeDtypeStruct((N_EXPERTS, lanes), F32),
        ],
        scratch_shapes=[pltpu.VMEM((N_EXPERTS, 1), F32)],
        compiler_params=pltpu.CompilerParams(
            dimension_semantics=("arbitrary",), vmem_limit_bytes=VMEM_LIMIT),
        name="router",
    )(h1, g2, w_router_t, b_router_t, ut)


def _dispatch_kernel(dest_ref, xp_ref, xs_in_ref, xs_ref, sem):
    del xs_in_ref
    i = pl.program_id(0)

    n_tok = dest_ref.shape[0] // TOP_K

    def row_copy(s, r):
        return pltpu.make_async_copy(_tile_of_row(xp_ref, s), _tile_of_row(xs_ref, r), sem)

    def issue(b, _):
        for j in range(DMA_UNROLL):
            s = b * DMA_UNROLL + j
            for k in range(TOP_K):
                row_copy(s, dest_ref[k * n_tok + i * DISPATCH_TB + s]).start(priority=k % 2)
        return 0

    def drain(b, _):
        for _ in range(DMA_UNROLL * TOP_K):
            row_copy(0, 0).wait()
        return 0

    lax.fori_loop(0, DISPATCH_TB // DMA_UNROLL, issue, 0)
    lax.fori_loop(0, DISPATCH_TB // DMA_UNROLL, drain, 0)


def _dispatch(dest_flat, xp, xs0):
    t = xp.shape[0] // ROW_SUBLANES
    n_rows = xs0.shape[0] // ROW_SUBLANES
    return pl.pallas_call(
        _dispatch_kernel,
        grid_spec=pltpu.PrefetchScalarGridSpec(
            num_scalar_prefetch=1,
            grid=(t // DISPATCH_TB,),
            in_specs=[pl.BlockSpec((DISPATCH_TB * ROW_SUBLANES, LANES), lambda i, dest: (i, 0)),
                      pl.BlockSpec(memory_space=pl.ANY)],
            out_specs=pl.BlockSpec(memory_space=pl.ANY),
            scratch_shapes=[pltpu.SemaphoreType.DMA(())],
        ),
        out_shape=jax.ShapeDtypeStruct((n_rows * ROW_SUBLANES, LANES), jnp.uint32),
        input_output_aliases={2: 0},
        compiler_params=pltpu.CompilerParams(
            dimension_semantics=("arbitrary",), has_side_effects=True),
        name="dispatch",
    )(dest_flat, xp, xs0)


def _expert_pieces(n_rows):
    k = -(-n_rows // EXP_SUB)
    base, extra = divmod(n_rows // EXP_GRAN, k)
    return [(base + (1 if j < extra else 0)) * EXP_GRAN for j in range(k)]


def _expert_kernel(ie_ref, it_ref, nv_ref, xs_ref, wg_ref, wu_ref, wd_ref, bg_ref, bu_ref, bd_ref,
                   y_ref, acc_ref):
    del ie_ref, it_ref
    i = pl.program_id(0)
    c = pl.program_id(1)
    nc = pl.num_programs(1)
    nv = nv_ref[i]
    half = ROW_SUBLANES * LANES

    @pl.when((i == 0) & (c == 0))
    def _():
        acc_ref[...] = jnp.zeros_like(acc_ref)

    def run(n_rows, last):
        wg = wg_ref[0].astype(BF16)
        wu = wu_ref[0].astype(BF16)
        wd = wd_ref[0].astype(BF16)
        first = c == 0
        lo = 0
        for m in _expert_pieces(n_rows):
            rows = slice(lo, lo + m)
            tiles = lambda s, lo=lo, m=m: pl.ds(lo * ROW_SUBLANES + s, m, stride=ROW_SUBLANES)
            pairs = [_unpack_bf16_pair(xs_ref[tiles(s), :]) for s in range(ROW_SUBLANES)]
            xb = jnp.concatenate([p[0].astype(BF16) for p in pairs] + [p[1].astype(BF16) for p in pairs], axis=1)
            g = jnp.dot(xb, wg, preferred_element_type=F32) + bg_ref[0]
            u = jnp.dot(xb, wu, preferred_element_type=F32) + bu_ref[0]
            g = jnp.minimum(g, SWIGLU_LIMIT)
            u = jnp.clip(u, -SWIGLU_LIMIT, SWIGLU_LIMIT)
            a = g * _sigmoid(SWIGLU_ALPHA * g) * (u + 1.0)
            y = jnp.dot(a.astype(BF16), wd, preferred_element_type=F32)
            total = jnp.where(first, jnp.broadcast_to(bd_ref[0], y.shape), acc_ref[rows, :]) + y
            if last:
                for s in range(ROW_SUBLANES):
                    y_ref[tiles(s), :] = _pack_bf16_pair(total[:, s * LANES:(s + 1) * LANES],
                                                         total[:, half + s * LANES:half + (s + 1) * LANES])
            else:
                acc_ref[rows, :] = total
            lo += m
        if last and n_rows < EXP_TM:
            y_ref[n_rows * ROW_SUBLANES:, :] = jnp.zeros(((EXP_TM - n_rows) * ROW_SUBLANES, LANES), jnp.uint32)

    for n_rows in range(EXP_GRAN, EXP_TM + 1, EXP_GRAN):
        fits = (nv > n_rows - EXP_GRAN) & (nv <= n_rows)
        pl.when(fits & (c < nc - 1))(lambda n_rows=n_rows: run(n_rows, False))
        pl.when(fits & (c == nc - 1))(lambda n_rows=n_rows: run(n_rows, True))

    @pl.when((c == nc - 1) & (nv == 0))
    def _():
        y_ref[...] = jnp.zeros_like(y_ref)


def _expert(item_e, item_t, item_nv, xs, w_up, b_up, w_down, b_down):
    n_rows = xs.shape[0] // ROW_SUBLANES
    ne, d, two_f = w_up.shape
    assert d == 2 * ROW_SUBLANES * LANES
    tile = (EXP_TM * ROW_SUBLANES, LANES)
    f = two_f // 2
    nc = f // EXP_TC
    n_items = item_e.shape[0]
    b_up3 = b_up.reshape(ne, 1, two_f)
    b_down3 = b_down.reshape(ne, 1, d)

    def chunk(c, nv_ref, i):
        return jnp.where(nv_ref[i] > 0, c, nc - 1)

    return pl.pallas_call(
        _expert_kernel,
        grid_spec=pltpu.PrefetchScalarGridSpec(
            num_scalar_prefetch=3,
            grid=(n_items, nc),
            in_specs=[
                pl.BlockSpec(tile, lambda i, c, ie, it, nv: (it[i], 0)),
                pl.BlockSpec((1, d, EXP_TC), lambda i, c, ie, it, nv: (ie[i], 0, chunk(c, nv, i))),
                pl.BlockSpec((1, d, EXP_TC), lambda i, c, ie, it, nv: (ie[i], 0, nc + chunk(c, nv, i))),
                pl.BlockSpec((1, EXP_TC, d), lambda i, c, ie, it, nv: (ie[i], chunk(c, nv, i), 0)),
                pl.BlockSpec((1, 1, EXP_TC), lambda i, c, ie, it, nv: (ie[i], 0, chunk(c, nv, i))),
                pl.BlockSpec((1, 1, EXP_TC), lambda i, c, ie, it, nv: (ie[i], 0, nc + chunk(c, nv, i))),
                pl.BlockSpec((1, 1, d), lambda i, c, ie, it, nv: (ie[i], 0, 0)),
            ],
            out_specs=pl.BlockSpec(tile, lambda i, c, ie, it, nv: (i, 0)),
            scratch_shapes=[pltpu.VMEM((EXP_TM, d), F32)],
        ),
        out_shape=jax.ShapeDtypeStruct((n_rows * ROW_SUBLANES, LANES), jnp.uint32),
        compiler_params=pltpu.CompilerParams(
            dimension_semantics=("arbitrary", "arbitrary"), vmem_limit_bytes=VMEM_LIMIT),
        name="expert",
    )(item_e, item_t, item_nv, xs, w_up, w_up, w_down, b_up3, b_up3, b_down3)


def _combine_kernel(dest_ref, y_ref, h_ref, gate_ref, o_ref, buf_ref, sem):
    i = pl.program_id(0)
    n = pl.num_programs(0)
    n_tok = dest_ref.shape[0] // TOP_K

    def row_copy(r, slot, k, s):
        return pltpu.make_async_copy(_tile_of_row(y_ref, r), _tile_of_row(buf_ref.at[slot, k], s), sem.at[slot])

    def gather(step, slot):
        def issue(b, _):
            for j in range(DMA_UNROLL):
                s = b * DMA_UNROLL + j
                for k in range(TOP_K):
                    row_copy(dest_ref[k * n_tok + step * COMBINE_TB + s], slot, k, s).start(priority=k % 2)
            return 0
        lax.fori_loop(0, COMBINE_TB // DMA_UNROLL, issue, 0)

    @pl.when(i == 0)
    def _():
        gather(0, 0)

    for slot in range(2):
        @pl.when((i + 1 < n) & ((i + 1) % 2 == slot))
        def _():
            gather(i + 1, slot)

    for slot in range(2):
        @pl.when(i % 2 == slot)
        def _():
            def drain(b, _):
                for _ in range(DMA_UNROLL * TOP_K):
                    row_copy(0, slot, 0, 0).wait()
                return 0
            lax.fori_loop(0, COMBINE_TB // DMA_UNROLL, drain, 0)
            half = ROW_SUBLANES * LANES
            gates = [jnp.broadcast_to(gate_ref[:, k:k + 1], (COMBINE_TB, LANES)) for k in range(TOP_K)]
            for s in range(ROW_SUBLANES):
                lo_cols = slice(s * LANES, (s + 1) * LANES)
                hi_cols = slice(half + s * LANES, half + (s + 1) * LANES)
                acc_lo = h_ref[:, lo_cols]
                acc_hi = h_ref[:, hi_cols]
                for k in range(TOP_K):
                    lo, hi = _unpack_bf16_pair(_load_tile_rows(buf_ref.at[slot, k], s, COMBINE_TB))
                    acc_lo = acc_lo + gates[k] * lo
                    acc_hi = acc_hi + gates[k] * hi
                o_ref[:, lo_cols] = acc_lo
                o_ref[:, hi_cols] = acc_hi


def _combine(dest_flat, y, h1, gate):
    t, d = h1.shape
    lanes = gate.shape[1]
    return pl.pallas_call(
        _combine_kernel,
        grid_spec=pltpu.PrefetchScalarGridSpec(
            num_scalar_prefetch=1,
            grid=(t // COMBINE_TB,),
            in_specs=[
                pl.BlockSpec(memory_space=pl.ANY),
                pl.BlockSpec((COMBINE_TB, d), lambda i, dest: (i, 0)),
                pl.BlockSpec((COMBINE_TB, lanes), lambda i, dest: (i, 0)),
            ],
            out_specs=pl.BlockSpec((COMBINE_TB, d), lambda i, dest: (i, 0)),
            scratch_shapes=[pltpu.VMEM((2, TOP_K, COMBINE_TB * ROW_SUBLANES, LANES), jnp.uint32),
                            pltpu.SemaphoreType.DMA((2,))],
        ),
        out_shape=jax.ShapeDtypeStruct((t, d), F32),
        compiler_params=pltpu.CompilerParams(
            dimension_semantics=("arbitrary",), vmem_limit_bytes=VMEM_LIMIT),
        name="combine",
    )(dest_flat, y, h1, gate)


def _plan_items(counts, n_items_max):
    tiles = (counts + EXP_TM - 1) // EXP_TM
    tile_end = jnp.cumsum(tiles)
    tile_start = tile_end - tiles
    n_items = tile_end[-1]
    g = jnp.arange(n_items_max, dtype=jnp.int32)
    live = g < n_items
    gg = jnp.maximum(jnp.minimum(g, n_items - 1), 0)
    e = jnp.minimum(jnp.searchsorted(tile_end, gg, side="right"), N_EXPERTS - 1).astype(jnp.int32)
    nv = jnp.clip(counts[e] - (gg - tile_start[e]) * EXP_TM, 0, EXP_TM)
    nv = jnp.where(live, nv, 0).astype(jnp.int32)
    return tile_start * EXP_TM, e, gg.astype(jnp.int32), nv


def _layer(x2, bsz, seq, norm1_gain, w_in, q_norm_gain, k_norm_gain, sgu_norm_gain, w_spatial, b_spatial,
           w_branch_a, w_branch_b, w_out, norm2_gain, w_router, b_router, w_up, b_up, w_down, b_down):
    t, d = x2.shape
    proj = _proj(x2, norm1_gain[None, :], w_in, q_norm_gain[None, :], k_norm_gain[None, :])
    oa = _attn(proj, bsz, seq, d, q_norm_gain, k_norm_gain)
    n_items_max = -(-(t * TOP_K) // EXP_TM) + N_EXPERTS
    h1, xs0 = _merge(x2, oa, proj, w_spatial, b_spatial.T, sgu_norm_gain[None, :],
                     w_branch_a.astype(BF16), w_branch_b.astype(BF16), w_out.astype(BF16),
                     n_items_max * EXP_TM * ROW_SUBLANES)
    xp, idx, rank, gate, cnt = _router(h1, norm2_gain[None, :], w_router.T, b_router[:, None])

    counts = cnt[:, 0].astype(jnp.int32)
    seg_start, item_e, item_t, item_nv = _plan_items(counts, n_items_max)
    onehot = idx[:TOP_K, :, None] == jnp.arange(N_EXPERTS, dtype=jnp.int32)
    dest = (jnp.sum(jnp.where(onehot, seg_start.astype(jnp.int32), 0), axis=-1)
            + rank[:TOP_K]).reshape(-1)

    xs = _dispatch(dest, xp, xs0)
    y = _expert(item_e, item_t, item_nv, xs, w_up, b_up, w_down, b_down)
    return _combine(dest, y, h1, gate[:TOP_K].T)


def kernel(x, norm1_gain, w_in, q_norm_gain, k_norm_gain, sgu_norm_gain, w_spatial, b_spatial, w_branch_a,
           w_branch_b, w_out, norm2_gain, w_router, b_router, w_up, b_up, w_down, b_down):
    bsz, seq, d = x.shape
    h = x.reshape(bsz * seq, d)
    for l in range(norm1_gain.shape[0]):
        h = _layer(h, bsz, seq, norm1_gain[l], w_in[l], q_norm_gain[l], k_norm_gain[l], sgu_norm_gain[l],
                   w_spatial[l], b_spatial[l], w_branch_a[l], w_branch_b[l], w_out[l], norm2_gain[l],
                   w_router[l], b_router[l], w_up[l], b_up[l], w_down[l], b_down[l])
    return h.reshape(bsz, seq, d)
```

```python
import math

import jax
import jax.numpy as jnp
from jax import lax
from jax.experimental import pallas as pl
from jax.experimental.pallas import tpu as pltpu

F32 = jnp.float32
BF16 = jnp.bfloat16

EPS = 1e-6
N_HEADS = 8
HEAD_DIM = 128
SGU_GROUPS = 8
SGU_BLOCK = 128
CHUNK = 64
N_EXPERTS = 32
TOP_K = 4
SWIGLU_LIMIT = 7.0
SWIGLU_ALPHA = 1.702

F32_EXP_ZERO_BELOW = -105.0
LOG2_E = 1.4426950408889634

VMEM_LIMIT = 56 * 1024 * 1024
PROJ_TM = 1024
PROJ_TN = 1024
PROJ_ROWS = 512
ATT_T = 256
ATT_HP = 8
MERGE_TM = 256
ROUTER_TM = 1024
EXP_TM = 1280
EXP_SUB = 1280
EXP_GRAN = 128
EXP_TC = 256
DISPATCH_TB = 512
COMBINE_TB = 128
DMA_UNROLL = 8
FILL_MAX_ROWS = 4096


def _sigmoid(x):
    return 1.0 / (1.0 + jnp.exp(-x))


ROW_SUBLANES = 8
LANES = 128


def _store_rows_as_tiles(ref, rows2d):
    n = rows2d.shape[0]
    for s in range(ROW_SUBLANES):
        ref[pl.ds(s, n, stride=ROW_SUBLANES), :] = rows2d[:, s * LANES:(s + 1) * LANES]


def _tile_of_row(ref, r):
    start = r * ROW_SUBLANES
    if not isinstance(r, int):
        start = pl.multiple_of(start, ROW_SUBLANES)
    return ref.at[pl.ds(start, ROW_SUBLANES)]


def _load_tile_rows(ref, s, n):
    return ref[pl.ds(s, n, stride=ROW_SUBLANES), :]


def _pack_bf16_pair(lo, hi):
    lo_bits = lax.bitcast_convert_type(lo.astype(BF16).astype(F32), jnp.uint32)
    hi_bits = lax.bitcast_convert_type(hi.astype(BF16).astype(F32), jnp.uint32)
    return hi_bits | (lo_bits >> 16)


def _unpack_bf16_pair(w):
    lo = lax.bitcast_convert_type(w << 16, F32)
    hi = lax.bitcast_convert_type(w & jnp.uint32(0xFFFF0000), F32)
    return lo, hi


def _gelu_tanh(x):
    c = math.sqrt(2.0 / math.pi)
    return 0.5 * x * (1.0 + jnp.tanh(c * (x + 0.044715 * (x * x * x))))


def _proj_kernel(x_ref, g1_ref, w_ref, qg_ref, kg_ref, o_ref, xn_ref):
    j = pl.program_id(1)
    n_q = (N_HEADS * HEAD_DIM) // PROJ_TN

    @pl.when(j == 0)
    def _():
        x = x_ref[...]
        ms = jnp.mean(x * x, axis=-1, keepdims=True)
        xn_ref[...] = (x * lax.rsqrt(ms + EPS) * g1_ref[...]).astype(BF16)

    def head_norm(a):
        gain = jnp.where(j < n_q, qg_ref[...], kg_ref[...])
        outs = []
        for h in range(PROJ_TN // HEAD_DIM):
            ah = a[:, h * HEAD_DIM:(h + 1) * HEAD_DIM]
            ms = jnp.mean(ah * ah, axis=-1, keepdims=True)
            outs.append(ah * lax.rsqrt(ms + EPS) * gain)
        return jnp.concatenate(outs, axis=1)

    def tile(epilogue):
        w = w_ref[...].astype(BF16)
        for r in range(PROJ_TM // PROJ_ROWS):
            rows = slice(r * PROJ_ROWS, (r + 1) * PROJ_ROWS)
            acc = jnp.dot(xn_ref[rows, :], w, preferred_element_type=F32)
            o_ref[rows, :] = epilogue(acc).astype(BF16)

    pl.when(j < 2 * n_q)(lambda: tile(head_norm))
    pl.when((j >= 2 * n_q) & (j < 3 * n_q))(lambda: tile(lambda a: a))
    pl.when((j >= 3 * n_q) & (j < 5 * n_q))(lambda: tile(_gelu_tanh))
    pl.when(j >= 5 * n_q)(lambda: tile(_sigmoid))


def _proj(x2, g1, w_in, qg, kg):
    t, d = x2.shape
    n = w_in.shape[1]
    nj = n // PROJ_TN
    shift = (2 * d) // PROJ_TN
    return pl.pallas_call(
        _proj_kernel,
        grid=(t // PROJ_TM, nj),
        in_specs=[
            pl.BlockSpec((PROJ_TM, d), lambda i, j: (i, 0)),
            pl.BlockSpec((1, d), lambda i, j: (0, 0)),
            pl.BlockSpec((d, PROJ_TN), lambda i, j: (0, j)),
            pl.BlockSpec((1, HEAD_DIM), lambda i, j: (0, 0)),
            pl.BlockSpec((1, HEAD_DIM), lambda i, j: (0, 0)),
        ],
        out_specs=pl.BlockSpec((PROJ_TM, PROJ_TN), lambda i, j: (i, (j + shift) % nj)),
        out_shape=jax.ShapeDtypeStruct((t, n), BF16),
        scratch_shapes=[pltpu.VMEM((PROJ_TM, d), BF16)],
        compiler_params=pltpu.CompilerParams(
            dimension_semantics=("arbitrary", "arbitrary"), vmem_limit_bytes=VMEM_LIMIT),
        name="proj",
    )(x2, g1, w_in, qg, kg)


def _attn_kernel(stop_ref, q_ref, k_ref, v_ref, u_ref, o_ref):
    qi = pl.program_id(2)
    scale = HEAD_DIM ** -0.5 * LOG2_E
    stop_at = stop_ref[0]

    def block(j, accs, runs, diag):
        start = pl.multiple_of(j * ATT_T, ATT_T)
        tri = u_ref[...]
        if diag:
            row = lax.broadcasted_iota(jnp.int32, (ATT_T, ATT_T), 0)
            col = lax.broadcasted_iota(jnp.int32, (ATT_T, ATT_T), 1)
            past = col < row
        heads = range(ATT_HP)
        cols = [slice(h * HEAD_DIM, (h + 1) * HEAD_DIM) for h in heads]
        zs = [lax.dot_general(q_ref[:, cols[h]], k_ref[pl.ds(start, ATT_T), cols[h]],
                              (((1,), (1,)), ((), ())), preferred_element_type=F32) * scale for h in heads]
        sps = [jnp.maximum(z, 0.0) + jnp.log(1.0 + jnp.exp2(-jnp.abs(z))) * LOG2_E for z in zs]
        if diag:
            sps = [jnp.where(past, sp, 0.0) for sp in sps]
        his = [sp.astype(BF16) for sp in sps]
        los = [(sp - hi.astype(F32)).astype(BF16) for sp, hi in zip(sps, his)]
        css = [jnp.dot(hi, tri, preferred_element_type=F32) + jnp.dot(lo, tri, preferred_element_type=F32)
               for hi, lo in zip(his, los)]
        ws = [jnp.exp2(zs[h] - (runs[h] + css[h])) for h in heads]
        if diag:
            ws = [jnp.where(past, w, 0.0) for w in ws]
        new_accs = [accs[h] + jnp.dot(ws[h].astype(BF16), v_ref[pl.ds(start, ATT_T), cols[h]],
                                      preferred_element_type=F32) for h in heads]
        new_runs = [runs[h] + css[h][:, 0:1] for h in heads]
        return tuple(new_accs), tuple(new_runs)

    def smallest(runs):
        m = runs[0]
        for r in runs[1:]:
            m = jnp.minimum(m, r)
        return jnp.min(m)

    accs = tuple(jnp.zeros((ATT_T, HEAD_DIM), F32) for _ in range(ATT_HP))
    runs = tuple(jnp.zeros((ATT_T, 1), F32) for _ in range(ATT_HP))
    accs, runs = block(qi, accs, runs, True)

    def cond(c):
        j, _, _, low = c
        return (j >= 0) & (low < stop_at)

    def body(c):
        j, accs, runs, _ = c
        accs, runs = block(j, accs, runs, False)
        return j - 1, accs, runs, smallest(runs)

    _, accs, _, _ = lax.while_loop(cond, body, (qi - 1, accs, runs, smallest(runs)))
    for h in range(ATT_HP):
        o_ref[:, h * HEAD_DIM:(h + 1) * HEAD_DIM] = accs[h].astype(BF16)


def _attn(proj, bsz, seq, d, q_gain, k_gain):
    t = bsz * seq
    nq = seq // ATT_T
    width = ATT_HP * HEAD_DIM
    col0 = (2 * d) // width
    seg = (N_HEADS * HEAD_DIM) // width
    i = jnp.arange(ATT_T)
    tri = (i[:, None] >= i[None, :]).astype(BF16)
    zmax = 1.02 * math.sqrt(HEAD_DIM) * jnp.max(jnp.abs(q_gain)) * jnp.max(jnp.abs(k_gain))
    stop_at = ((zmax - F32_EXP_ZERO_BELOW) * LOG2_E).reshape(1).astype(F32)
    return pl.pallas_call(
        _attn_kernel,
        grid_spec=pltpu.PrefetchScalarGridSpec(
            num_scalar_prefetch=1,
            grid=(bsz, seg, nq),
            in_specs=[
                pl.BlockSpec((ATT_T, width), lambda b, h, qi, s: (b * nq + qi, col0 + h)),
                pl.BlockSpec((seq, width), lambda b, h, qi, s: (b, col0 + seg + h)),
                pl.BlockSpec((seq, width), lambda b, h, qi, s: (b, col0 + 2 * seg + h)),
                pl.BlockSpec((ATT_T, ATT_T), lambda b, h, qi, s: (0, 0)),
            ],
            out_specs=pl.BlockSpec((ATT_T, width), lambda b, h, qi, s: (b * nq + qi, h)),
        ),
        out_shape=jax.ShapeDtypeStruct((t, N_HEADS * HEAD_DIM), BF16),
        compiler_params=pltpu.CompilerParams(
            dimension_semantics=("arbitrary", "arbitrary", "arbitrary"), vmem_limit_bytes=VMEM_LIMIT),
        name="attn",
    )(stop_at, proj, proj, proj, tri)


def _merge_kernel(x_ref, oa_ref, u_ref, vv_ref, ga_ref, gb_ref, ws_ref, bst_ref, sg_ref,
                  wa_ref, wb_ref, wo_ref, h_ref, fill_ref, vln_ref, ob_ref, zero_ref, fill_sem):
    i = pl.program_id(0)
    z_rows = zero_ref.shape[0]
    n_fill = fill_ref.shape[0] // (pl.num_programs(0) * z_rows)

    @pl.when(i == 0)
    def _():
        zero_ref[...] = jnp.zeros_like(zero_ref)

    def fill_copy(j):
        start = pl.multiple_of((i * n_fill + j) * z_rows, ROW_SUBLANES)
        return pltpu.make_async_copy(zero_ref, fill_ref.at[pl.ds(start, z_rows)], fill_sem)

    for j in range(n_fill):
        fill_copy(j).start()

    vv = vv_ref[...].astype(F32)
    mu = jnp.mean(vv, axis=-1, keepdims=True)
    xc = vv - mu
    var = jnp.mean(xc * xc, axis=-1, keepdims=True)
    vln_ref[...] = (xc * lax.rsqrt(var + EPS) * sg_ref[...]).astype(BF16)

    row = lax.broadcasted_iota(jnp.int32, (SGU_BLOCK, SGU_BLOCK), 0)
    col = lax.broadcasted_iota(jnp.int32, (SGU_BLOCK, SGU_BLOCK), 1)
    mask = (col // CHUNK) <= (row // CHUNK)
    for g in range(SGU_GROUPS):
        wg = jnp.where(mask, ws_ref[g], 0.0).astype(BF16)
        bg = bst_ref[:, g:g + 1]
        cs = slice(g * SGU_BLOCK, (g + 1) * SGU_BLOCK)
        for n in range(MERGE_TM // SGU_BLOCK):
            rs = slice(n * SGU_BLOCK, (n + 1) * SGU_BLOCK)
            sgu = jnp.dot(wg, vln_ref[rs, cs], preferred_element_type=F32) + bg
            ob_ref[rs, cs] = (u_ref[rs, cs].astype(F32) * sgu).astype(BF16)

    ya = jnp.dot(oa_ref[...], wa_ref[...], preferred_element_type=F32)
    yb = jnp.dot(ob_ref[...], wb_ref[...], preferred_element_type=F32)
    y = ga_ref[...].astype(F32) * ya + gb_ref[...].astype(F32) * yb
    h_ref[...] = x_ref[...] + jnp.dot(y.astype(BF16), wo_ref[...], preferred_element_type=F32)

    for j in range(n_fill):
        fill_copy(j).wait()


def _fill_rows_per_copy(rows_per_step):
    for n in range(1, rows_per_step + 1):
        if rows_per_step % n == 0 and (rows_per_step // n) % ROW_SUBLANES == 0 and rows_per_step // n <= FILL_MAX_ROWS:
            return rows_per_step // n
    raise ValueError(f"no aligned split of {rows_per_step} fill rows")


def _merge(x2, oa, proj, w_spatial, b_spatial_t, sgu_gain, wa, wb, wo, n_fill_rows):
    t, d = x2.shape
    aw = oa.shape[1]
    sw = SGU_GROUPS * SGU_BLOCK
    ucol = (2 * d + 3 * aw) // sw
    const = lambda shape: pl.BlockSpec(shape, lambda i: (0,) * len(shape), pipeline_mode=pl.Buffered(1))
    n_steps = t // MERGE_TM
    assert n_fill_rows % n_steps == 0
    z_rows = _fill_rows_per_copy(n_fill_rows // n_steps)
    return pl.pallas_call(
        _merge_kernel,
        grid=(t // MERGE_TM,),
        in_specs=[
            pl.BlockSpec((MERGE_TM, d), lambda i: (i, 0)),
            pl.BlockSpec((MERGE_TM, aw), lambda i: (i, 0)),
            pl.BlockSpec((MERGE_TM, sw), lambda i: (i, ucol)),
            pl.BlockSpec((MERGE_TM, sw), lambda i: (i, ucol + 1)),
            pl.BlockSpec((MERGE_TM, d), lambda i: (i, 0)),
            pl.BlockSpec((MERGE_TM, d), lambda i: (i, 1)),
            const((SGU_GROUPS, SGU_BLOCK, SGU_BLOCK)),
            const((SGU_BLOCK, SGU_GROUPS)),
            const((1, sw)),
            const((aw, d)),
            const((sw, d)),
            const((d, d)),
        ],
        out_specs=[pl.BlockSpec((MERGE_TM, d), lambda i: (i, 0)), pl.BlockSpec(memory_space=pl.ANY)],
        out_shape=[jax.ShapeDtypeStruct((t, d), F32),
                   jax.ShapeDtypeStruct((n_fill_rows, LANES), jnp.uint32)],
        scratch_shapes=[pltpu.VMEM((MERGE_TM, sw), BF16), pltpu.VMEM((MERGE_TM, sw), BF16),
                        pltpu.VMEM((z_rows, LANES), jnp.uint32), pltpu.SemaphoreType.DMA(())],
        compiler_params=pltpu.CompilerParams(
            dimension_semantics=("arbitrary",), vmem_limit_bytes=VMEM_LIMIT),
        name="merge",
    )(x2, oa, proj, proj, proj, proj, w_spatial, b_spatial_t, sgu_gain, wa, wb, wo)


def _router_kernel(h_ref, g2_ref, wrt_ref, brt_ref, ut_ref, xp_ref, idx_ref, rank_ref, gate_ref, cnt_ref,
                   carry_ref):
    @pl.when(pl.program_id(0) == 0)
    def _():
        carry_ref[...] = jnp.zeros_like(carry_ref)

    h = h_ref[...]
    half = h.shape[1] // 2
    ms = jnp.mean(h * h, axis=-1, keepdims=True)
    xn = h * lax.rsqrt(ms + EPS) * g2_ref[...]
    xn_hi = xn.astype(BF16)
    xn_lo = (xn - xn_hi.astype(F32)).astype(BF16)
    hi_bits = lax.bitcast_convert_type(xn_hi.astype(F32), jnp.uint32)
    _store_rows_as_tiles(xp_ref, hi_bits[:, half:] | (hi_bits[:, :half] >> 16))

    wr = wrt_ref[...]
    wr_hi = wr.astype(BF16)
    wr_lo = (wr - wr_hi.astype(F32)).astype(BF16)
    nt = (((1,), (1,)), ((), ()))
    logits = (lax.dot_general(wr_hi, xn_hi, nt, preferred_element_type=F32)
              + lax.dot_general(wr_hi, xn_lo, nt, preferred_element_type=F32)
              + lax.dot_general(wr_lo, xn_hi, nt, preferred_element_type=F32)) + brt_ref[...]
    sub = lax.broadcasted_iota(jnp.int32, logits.shape, 0).astype(F32)
    out_row = lax.broadcasted_iota(jnp.int32, idx_ref.shape, 0)
    work = logits
    vals, sels = [], []
    idx_out = jnp.zeros(idx_ref.shape, jnp.int32)
    for k in range(TOP_K):
        m = jnp.max(work, axis=0, keepdims=True)
        idx = jnp.min(jnp.where(work == m, sub, float(N_EXPERTS)), axis=0, keepdims=True)
        sel = sub == idx
        work = jnp.where(sel, -jnp.inf, work)
        vals.append(m)
        sels.append(sel)
        idx_out = jnp.where(out_row == k, idx.astype(jnp.int32), idx_out)
    idx_ref[...] = idx_out

    exps = [jnp.exp(v - vals[0]) for v in vals]
    denom = exps[0] + exps[1] + exps[2] + exps[3]
    gate_out = jnp.zeros(gate_ref.shape, F32)
    for k in range(TOP_K):
        gate_out = jnp.where(out_row == k, exps[k] / denom, gate_out)
    gate_ref[...] = gate_out

    member = (sels[0] | sels[1] | sels[2] | sels[3])
    before = jnp.dot(member.astype(BF16), ut_ref[...], preferred_element_type=F32)
    pos = carry_ref[...] + before
    rank_out = jnp.zeros(rank_ref.shape, jnp.int32)
    for k in range(TOP_K):
        r = jnp.sum(jnp.where(sels[k], pos, 0.0), axis=0, keepdims=True)
        rank_out = jnp.where(out_row == k, r.astype(jnp.int32), rank_out)
    rank_ref[...] = rank_out

    carry_ref[...] = carry_ref[...] + jnp.sum(member.astype(F32), axis=1, keepdims=True)
    cnt_ref[...] = jnp.broadcast_to(carry_ref[...], cnt_ref.shape)


def _router(h1, g2, w_router_t, b_router_t):
    t, d = h1.shape
    const = lambda shape: pl.BlockSpec(shape, lambda i: (0,) * len(shape))
    i = jnp.arange(ROUTER_TM)
    ut = (i[:, None] < i[None, :]).astype(BF16)
    rows, lanes = 8, 128
    tok = lambda: pl.BlockSpec((rows, ROUTER_TM), lambda i: (0, i))
    return pl.pallas_call(
        _router_kernel,
        grid=(t // ROUTER_TM,),
        in_specs=[
            pl.BlockSpec((ROUTER_TM, d), lambda i: (i, 0)),
            const((1, d)),
            const((N_EXPERTS, d)),
            const((N_EXPERTS, 1)),
            const((ROUTER_TM, ROUTER_TM)),
        ],
        out_specs=[
            pl.BlockSpec((ROUTER_TM * ROW_SUBLANES, LANES), lambda i: (i, 0)),
            tok(), tok(), tok(),
            pl.BlockSpec((N_EXPERTS, lanes), lambda i: (0, 0)),
        ],
        out_shape=[
            jax.ShapeDtypeStruct((t * ROW_SUBLANES, LANES), jnp.uint32),
            jax.ShapeDtypeStruct((rows, t), jnp.int32),
            jax.ShapeDtypeStruct((rows, t), jnp.int32),
            jax.ShapeDtypeStruct((rows, t), F32),
            jax.ShapeDtypeStruct((N_EXPERTS, lanes), F32),
        ],
        scratch_shapes=[pltpu.VMEM((N_EXPERTS, 1), F32)],
        compiler_params=pltpu.CompilerParams(
            dimension_semantics=("arbitrary",), vmem_limit_bytes=VMEM_LIMIT),
        name="router",
    )(h1, g2, w_router_t, b_router_t, ut)


def _dispatch_kernel(dest_ref, xp_ref, xs_in_ref, xs_ref, sem):
    del xs_in_ref
    i = pl.program_id(0)

    n_tok = dest_ref.shape[0] // TOP_K

    def row_copy(s, r):
        return pltpu.make_async_copy(_tile_of_row(xp_ref, s), _tile_of_row(xs_ref, r), sem)

    def issue(b, _):
        for j in range(DMA_UNROLL):
            s = b * DMA_UNROLL + j
            for k in range(TOP_K):
                row_copy(s, dest_ref[k * n_tok + i * DISPATCH_TB + s]).start(priority=k % 2)
        return 0

    def drain(b, _):
        for _ in range(DMA_UNROLL * TOP_K):
            row_copy(0, 0).wait()
        return 0

    lax.fori_loop(0, DISPATCH_TB // DMA_UNROLL, issue, 0)
    lax.fori_loop(0, DISPATCH_TB // DMA_UNROLL, drain, 0)


def _dispatch(dest_flat, xp, xs0):
    t = xp.shape[0] // ROW_SUBLANES
    n_rows = xs0.shape[0] // ROW_SUBLANES
    return pl.pallas_call(
        _dispatch_kernel,
        grid_spec=pltpu.PrefetchScalarGridSpec(
            num_scalar_prefetch=1,
            grid=(t // DISPATCH_TB,),
            in_specs=[pl.BlockSpec((DISPATCH_TB * ROW_SUBLANES, LANES), lambda i, dest: (i, 0)),
                      pl.BlockSpec(memory_space=pl.ANY)],
            out_specs=pl.BlockSpec(memory_space=pl.ANY),
            scratch_shapes=[pltpu.SemaphoreType.DMA(())],
        ),
        out_shape=jax.ShapeDtypeStruct((n_rows * ROW_SUBLANES, LANES), jnp.uint32),
        input_output_aliases={2: 0},
        compiler_params=pltpu.CompilerParams(
            dimension_semantics=("arbitrary",), has_side_effects=True),
        name="dispatch",
    )(dest_flat, xp, xs0)


def _expert_pieces(n_rows):
    k = -(-n_rows // EXP_SUB)
    base, extra = divmod(n_rows // EXP_GRAN, k)
    return [(base + (1 if j < extra else 0)) * EXP_GRAN for j in range(k)]


def _expert_kernel(ie_ref, it_ref, nv_ref, xs_ref, wg_ref, wu_ref, wd_ref, bg_ref, bu_ref, bd_ref,
                   y_ref, acc_ref):
    del ie_ref, it_ref
    i = pl.program_id(0)
    c = pl.program_id(1)
    nc = pl.num_programs(1)
    nv = nv_ref[i]
    half = ROW_SUBLANES * LANES

    @pl.when((i == 0) & (c == 0))
    def _():
        acc_ref[...] = jnp.zeros_like(acc_ref)

    def run(n_rows):
        wg = wg_ref[0].astype(BF16)
        wu = wu_ref[0].astype(BF16)
        wd = wd_ref[0].astype(BF16)
        first = c == 0
        lo = 0
        for m in _expert_pieces(n_rows):
            rows = slice(lo, lo + m)
            tiles = lambda s, lo=lo, m=m: pl.ds(lo * ROW_SUBLANES + s, m, stride=ROW_SUBLANES)
            pairs = [_unpack_bf16_pair(xs_ref[tiles(s), :]) for s in range(ROW_SUBLANES)]
            xb = jnp.concatenate([p[0].astype(BF16) for p in pairs] + [p[1].astype(BF16) for p in pairs], axis=1)
            g = jnp.dot(xb, wg, preferred_element_type=F32) + bg_ref[0]
            u = jnp.dot(xb, wu, preferred_element_type=F32) + bu_ref[0]
            g = jnp.minimum(g, SWIGLU_LIMIT)
            u = jnp.clip(u, -SWIGLU_LIMIT, SWIGLU_LIMIT)
            a = g * _sigmoid(SWIGLU_ALPHA * g) * (u + 1.0)
            y = jnp.dot(a.astype(BF16), wd, preferred_element_type=F32)
            acc_ref[rows, :] = jnp.where(first, jnp.broadcast_to(bd_ref[0], y.shape), acc_ref[rows, :]) + y
            lo += m

    for n_rows in range(EXP_GRAN, EXP_TM + 1, EXP_GRAN):
        pl.when((nv > n_rows - EXP_GRAN) & (nv <= n_rows))(lambda n_rows=n_rows: run(n_rows))

    @pl.when((c == nc - 1) & (nv == 0))
    def _():
        y_ref[...] = jnp.zeros_like(y_ref)

    @pl.when((c == nc - 1) & (nv > 0))
    def _():
        for s in range(ROW_SUBLANES):
            lo = acc_ref[:, s * LANES:(s + 1) * LANES]
            hi = acc_ref[:, half + s * LANES:half + (s + 1) * LANES]
            y_ref[pl.ds(s, EXP_TM, stride=ROW_SUBLANES), :] = _pack_bf16_pair(lo, hi)


def _expert(item_e, item_t, item_nv, xs, w_up, b_up, w_down, b_down):
    n_rows = xs.shape[0] // ROW_SUBLANES
    ne, d, two_f = w_up.shape
    assert d == 2 * ROW_SUBLANES * LANES
    tile = (EXP_TM * ROW_SUBLANES, LANES)
    f = two_f // 2
    nc = f // EXP_TC
    n_items = item_e.shape[0]
    b_up3 = b_up.reshape(ne, 1, two_f)
    b_down3 = b_down.reshape(ne, 1, d)

    def chunk(c, nv_ref, i):
        return jnp.where(nv_ref[i] > 0, c, nc - 1)

    return pl.pallas_call(
        _expert_kernel,
        grid_spec=pltpu.PrefetchScalarGridSpec(
            num_scalar_prefetch=3,
            grid=(n_items, nc),
            in_specs=[
                pl.BlockSpec(tile, lambda i, c, ie, it, nv: (it[i], 0)),
                pl.BlockSpec((1, d, EXP_TC), lambda i, c, ie, it, nv: (ie[i], 0, chunk(c, nv, i))),
                pl.BlockSpec((1, d, EXP_TC), lambda i, c, ie, it, nv: (ie[i], 0, nc + chunk(c, nv, i))),
                pl.BlockSpec((1, EXP_TC, d), lambda i, c, ie, it, nv: (ie[i], chunk(c, nv, i), 0)),
                pl.BlockSpec((1, 1, EXP_TC), lambda i, c, ie, it, nv: (ie[i], 0, chunk(c, nv, i))),
                pl.BlockSpec((1, 1, EXP_TC), lambda i, c, ie, it, nv: (ie[i], 0, nc + chunk(c, nv, i))),
                pl.BlockSpec((1, 1, d), lambda i, c, ie, it, nv: (ie[i], 0, 0)),
            ],
            out_specs=pl.BlockSpec(tile, lambda i, c, ie, it, nv: (i, 0)),
            scratch_shapes=[pltpu.VMEM((EXP_TM, d), F32)],
        ),
        out_shape=jax.ShapeDtypeStruct((n_rows * ROW_SUBLANES, LANES), jnp.uint32),
        compiler_params=pltpu.CompilerParams(
            dimension_semantics=("arbitrary", "arbitrary"), vmem_limit_bytes=VMEM_LIMIT),
        name="expert",
    )(item_e, item_t, item_nv, xs, w_up, w_up, w_down, b_up3, b_up3, b_down3)


def _combine_kernel(dest_ref, y_ref, h_ref, gate_ref, o_ref, buf_ref, sem):
    i = pl.program_id(0)
    n = pl.num_programs(0)
    n_tok = dest_ref.shape[0] // TOP_K

    def row_copy(r, slot, k, s):
        return pltpu.make_async_copy(_tile_of_row(y_ref, r), _tile_of_row(buf_ref.at[slot, k], s), sem.at[slot])

    def gather(step, slot):
        def issue(b, _):
            for j in range(DMA_UNROLL):
                s = b * DMA_UNROLL + j
                for k in range(TOP_K):
                    row_copy(dest_ref[k * n_tok + step * COMBINE_TB + s], slot, k, s).start(priority=k % 2)
            return 0
        lax.fori_loop(0, COMBINE_TB // DMA_UNROLL, issue, 0)

    @pl.when(i == 0)
    def _():
        gather(0, 0)

    for slot in range(2):
        @pl.when((i + 1 < n) & ((i + 1) % 2 == slot))
        def _():
            gather(i + 1, slot)

    for slot in range(2):
        @pl.when(i % 2 == slot)
        def _():
            def drain(b, _):
                for _ in range(DMA_UNROLL * TOP_K):
                    row_copy(0, slot, 0, 0).wait()
                return 0
            lax.fori_loop(0, COMBINE_TB // DMA_UNROLL, drain, 0)
            half = ROW_SUBLANES * LANES
            gates = [jnp.broadcast_to(gate_ref[:, k:k + 1], (COMBINE_TB, LANES)) for k in range(TOP_K)]
            for s in range(ROW_SUBLANES):
                lo_cols = slice(s * LANES, (s + 1) * LANES)
                hi_cols = slice(half + s * LANES, half + (s + 1) * LANES)
                acc_lo = h_ref[:, lo_cols]
                acc_hi = h_ref[:, hi_cols]
                for k in range(TOP_K):
                    lo, hi = _unpack_bf16_pair(_load_tile_rows(buf_ref.at[slot, k], s, COMBINE_TB))
                    acc_lo = acc_lo + gates[k] * lo
                    acc_hi = acc_hi + gates[k] * hi
                o_ref[:, lo_cols] = acc_lo
                o_ref[:, hi_cols] = acc_hi


def _combine(dest_flat, y, h1, gate):
    t, d = h1.shape
    lanes = gate.shape[1]
    return pl.pallas_call(
        _combine_kernel,
        grid_spec=pltpu.PrefetchScalarGridSpec(
            num_scalar_prefetch=1,
            grid=(t // COMBINE_TB,),
            in_specs=[
                pl.BlockSpec(memory_space=pl.ANY),
                pl.BlockSpec((COMBINE_TB, d), lambda i, dest: (i, 0)),
                pl.BlockSpec((COMBINE_TB, lanes), lambda i, dest: (i, 0)),
            ],
            out_specs=pl.BlockSpec((COMBINE_TB, d), lambda i, dest: (i, 0)),
            scratch_shapes=[pltpu.VMEM((2, TOP_K, COMBINE_TB * ROW_SUBLANES, LANES), jnp.uint32),
                            pltpu.SemaphoreType.DMA((2,))],
        ),
        out_shape=jax.ShapeDtypeStruct((t, d), F32),
        compiler_params=pltpu.CompilerParams(
            dimension_semantics=("arbitrary",), vmem_limit_bytes=VMEM_LIMIT),
        name="combine",
    )(dest_flat, y, h1, gate)


def _plan_items(counts, n_items_max):
    tiles = (counts + EXP_TM - 1) // EXP_TM
    tile_end = jnp.cumsum(tiles)
    tile_start = tile_end - tiles
    n_items = tile_end[-1]
    g = jnp.arange(n_items_max, dtype=jnp.int32)
    live = g < n_items
    gg = jnp.maximum(jnp.minimum(g, n_items - 1), 0)
    e = jnp.minimum(jnp.searchsorted(tile_end, gg, side="right"), N_EXPERTS - 1).astype(jnp.int32)
    nv = jnp.clip(counts[e] - (gg - tile_start[e]) * EXP_TM, 0, EXP_TM)
    nv = jnp.where(live, nv, 0).astype(jnp.int32)
    return tile_start * EXP_TM, e, gg.astype(jnp.int32), nv


def _layer(x2, bsz, seq, norm1_gain, w_in, q_norm_gain, k_norm_gain, sgu_norm_gain, w_spatial, b_spatial,
           w_branch_a, w_branch_b, w_out, norm2_gain, w_router, b_router, w_up, b_up, w_down, b_down):
    t, d = x2.shape
    proj = _proj(x2, norm1_gain[None, :], w_in, q_norm_gain[None, :], k_norm_gain[None, :])
    oa = _attn(proj, bsz, seq, d, q_norm_gain, k_norm_gain)
    n_items_max = -(-(t * TOP_K) // EXP_TM) + N_EXPERTS
    h1, xs0 = _merge(x2, oa, proj, w_spatial, b_spatial.T, sgu_norm_gain[None, :],
                     w_branch_a.astype(BF16), w_branch_b.astype(BF16), w_out.astype(BF16),
                     n_items_max * EXP_TM * ROW_SUBLANES)
    xp, idx, rank, gate, cnt = _router(h1, norm2_gain[None, :], w_router.T, b_router[:, None])

    counts = cnt[:, 0].astype(jnp.int32)
    seg_start, item_e, item_t, item_nv = _plan_items(counts, n_items_max)
    onehot = idx[:TOP_K, :, None] == jnp.arange(N_EXPERTS, dtype=jnp.int32)
    dest = (jnp.sum(jnp.where(onehot, seg_start.astype(jnp.int32), 0), axis=-1)
            + rank[:TOP_K]).reshape(-1)

    xs = _dispatch(dest, xp, xs0)
    y = _expert(item_e, item_t, item_nv, xs, w_up, b_up, w_down, b_down)
    return _combine(dest, y, h1, gate[:TOP_K].T)


def kernel(x, norm1_gain, w_in, q_norm_gain, k_norm_gain, sgu_norm_gain, w_spatial, b_spatial, w_branch_a,
           w_branch_b, w_out, norm2_gain, w_router, b_router, w_up, b_up, w_down, b_down):
    bsz, seq, d = x.shape
    h = x.reshape(bsz * seq, d)
    for l in range(norm1_gain.shape[0]):
        h = _layer(h, bsz, seq, norm1_gain[l], w_in[l], q_norm_gain[l], k_norm_gain[l], sgu_norm_gain[l],
                   w_spatial[l], b_spatial[l], w_branch_a[l], w_branch_b[l], w_out[l], norm2_gain[l],
                   w_router[l], b_router[l], w_up[l], b_up[l], w_down[l], b_down[l])
    return h.reshape(bsz, seq, d)
```

```python
import math

import jax
import jax.numpy as jnp
from jax import lax
from jax.experimental import pallas as pl
from jax.experimental.pallas import tpu as pltpu

F32 = jnp.float32
BF16 = jnp.bfloat16

EPS = 1e-6
N_HEADS = 8
HEAD_DIM = 128
SGU_GROUPS = 8
SGU_BLOCK = 128
CHUNK = 64
N_EXPERTS = 32
TOP_K = 4
SWIGLU_LIMIT = 7.0
SWIGLU_ALPHA = 1.702

F32_EXP_ZERO_BELOW = -105.0
LOG2_E = 1.4426950408889634

VMEM_LIMIT = 56 * 1024 * 1024
PROJ_TM = 1024
PROJ_TN = 1024
PROJ_ROWS = 512
ATT_T = 256
ATT_HP = 8
MERGE_TM = 256
ROUTER_TM = 1024
EXP_TM = 1280
EXP_SUB = 1280
EXP_GRAN = 128
EXP_TC = 256
DISPATCH_TB = 512
COMBINE_TB = 128
DMA_UNROLL = 8
FILL_MAX_ROWS = 4096


def _sigmoid(x):
    return 1.0 / (1.0 + jnp.exp(-x))


ROW_SUBLANES = 8
LANES = 128


def _store_rows_as_tiles(ref, rows2d):
    n = rows2d.shape[0]
    for s in range(ROW_SUBLANES):
        ref[pl.ds(s, n, stride=ROW_SUBLANES), :] = rows2d[:, s * LANES:(s + 1) * LANES]


def _tile_of_row(ref, r):
    start = r * ROW_SUBLANES
    if not isinstance(r, int):
        start = pl.multiple_of(start, ROW_SUBLANES)
    return ref.at[pl.ds(start, ROW_SUBLANES)]


def _load_tile_rows(ref, s, n):
    return ref[pl.ds(s, n, stride=ROW_SUBLANES), :]


def _pack_bf16_pair(lo, hi):
    lo_bits = lax.bitcast_convert_type(lo.astype(BF16).astype(F32), jnp.uint32)
    hi_bits = lax.bitcast_convert_type(hi.astype(BF16).astype(F32), jnp.uint32)
    return hi_bits | (lo_bits >> 16)


def _unpack_bf16_pair(w):
    lo = lax.bitcast_convert_type(w << 16, F32)
    hi = lax.bitcast_convert_type(w & jnp.uint32(0xFFFF0000), F32)
    return lo, hi


def _gelu_tanh(x):
    c = math.sqrt(2.0 / math.pi)
    return 0.5 * x * (1.0 + jnp.tanh(c * (x + 0.044715 * (x * x * x))))


def _proj_kernel(x_ref, g1_ref, w_ref, qg_ref, kg_ref, o_ref, xn_ref):
    j = pl.program_id(1)
    n_q = (N_HEADS * HEAD_DIM) // PROJ_TN

    @pl.when(j == 0)
    def _():
        x = x_ref[...]
        ms = jnp.mean(x * x, axis=-1, keepdims=True)
        xn_ref[...] = (x * lax.rsqrt(ms + EPS) * g1_ref[...]).astype(BF16)

    def head_norm(a):
        gain = jnp.where(j < n_q, qg_ref[...], kg_ref[...])
        outs = []
        for h in range(PROJ_TN // HEAD_DIM):
            ah = a[:, h * HEAD_DIM:(h + 1) * HEAD_DIM]
            ms = jnp.mean(ah * ah, axis=-1, keepdims=True)
            outs.append(ah * lax.rsqrt(ms + EPS) * gain)
        return jnp.concatenate(outs, axis=1)

    def tile(epilogue):
        w = w_ref[...].astype(BF16)
        for r in range(PROJ_TM // PROJ_ROWS):
            rows = slice(r * PROJ_ROWS, (r + 1) * PROJ_ROWS)
            acc = jnp.dot(xn_ref[rows, :], w, preferred_element_type=F32)
            o_ref[rows, :] = epilogue(acc).astype(BF16)

    pl.when(j < 2 * n_q)(lambda: tile(head_norm))
    pl.when((j >= 2 * n_q) & (j < 3 * n_q))(lambda: tile(lambda a: a))
    pl.when((j >= 3 * n_q) & (j < 5 * n_q))(lambda: tile(_gelu_tanh))
    pl.when(j >= 5 * n_q)(lambda: tile(_sigmoid))


def _proj(x2, g1, w_in, qg, kg):
    t, d = x2.shape
    n = w_in.shape[1]
    nj = n // PROJ_TN
    shift = (2 * d) // PROJ_TN
    return pl.pallas_call(
        _proj_kernel,
        grid=(t // PROJ_TM, nj),
        in_specs=[
            pl.BlockSpec((PROJ_TM, d), lambda i, j: (i, 0)),
            pl.BlockSpec((1, d), lambda i, j: (0, 0)),
            pl.BlockSpec((d, PROJ_TN), lambda i, j: (0, j)),
            pl.BlockSpec((1, HEAD_DIM), lambda i, j: (0, 0)),
            pl.BlockSpec((1, HEAD_DIM), lambda i, j: (0, 0)),
        ],
        out_specs=pl.BlockSpec((PROJ_TM, PROJ_TN), lambda i, j: (i, (j + shift) % nj)),
        out_shape=jax.ShapeDtypeStruct((t, n), BF16),
        scratch_shapes=[pltpu.VMEM((PROJ_TM, d), BF16)],
        compiler_params=pltpu.CompilerParams(
            dimension_semantics=("arbitrary", "arbitrary"), vmem_limit_bytes=VMEM_LIMIT),
        name="proj",
    )(x2, g1, w_in, qg, kg)


def _attn_kernel(stop_ref, q_ref, k_ref, v_ref, u_ref, o_ref):
    qi = pl.program_id(2)
    scale = HEAD_DIM ** -0.5 * LOG2_E
    stop_at = stop_ref[0]

    def block(j, accs, runs, diag):
        start = pl.multiple_of(j * ATT_T, ATT_T)
        tri = u_ref[...]
        if diag:
            row = lax.broadcasted_iota(jnp.int32, (ATT_T, ATT_T), 0)
            col = lax.broadcasted_iota(jnp.int32, (ATT_T, ATT_T), 1)
            past = col < row
        heads = range(ATT_HP)
        cols = [slice(h * HEAD_DIM, (h + 1) * HEAD_DIM) for h in heads]
        zs = [lax.dot_general(q_ref[:, cols[h]], k_ref[pl.ds(start, ATT_T), cols[h]],
                              (((1,), (1,)), ((), ())), preferred_element_type=F32) * scale for h in heads]
        sps = [jnp.maximum(z, 0.0) + jnp.log(1.0 + jnp.exp2(-jnp.abs(z))) * LOG2_E for z in zs]
        if diag:
            sps = [jnp.where(past, sp, 0.0) for sp in sps]
        his = [sp.astype(BF16) for sp in sps]
        los = [(sp - hi.astype(F32)).astype(BF16) for sp, hi in zip(sps, his)]
        css = [jnp.dot(hi, tri, preferred_element_type=F32) + jnp.dot(lo, tri, preferred_element_type=F32)
               for hi, lo in zip(his, los)]
        ws = [jnp.exp2(zs[h] - (runs[h] + css[h])) for h in heads]
        if diag:
            ws = [jnp.where(past, w, 0.0) for w in ws]
        new_accs = [accs[h] + jnp.dot(ws[h].astype(BF16), v_ref[pl.ds(start, ATT_T), cols[h]],
                                      preferred_element_type=F32) for h in heads]
        new_runs = [runs[h] + css[h][:, 0:1] for h in heads]
        return tuple(new_accs), tuple(new_runs)

    def smallest(runs):
        m = runs[0]
        for r in runs[1:]:
            m = jnp.minimum(m, r)
        return jnp.min(m)

    accs = tuple(jnp.zeros((ATT_T, HEAD_DIM), F32) for _ in range(ATT_HP))
    runs = tuple(jnp.zeros((ATT_T, 1), F32) for _ in range(ATT_HP))
    accs, runs = block(qi, accs, runs, True)

    def cond(c):
        j, _, _, low = c
        return (j >= 0) & (low < stop_at)

    def body(c):
        j, accs, runs, _ = c
        accs, runs = block(j, accs, runs, False)
        return j - 1, accs, runs, smallest(runs)

    _, accs, _, _ = lax.while_loop(cond, body, (qi - 1, accs, runs, smallest(runs)))
    for h in range(ATT_HP):
        o_ref[:, h * HEAD_DIM:(h + 1) * HEAD_DIM] = accs[h].astype(BF16)


def _attn(proj, bsz, seq, d, q_gain, k_gain):
    t = bsz * seq
    nq = seq // ATT_T
    width = ATT_HP * HEAD_DIM
    col0 = (2 * d) // width
    seg = (N_HEADS * HEAD_DIM) // width
    i = jnp.arange(ATT_T)
    tri = (i[:, None] >= i[None, :]).astype(BF16)
    zmax = 1.02 * math.sqrt(HEAD_DIM) * jnp.max(jnp.abs(q_gain)) * jnp.max(jnp.abs(k_gain))
    stop_at = ((zmax - F32_EXP_ZERO_BELOW) * LOG2_E).reshape(1).astype(F32)
    return pl.pallas_call(
        _attn_kernel,
        grid_spec=pltpu.PrefetchScalarGridSpec(
            num_scalar_prefetch=1,
            grid=(bsz, seg, nq),
            in_specs=[
                pl.BlockSpec((ATT_T, width), lambda b, h, qi, s: (b * nq + qi, col0 + h)),
                pl.BlockSpec((seq, width), lambda b, h, qi, s: (b, col0 + seg + h)),
                pl.BlockSpec((seq, width), lambda b, h, qi, s: (b, col0 + 2 * seg + h)),
                pl.BlockSpec((ATT_T, ATT_T), lambda b, h, qi, s: (0, 0)),
            ],
            out_specs=pl.BlockSpec((ATT_T, width), lambda b, h, qi, s: (b * nq + qi, h)),
        ),
        out_shape=jax.ShapeDtypeStruct((t, N_HEADS * HEAD_DIM), BF16),
        compiler_params=pltpu.CompilerParams(
            dimension_semantics=("arbitrary", "arbitrary", "arbitrary"), vmem_limit_bytes=VMEM_LIMIT),
        name="attn",
    )(stop_at, proj, proj, proj, tri)


def _merge_kernel(x_ref, oa_ref, u_ref, vv_ref, ga_ref, gb_ref, ws_ref, bst_ref, sg_ref,
                  wa_ref, wb_ref, wo_ref, h_ref, fill_ref, vln_ref, ob_ref, zero_ref, fill_sem):
    i = pl.program_id(0)
    z_rows = zero_ref.shape[0]
    n_fill = fill_ref.shape[0] // (pl.num_programs(0) * z_rows)

    @pl.when(i == 0)
    def _():
        zero_ref[...] = jnp.zeros_like(zero_ref)

    def fill_copy(j):
        start = pl.multiple_of((i * n_fill + j) * z_rows, ROW_SUBLANES)
        return pltpu.make_async_copy(zero_ref, fill_ref.at[pl.ds(start, z_rows)], fill_sem)

    for j in range(n_fill):
        fill_copy(j).start()

    vv = vv_ref[...].astype(F32)
    mu = jnp.mean(vv, axis=-1, keepdims=True)
    xc = vv - mu
    var = jnp.mean(xc * xc, axis=-1, keepdims=True)
    vln_ref[...] = (xc * lax.rsqrt(var + EPS) * sg_ref[...]).astype(BF16)

    row = lax.broadcasted_iota(jnp.int32, (SGU_BLOCK, SGU_BLOCK), 0)
    col = lax.broadcasted_iota(jnp.int32, (SGU_BLOCK, SGU_BLOCK), 1)
    mask = (col // CHUNK) <= (row // CHUNK)
    for g in range(SGU_GROUPS):
        wg = jnp.where(mask, ws_ref[g], 0.0).astype(BF16)
        bg = bst_ref[:, g:g + 1]
        cs = slice(g * SGU_BLOCK, (g + 1) * SGU_BLOCK)
        for n in range(MERGE_TM // SGU_BLOCK):
            rs = slice(n * SGU_BLOCK, (n + 1) * SGU_BLOCK)
            sgu = jnp.dot(wg, vln_ref[rs, cs], preferred_element_type=F32) + bg
            ob_ref[rs, cs] = (u_ref[rs, cs].astype(F32) * sgu).astype(BF16)

    ya = jnp.dot(oa_ref[...], wa_ref[...], preferred_element_type=F32)
    yb = jnp.dot(ob_ref[...], wb_ref[...], preferred_element_type=F32)
    y = ga_ref[...].astype(F32) * ya + gb_ref[...].astype(F32) * yb
    h_ref[...] = x_ref[...] + jnp.dot(y.astype(BF16), wo_ref[...], preferred_element_type=F32)

    for j in range(n_fill):
        fill_copy(j).wait()


def _fill_rows_per_copy(rows_per_step):
    for n in range(1, rows_per_step + 1):
        if rows_per_step % n == 0 and (rows_per_step // n) % ROW_SUBLANES == 0 and rows_per_step // n <= FILL_MAX_ROWS:
            return rows_per_step // n
    raise ValueError(f"no aligned split of {rows_per_step} fill rows")


def _merge(x2, oa, proj, w_spatial, b_spatial_t, sgu_gain, wa, wb, wo, n_fill_rows):
    t, d = x2.shape
    aw = oa.shape[1]
    sw = SGU_GROUPS * SGU_BLOCK
    ucol = (2 * d + 3 * aw) // sw
    const = lambda shape: pl.BlockSpec(shape, lambda i: (0,) * len(shape), pipeline_mode=pl.Buffered(1))
    n_steps = t // MERGE_TM
    assert n_fill_rows % n_steps == 0
    z_rows = _fill_rows_per_copy(n_fill_rows // n_steps)
    return pl.pallas_call(
        _merge_kernel,
        grid=(t // MERGE_TM,),
        in_specs=[
            pl.BlockSpec((MERGE_TM, d), lambda i: (i, 0)),
            pl.BlockSpec((MERGE_TM, aw), lambda i: (i, 0)),
            pl.BlockSpec((MERGE_TM, sw), lambda i: (i, ucol)),
            pl.BlockSpec((MERGE_TM, sw), lambda i: (i, ucol + 1)),
            pl.BlockSpec((MERGE_TM, d), lambda i: (i, 0)),
            pl.BlockSpec((MERGE_TM, d), lambda i: (i, 1)),
            const((SGU_GROUPS, SGU_BLOCK, SGU_BLOCK)),
            const((SGU_BLOCK, SGU_GROUPS)),
            const((1, sw)),
            const((aw, d)),
            const((sw, d)),
            const((d, d)),
        ],
        out_specs=[pl.BlockSpec((MERGE_TM, d), lambda i: (i, 0)), pl.BlockSpec(memory_space=pl.ANY)],
        out_shape=[jax.ShapeDtypeStruct((t, d), F32),
                   jax.ShapeDtypeStruct((n_fill_rows, LANES), jnp.uint32)],
        scratch_shapes=[pltpu.VMEM((MERGE_TM, sw), BF16), pltpu.VMEM((MERGE_TM, sw), BF16),
                        pltpu.VMEM((z_rows, LANES), jnp.uint32), pltpu.SemaphoreType.DMA(())],
        compiler_params=pltpu.CompilerParams(
            dimension_semantics=("arbitrary",), vmem_limit_bytes=VMEM_LIMIT),
        name="merge",
    )(x2, oa, proj, proj, proj, proj, w_spatial, b_spatial_t, sgu_gain, wa, wb, wo)


def _router_kernel(h_ref, g2_ref, wrt_ref, brt_ref, ut_ref, xp_ref, idx_ref, rank_ref, gate_ref, cnt_ref,
                   carry_ref):
    @pl.when(pl.program_id(0) == 0)
    def _():
        carry_ref[...] = jnp.zeros_like(carry_ref)

    h = h_ref[...]
    half = h.shape[1] // 2
    ms = jnp.mean(h * h, axis=-1, keepdims=True)
    xn = h * lax.rsqrt(ms + EPS) * g2_ref[...]
    xn_hi = xn.astype(BF16)
    xn_lo = (xn - xn_hi.astype(F32)).astype(BF16)
    hi_bits = lax.bitcast_convert_type(xn_hi.astype(F32), jnp.uint32)
    _store_rows_as_tiles(xp_ref, hi_bits[:, half:] | (hi_bits[:, :half] >> 16))

    wr = wrt_ref[...]
    wr_hi = wr.astype(BF16)
    wr_lo = (wr - wr_hi.astype(F32)).astype(BF16)
    nt = (((1,), (1,)), ((), ()))
    logits = (lax.dot_general(wr_hi, xn_hi, nt, preferred_element_type=F32)
              + lax.dot_general(wr_hi, xn_lo, nt, preferred_element_type=F32)
              + lax.dot_general(wr_lo, xn_hi, nt, preferred_element_type=F32)) + brt_ref[...]
    sub = lax.broadcasted_iota(jnp.int32, logits.shape, 0).astype(F32)
    out_row = lax.broadcasted_iota(jnp.int32, idx_ref.shape, 0)
    work = logits
    vals, sels = [], []
    idx_out = jnp.zeros(idx_ref.shape, jnp.int32)
    for k in range(TOP_K):
        m = jnp.max(work, axis=0, keepdims=True)
        idx = jnp.min(jnp.where(work == m, sub, float(N_EXPERTS)), axis=0, keepdims=True)
        sel = sub == idx
        work = jnp.where(sel, -jnp.inf, work)
        vals.append(m)
        sels.append(sel)
        idx_out = jnp.where(out_row == k, idx.astype(jnp.int32), idx_out)
    idx_ref[...] = idx_out

    exps = [jnp.exp(v - vals[0]) for v in vals]
    denom = exps[0] + exps[1] + exps[2] + exps[3]
    gate_out = jnp.zeros(gate_ref.shape, F32)
    for k in range(TOP_K):
        gate_out = jnp.where(out_row == k, exps[k] / denom, gate_out)
    gate_ref[...] = gate_out

    member = (sels[0] | sels[1] | sels[2] | sels[3])
    before = jnp.dot(member.astype(BF16), ut_ref[...], preferred_element_type=F32)
    pos = carry_ref[...] + before
    rank_out = jnp.zeros(rank_ref.shape, jnp.int32)
    for k in range(TOP_K):
        r = jnp.sum(jnp.where(sels[k], pos, 0.0), axis=0, keepdims=True)
        rank_out = jnp.where(out_row == k, r.astype(jnp.int32), rank_out)
    rank_ref[...] = rank_out

    carry_ref[...] = carry_ref[...] + jnp.sum(member.astype(F32), axis=1, keepdims=True)
    cnt_ref[...] = jnp.broadcast_to(carry_ref[...], cnt_ref.shape)


def _router(h1, g2, w_router_t, b_router_t):
    t, d = h1.shape
    const = lambda shape: pl.BlockSpec(shape, lambda i: (0,) * len(shape))
    i = jnp.arange(ROUTER_TM)
    ut = (i[:, None] < i[None, :]).astype(BF16)
    rows, lanes = 8, 128
    tok = lambda: pl.BlockSpec((rows, ROUTER_TM), lambda i: (0, i))
    return pl.pallas_call(
        _router_kernel,
        grid=(t // ROUTER_TM,),
        in_specs=[
            pl.BlockSpec((ROUTER_TM, d), lambda i: (i, 0)),
            const((1, d)),
            const((N_EXPERTS, d)),
            const((N_EXPERTS, 1)),
            const((ROUTER_TM, ROUTER_TM)),
        ],
        out_specs=[
            pl.BlockSpec((ROUTER_TM * ROW_SUBLANES, LANES), lambda i: (i, 0)),
            tok(), tok(), tok(),
            pl.BlockSpec((N_EXPERTS, lanes), lambda i: (0, 0)),
        ],
        out_shape=[
            jax.ShapeDtypeStruct((t * ROW_SUBLANES, LANES), jnp.uint32),
            jax.ShapeDtypeStruct((rows, t), jnp.int32),
            jax.ShapeDtypeStruct((rows, t), jnp.int32),
            jax.ShapeDtypeStruct((rows, t), F32),
            jax.ShapeDtypeStruct((N_EXPERTS, lanes), F32),
        ],
        scratch_shapes=[pltpu.VMEM((N_EXPERTS, 1), F32)],
        compiler_params=pltpu.CompilerParams(
            dimension_semantics=("arbitrary",), vmem_limit_bytes=VMEM_LIMIT),
        name="router",
    )(h1, g2, w_router_t, b_router_t, ut)


def _dispatch_kernel(dest_ref, xp_ref, xs_in_ref, xs_ref, sem):
    del xs_in_ref
    i = pl.program_id(0)

    n_tok = dest_ref.shape[0] // TOP_K

    def row_copy(s, r):
        return pltpu.make_async_copy(_tile_of_row(xp_ref, s), _tile_of_row(xs_ref, r), sem)

    def issue(b, _):
        for j in range(DMA_UNROLL):
            s = b * DMA_UNROLL + j
            for k in range(TOP_K):
                row_copy(s, dest_ref[k * n_tok + i * DISPATCH_TB + s]).start(priority=k % 2)
        return 0

    def drain(b, _):
        for _ in range(DMA_UNROLL * TOP_K):
            row_copy(0, 0).wait()
        return 0

    lax.fori_loop(0, DISPATCH_TB // DMA_UNROLL, issue, 0)
    lax.fori_loop(0, DISPATCH_TB // DMA_UNROLL, drain, 0)


def _dispatch(dest_flat, xp, xs0):
    t = xp.shape[0] // ROW_SUBLANES
    n_rows = xs0.shape[0] // ROW_SUBLANES
    return pl.pallas_call(
        _dispatch_kernel,
        grid_spec=pltpu.PrefetchScalarGridSpec(
            num_scalar_prefetch=1,
            grid=(t // DISPATCH_TB,),
            in_specs=[pl.BlockSpec((DISPATCH_TB * ROW_SUBLANES, LANES), lambda i, dest: (i, 0)),
                      pl.BlockSpec(memory_space=pl.ANY)],
            out_specs=pl.BlockSpec(memory_space=pl.ANY),
            scratch_shapes=[pltpu.SemaphoreType.DMA(())],
        ),
        out_shape=jax.ShapeDtypeStruct((n_rows * ROW_SUBLANES, LANES), jnp.uint32),
        input_output_aliases={2: 0},
        compiler_params=pltpu.CompilerParams(
            dimension_semantics=("arbitrary",), has_side_effects=True),
        name="dispatch",
    )(dest_flat, xp, xs0)


def _expert_pieces(n_rows):
    k = -(-n_rows // EXP_SUB)
    base, extra = divmod(n_rows // EXP_GRAN, k)
    return [(base + (1 if j < extra else 0)) * EXP_GRAN for j in range(k)]


def _expert_kernel(ie_ref, it_ref, nv_ref, xs_ref, wg_ref, wu_ref, wd_ref, bg_ref, bu_ref, bd_ref,
                   y_ref, acc_ref):
    del ie_ref, it_ref
    i = pl.program_id(0)
    c = pl.program_id(1)
    nc = pl.num_programs(1)
    nv = nv_ref[i]
    half = ROW_SUBLANES * LANES

    @pl.when((i == 0) & (c == 0))
    def _():
        acc_ref[...] = jnp.zeros_like(acc_ref)

    def run(n_rows):
        wg = wg_ref[0].astype(BF16)
        wu = wu_ref[0].astype(BF16)
        wd = wd_ref[0].astype(BF16)
        first = c == 0
        lo = 0
        for m in _expert_pieces(n_rows):
            rows = slice(lo, lo + m)
            tiles = lambda s, lo=lo, m=m: pl.ds(lo * ROW_SUBLANES + s, m, stride=ROW_SUBLANES)
            pairs = [_unpack_bf16_pair(xs_ref[tiles(s), :]) for s in range(ROW_SUBLANES)]
            xb = jnp.concatenate([p[0].astype(BF16) for p in pairs] + [p[1].astype(BF16) for p in pairs], axis=1)
            g = jnp.dot(xb, wg, preferred_element_type=F32) + bg_ref[0]
            u = jnp.dot(xb, wu, preferred_element_type=F32) + bu_ref[0]
            g = jnp.minimum(g, SWIGLU_LIMIT)
            u = jnp.clip(u, -SWIGLU_LIMIT, SWIGLU_LIMIT)
            a = g * _sigmoid(SWIGLU_ALPHA * g) * (u + 1.0)
            y = jnp.dot(a.astype(BF16), wd, preferred_element_type=F32)
            acc_ref[rows, :] = jnp.where(first, jnp.broadcast_to(bd_ref[0], y.shape), acc_ref[rows, :]) + y
            lo += m

    for n_rows in range(EXP_GRAN, EXP_TM + 1, EXP_GRAN):
        pl.when((nv > n_rows - EXP_GRAN) & (nv <= n_rows))(lambda n_rows=n_rows: run(n_rows))

    @pl.when((c == nc - 1) & (nv > 0))
    def _():
        for s in range(ROW_SUBLANES):
            lo = acc_ref[:, s * LANES:(s + 1) * LANES]
            hi = acc_ref[:, half + s * LANES:half + (s + 1) * LANES]
            y_ref[pl.ds(s, EXP_TM, stride=ROW_SUBLANES), :] = _pack_bf16_pair(lo, hi)


def _expert(item_e, item_t, item_nv, xs, w_up, b_up, w_down, b_down):
    n_rows = xs.shape[0] // ROW_SUBLANES
    ne, d, two_f = w_up.shape
    assert d == 2 * ROW_SUBLANES * LANES
    tile = (EXP_TM * ROW_SUBLANES, LANES)
    f = two_f // 2
    nc = f // EXP_TC
    n_items = item_e.shape[0]
    b_up3 = b_up.reshape(ne, 1, two_f)
    b_down3 = b_down.reshape(ne, 1, d)

    def chunk(c, nv_ref, i):
        return jnp.where(nv_ref[i] > 0, c, nc - 1)

    return pl.pallas_call(
        _expert_kernel,
        grid_spec=pltpu.PrefetchScalarGridSpec(
            num_scalar_prefetch=3,
            grid=(n_items, nc),
            in_specs=[
                pl.BlockSpec(tile, lambda i, c, ie, it, nv: (it[i], 0)),
                pl.BlockSpec((1, d, EXP_TC), lambda i, c, ie, it, nv: (ie[i], 0, chunk(c, nv, i))),
                pl.BlockSpec((1, d, EXP_TC), lambda i, c, ie, it, nv: (ie[i], 0, nc + chunk(c, nv, i))),
                pl.BlockSpec((1, EXP_TC, d), lambda i, c, ie, it, nv: (ie[i], chunk(c, nv, i), 0)),
                pl.BlockSpec((1, 1, EXP_TC), lambda i, c, ie, it, nv: (ie[i], 0, chunk(c, nv, i))),
                pl.BlockSpec((1, 1, EXP_TC), lambda i, c, ie, it, nv: (ie[i], 0, nc + chunk(c, nv, i))),
                pl.BlockSpec((1, 1, d), lambda i, c, ie, it, nv: (ie[i], 0, 0)),
            ],
            out_specs=pl.BlockSpec(tile, lambda i, c, ie, it, nv: (it[i], 0)),
            scratch_shapes=[pltpu.VMEM((EXP_TM, d), F32)],
        ),
        out_shape=jax.ShapeDtypeStruct((n_rows * ROW_SUBLANES, LANES), jnp.uint32),
        input_output_aliases={3: 0},
        compiler_params=pltpu.CompilerParams(
            dimension_semantics=("arbitrary", "arbitrary"), vmem_limit_bytes=VMEM_LIMIT),
        name="expert",
    )(item_e, item_t, item_nv, xs, w_up, w_up, w_down, b_up3, b_up3, b_down3)


def _combine_kernel(dest_ref, y_ref, h_ref, gate_ref, o_ref, buf_ref, sem):
    i = pl.program_id(0)
    n = pl.num_programs(0)
    n_tok = dest_ref.shape[0] // TOP_K

    def row_copy(r, slot, k, s):
        return pltpu.make_async_copy(_tile_of_row(y_ref, r), _tile_of_row(buf_ref.at[slot, k], s), sem.at[slot])

    def gather(step, slot):
        def issue(b, _):
            for j in range(DMA_UNROLL):
                s = b * DMA_UNROLL + j
                for k in range(TOP_K):
                    row_copy(dest_ref[k * n_tok + step * COMBINE_TB + s], slot, k, s).start(priority=k % 2)
            return 0
        lax.fori_loop(0, COMBINE_TB // DMA_UNROLL, issue, 0)

    @pl.when(i == 0)
    def _():
        gather(0, 0)

    for slot in range(2):
        @pl.when((i + 1 < n) & ((i + 1) % 2 == slot))
        def _():
            gather(i + 1, slot)

    for slot in range(2):
        @pl.when(i % 2 == slot)
        def _():
            def drain(b, _):
                for _ in range(DMA_UNROLL * TOP_K):
                    row_copy(0, slot, 0, 0).wait()
                return 0
            lax.fori_loop(0, COMBINE_TB // DMA_UNROLL, drain, 0)
            half = ROW_SUBLANES * LANES
            gates = [jnp.broadcast_to(gate_ref[:, k:k + 1], (COMBINE_TB, LANES)) for k in range(TOP_K)]
            for s in range(ROW_SUBLANES):
                lo_cols = slice(s * LANES, (s + 1) * LANES)
                hi_cols = slice(half + s * LANES, half + (s + 1) * LANES)
                acc_lo = h_ref[:, lo_cols]
                acc_hi = h_ref[:, hi_cols]
                for k in range(TOP_K):
                    lo, hi = _unpack_bf16_pair(_load_tile_rows(buf_ref.at[slot, k], s, COMBINE_TB))
                    acc_lo = acc_lo + gates[k] * lo
                    acc_hi = acc_hi + gates[k] * hi
                o_ref[:, lo_cols] = acc_lo
                o_ref[:, hi_cols] = acc_hi


def _combine(dest_flat, y, h1, gate):
    t, d = h1.shape
    lanes = gate.shape[1]
    return pl.pallas_call(
        _combine_kernel,
        grid_spec=pltpu.PrefetchScalarGridSpec(
            num_scalar_prefetch=1,
            grid=(t // COMBINE_TB,),
            in_specs=[
                pl.BlockSpec(memory_space=pl.ANY),
                pl.BlockSpec((COMBINE_TB, d), lambda i, dest: (i, 0)),
                pl.BlockSpec((COMBINE_TB, lanes), lambda i, dest: (i, 0)),
            ],
            out_specs=pl.BlockSpec((COMBINE_TB, d), lambda i, dest: (i, 0)),
            scratch_shapes=[pltpu.VMEM((2, TOP_K, COMBINE_TB * ROW_SUBLANES, LANES), jnp.uint32),
                            pltpu.SemaphoreType.DMA((2,))],
        ),
        out_shape=jax.ShapeDtypeStruct((t, d), F32),
        compiler_params=pltpu.CompilerParams(
            dimension_semantics=("arbitrary",), vmem_limit_bytes=VMEM_LIMIT),
        name="combine",
    )(dest_flat, y, h1, gate)


def _plan_items(counts, n_items_max):
    tiles = (counts + EXP_TM - 1) // EXP_TM
    tile_end = jnp.cumsum(tiles)
    tile_start = tile_end - tiles
    n_items = tile_end[-1]
    g = jnp.arange(n_items_max, dtype=jnp.int32)
    live = g < n_items
    gg = jnp.maximum(jnp.minimum(g, n_items - 1), 0)
    e = jnp.minimum(jnp.searchsorted(tile_end, gg, side="right"), N_EXPERTS - 1).astype(jnp.int32)
    nv = jnp.clip(counts[e] - (gg - tile_start[e]) * EXP_TM, 0, EXP_TM)
    nv = jnp.where(live, nv, 0).astype(jnp.int32)
    return tile_start * EXP_TM, e, gg.astype(jnp.int32), nv


def _layer(x2, bsz, seq, norm1_gain, w_in, q_norm_gain, k_norm_gain, sgu_norm_gain, w_spatial, b_spatial,
           w_branch_a, w_branch_b, w_out, norm2_gain, w_router, b_router, w_up, b_up, w_down, b_down):
    t, d = x2.shape
    proj = _proj(x2, norm1_gain[None, :], w_in, q_norm_gain[None, :], k_norm_gain[None, :])
    oa = _attn(proj, bsz, seq, d, q_norm_gain, k_norm_gain)
    n_items_max = -(-(t * TOP_K) // EXP_TM) + N_EXPERTS
    h1, xs0 = _merge(x2, oa, proj, w_spatial, b_spatial.T, sgu_norm_gain[None, :],
                     w_branch_a.astype(BF16), w_branch_b.astype(BF16), w_out.astype(BF16),
                     n_items_max * EXP_TM * ROW_SUBLANES)
    xp, idx, rank, gate, cnt = _router(h1, norm2_gain[None, :], w_router.T, b_router[:, None])

    counts = cnt[:, 0].astype(jnp.int32)
    seg_start, item_e, item_t, item_nv = _plan_items(counts, n_items_max)
    onehot = idx[:TOP_K, :, None] == jnp.arange(N_EXPERTS, dtype=jnp.int32)
    dest = (jnp.sum(jnp.where(onehot, seg_start.astype(jnp.int32), 0), axis=-1)
            + rank[:TOP_K]).reshape(-1)

    xs = _dispatch(dest, xp, xs0)
    y = _expert(item_e, item_t, item_nv, xs, w_up, b_up, w_down, b_down)
    return _combine(dest, y, h1, gate[:TOP_K].T)


def kernel(x, norm1_gain, w_in, q_norm_gain, k_norm_gain, sgu_norm_gain, w_spatial, b_spatial, w_branch_a,
           w_branch_b, w_out, norm2_gain, w_router, b_router, w_up, b_up, w_down, b_down):
    bsz, seq, d = x.shape
    h = x.reshape(bsz * seq, d)
    for l in range(norm1_gain.shape[0]):
        h = _layer(h, bsz, seq, norm1_gain[l], w_in[l], q_norm_gain[l], k_norm_gain[l], sgu_norm_gain[l],
                   w_spatial[l], b_spatial[l], w_branch_a[l], w_branch_b[l], w_out[l], norm2_gain[l],
                   w_router[l], b_router[l], w_up[l], b_up[l], w_down[l], b_down[l])
    return h.reshape(bsz, seq, d)
```

```python
import math

import jax
import jax.numpy as jnp
from jax import lax
from jax.experimental import pallas as pl
from jax.experimental.pallas import tpu as pltpu

F32 = jnp.float32
BF16 = jnp.bfloat16

EPS = 1e-6
N_HEADS = 8
HEAD_DIM = 128
SGU_GROUPS = 8
SGU_BLOCK = 128
CHUNK = 64
N_EXPERTS = 32
TOP_K = 4
SWIGLU_LIMIT = 7.0
SWIGLU_ALPHA = 1.702

F32_EXP_ZERO_BELOW = -105.0
LOG2_E = 1.4426950408889634

VMEM_LIMIT = 56 * 1024 * 1024
PROJ_TM = 1024
PROJ_TN = 1024
PROJ_ROWS = 512
ATT_T = 256
ATT_HP = 8
MERGE_TM = 256
ROUTER_TM = 1024
EXP_TM = 1152
EXP_SUB = 1152
EXP_GRAN = 128
EXP_TC = 256
DISPATCH_TB = 512
COMBINE_TB = 128
DMA_UNROLL = 8
FILL_MAX_ROWS = 4096


def _sigmoid(x):
    return 1.0 / (1.0 + jnp.exp(-x))


ROW_SUBLANES = 8
LANES = 128


def _store_rows_as_tiles(ref, rows2d):
    n = rows2d.shape[0]
    for s in range(ROW_SUBLANES):
        ref[pl.ds(s, n, stride=ROW_SUBLANES), :] = rows2d[:, s * LANES:(s + 1) * LANES]


def _tile_of_row(ref, r):
    start = r * ROW_SUBLANES
    if not isinstance(r, int):
        start = pl.multiple_of(start, ROW_SUBLANES)
    return ref.at[pl.ds(start, ROW_SUBLANES)]


def _load_tile_rows(ref, s, n):
    return ref[pl.ds(s, n, stride=ROW_SUBLANES), :]


def _pack_bf16_pair(lo, hi):
    lo_bits = lax.bitcast_convert_type(lo.astype(BF16).astype(F32), jnp.uint32)
    hi_bits = lax.bitcast_convert_type(hi.astype(BF16).astype(F32), jnp.uint32)
    return hi_bits | (lo_bits >> 16)


def _unpack_bf16_pair(w):
    lo = lax.bitcast_convert_type(w << 16, F32)
    hi = lax.bitcast_convert_type(w & jnp.uint32(0xFFFF0000), F32)
    return lo, hi


def _gelu_tanh(x):
    c = math.sqrt(2.0 / math.pi)
    return 0.5 * x * (1.0 + jnp.tanh(c * (x + 0.044715 * (x * x * x))))


def _proj_kernel(x_ref, g1_ref, w_ref, qg_ref, kg_ref, o_ref, xn_ref):
    j = pl.program_id(1)
    n_q = (N_HEADS * HEAD_DIM) // PROJ_TN

    @pl.when(j == 0)
    def _():
        x = x_ref[...]
        ms = jnp.mean(x * x, axis=-1, keepdims=True)
        xn_ref[...] = (x * lax.rsqrt(ms + EPS) * g1_ref[...]).astype(BF16)

    def head_norm(a):
        gain = jnp.where(j < n_q, qg_ref[...], kg_ref[...])
        outs = []
        for h in range(PROJ_TN // HEAD_DIM):
            ah = a[:, h * HEAD_DIM:(h + 1) * HEAD_DIM]
            ms = jnp.mean(ah * ah, axis=-1, keepdims=True)
            outs.append(ah * lax.rsqrt(ms + EPS) * gain)
        return jnp.concatenate(outs, axis=1)

    def tile(epilogue):
        w = w_ref[...].astype(BF16)
        for r in range(PROJ_TM // PROJ_ROWS):
            rows = slice(r * PROJ_ROWS, (r + 1) * PROJ_ROWS)
            acc = jnp.dot(xn_ref[rows, :], w, preferred_element_type=F32)
            o_ref[rows, :] = epilogue(acc).astype(BF16)

    pl.when(j < 2 * n_q)(lambda: tile(head_norm))
    pl.when((j >= 2 * n_q) & (j < 3 * n_q))(lambda: tile(lambda a: a))
    pl.when((j >= 3 * n_q) & (j < 5 * n_q))(lambda: tile(_gelu_tanh))
    pl.when(j >= 5 * n_q)(lambda: tile(_sigmoid))


def _proj(x2, g1, w_in, qg, kg):
    t, d = x2.shape
    n = w_in.shape[1]
    nj = n // PROJ_TN
    shift = (2 * d) // PROJ_TN
    return pl.pallas_call(
        _proj_kernel,
        grid=(t // PROJ_TM, nj),
        in_specs=[
            pl.BlockSpec((PROJ_TM, d), lambda i, j: (i, 0)),
            pl.BlockSpec((1, d), lambda i, j: (0, 0)),
            pl.BlockSpec((d, PROJ_TN), lambda i, j: (0, j)),
            pl.BlockSpec((1, HEAD_DIM), lambda i, j: (0, 0)),
            pl.BlockSpec((1, HEAD_DIM), lambda i, j: (0, 0)),
        ],
        out_specs=pl.BlockSpec((PROJ_TM, PROJ_TN), lambda i, j: (i, (j + shift) % nj)),
        out_shape=jax.ShapeDtypeStruct((t, n), BF16),
        scratch_shapes=[pltpu.VMEM((PROJ_TM, d), BF16)],
        compiler_params=pltpu.CompilerParams(
            dimension_semantics=("arbitrary", "arbitrary"), vmem_limit_bytes=VMEM_LIMIT),
        name="proj",
    )(x2, g1, w_in, qg, kg)


def _attn_kernel(stop_ref, q_ref, k_ref, v_ref, u_ref, o_ref):
    qi = pl.program_id(2)
    scale = HEAD_DIM ** -0.5 * LOG2_E
    stop_at = stop_ref[0]

    def block(j, accs, runs, diag):
        start = pl.multiple_of(j * ATT_T, ATT_T)
        tri = u_ref[...]
        if diag:
            row = lax.broadcasted_iota(jnp.int32, (ATT_T, ATT_T), 0)
            col = lax.broadcasted_iota(jnp.int32, (ATT_T, ATT_T), 1)
            past = col < row
        heads = range(ATT_HP)
        cols = [slice(h * HEAD_DIM, (h + 1) * HEAD_DIM) for h in heads]
        zs = [lax.dot_general(q_ref[:, cols[h]], k_ref[pl.ds(start, ATT_T), cols[h]],
                              (((1,), (1,)), ((), ())), preferred_element_type=F32) * scale for h in heads]
        sps = [jnp.maximum(z, 0.0) + jnp.log(1.0 + jnp.exp2(-jnp.abs(z))) * LOG2_E for z in zs]
        if diag:
            sps = [jnp.where(past, sp, 0.0) for sp in sps]
        his = [sp.astype(BF16) for sp in sps]
        los = [(sp - hi.astype(F32)).astype(BF16) for sp, hi in zip(sps, his)]
        css = [jnp.dot(hi, tri, preferred_element_type=F32) + jnp.dot(lo, tri, preferred_element_type=F32)
               for hi, lo in zip(his, los)]
        ws = [jnp.exp2(zs[h] - (runs[h] + css[h])) for h in heads]
        if diag:
            ws = [jnp.where(past, w, 0.0) for w in ws]
        new_accs = [accs[h] + jnp.dot(ws[h].astype(BF16), v_ref[pl.ds(start, ATT_T), cols[h]],
                                      preferred_element_type=F32) for h in heads]
        new_runs = [runs[h] + css[h][:, 0:1] for h in heads]
        return tuple(new_accs), tuple(new_runs)

    def smallest(runs):
        m = runs[0]
        for r in runs[1:]:
            m = jnp.minimum(m, r)
        return jnp.min(m)

    accs = tuple(jnp.zeros((ATT_T, HEAD_DIM), F32) for _ in range(ATT_HP))
    runs = tuple(jnp.zeros((ATT_T, 1), F32) for _ in range(ATT_HP))
    accs, runs = block(qi, accs, runs, True)

    def cond(c):
        j, _, _, low = c
        return (j >= 0) & (low < stop_at)

    def body(c):
        j, accs, runs, _ = c
        accs, runs = block(j, accs, runs, False)
        return j - 1, accs, runs, smallest(runs)

    _, accs, _, _ = lax.while_loop(cond, body, (qi - 1, accs, runs, smallest(runs)))
    for h in range(ATT_HP):
        o_ref[:, h * HEAD_DIM:(h + 1) * HEAD_DIM] = accs[h].astype(BF16)


def _attn(proj, bsz, seq, d, q_gain, k_gain):
    t = bsz * seq
    nq = seq // ATT_T
    width = ATT_HP * HEAD_DIM
    col0 = (2 * d) // width
    seg = (N_HEADS * HEAD_DIM) // width
    i = jnp.arange(ATT_T)
    tri = (i[:, None] >= i[None, :]).astype(BF16)
    zmax = 1.02 * math.sqrt(HEAD_DIM) * jnp.max(jnp.abs(q_gain)) * jnp.max(jnp.abs(k_gain))
    stop_at = ((zmax - F32_EXP_ZERO_BELOW) * LOG2_E).reshape(1).astype(F32)
    return pl.pallas_call(
        _attn_kernel,
        grid_spec=pltpu.PrefetchScalarGridSpec(
            num_scalar_prefetch=1,
            grid=(bsz, seg, nq),
            in_specs=[
                pl.BlockSpec((ATT_T, width), lambda b, h, qi, s: (b * nq + qi, col0 + h)),
                pl.BlockSpec((seq, width), lambda b, h, qi, s: (b, col0 + seg + h)),
                pl.BlockSpec((seq, width), lambda b, h, qi, s: (b, col0 + 2 * seg + h)),
                pl.BlockSpec((ATT_T, ATT_T), lambda b, h, qi, s: (0, 0)),
            ],
            out_specs=pl.BlockSpec((ATT_T, width), lambda b, h, qi, s: (b * nq + qi, h)),
        ),
        out_shape=jax.ShapeDtypeStruct((t, N_HEADS * HEAD_DIM), BF16),
        compiler_params=pltpu.CompilerParams(
            dimension_semantics=("arbitrary", "arbitrary", "arbitrary"), vmem_limit_bytes=VMEM_LIMIT),
        name="attn",
    )(stop_at, proj, proj, proj, tri)


def _merge_kernel(x_ref, oa_ref, u_ref, vv_ref, ga_ref, gb_ref, ws_ref, bst_ref, sg_ref,
                  wa_ref, wb_ref, wo_ref, h_ref, fill_ref, vln_ref, ob_ref, zero_ref, fill_sem):
    i = pl.program_id(0)
    z_rows = zero_ref.shape[0]
    n_fill = fill_ref.shape[0] // (pl.num_programs(0) * z_rows)

    @pl.when(i == 0)
    def _():
        zero_ref[...] = jnp.zeros_like(zero_ref)

    def fill_copy(j):
        start = pl.multiple_of((i * n_fill + j) * z_rows, ROW_SUBLANES)
        return pltpu.make_async_copy(zero_ref, fill_ref.at[pl.ds(start, z_rows)], fill_sem)

    for j in range(n_fill):
        fill_copy(j).start()

    vv = vv_ref[...].astype(F32)
    mu = jnp.mean(vv, axis=-1, keepdims=True)
    xc = vv - mu
    var = jnp.mean(xc * xc, axis=-1, keepdims=True)
    vln_ref[...] = (xc * lax.rsqrt(var + EPS) * sg_ref[...]).astype(BF16)

    row = lax.broadcasted_iota(jnp.int32, (SGU_BLOCK, SGU_BLOCK), 0)
    col = lax.broadcasted_iota(jnp.int32, (SGU_BLOCK, SGU_BLOCK), 1)
    mask = (col // CHUNK) <= (row // CHUNK)
    for g in range(SGU_GROUPS):
        wg = jnp.where(mask, ws_ref[g], 0.0).astype(BF16)
        bg = bst_ref[:, g:g + 1]
        cs = slice(g * SGU_BLOCK, (g + 1) * SGU_BLOCK)
        blocks = [slice(n * SGU_BLOCK, (n + 1) * SGU_BLOCK) for n in range(MERGE_TM // SGU_BLOCK)]
        sgu = jnp.dot(wg, jnp.concatenate([vln_ref[rs, cs] for rs in blocks], axis=1),
                      preferred_element_type=F32) + bg
        for n, rs in enumerate(blocks):
            ob_ref[rs, cs] = (u_ref[rs, cs].astype(F32) * sgu[:, n * SGU_BLOCK:(n + 1) * SGU_BLOCK]).astype(BF16)

    ya = jnp.dot(oa_ref[...], wa_ref[...], preferred_element_type=F32)
    yb = jnp.dot(ob_ref[...], wb_ref[...], preferred_element_type=F32)
    y = ga_ref[...].astype(F32) * ya + gb_ref[...].astype(F32) * yb
    h_ref[...] = x_ref[...] + jnp.dot(y.astype(BF16), wo_ref[...], preferred_element_type=F32)

    for j in range(n_fill):
        fill_copy(j).wait()


def _fill_rows_per_copy(rows_per_step):
    for n in range(1, rows_per_step + 1):
        if rows_per_step % n == 0 and (rows_per_step // n) % ROW_SUBLANES == 0 and rows_per_step // n <= FILL_MAX_ROWS:
            return rows_per_step // n
    raise ValueError(f"no aligned split of {rows_per_step} fill rows")


def _merge(x2, oa, proj, w_spatial, b_spatial_t, sgu_gain, wa, wb, wo, n_fill_rows):
    t, d = x2.shape
    aw = oa.shape[1]
    sw = SGU_GROUPS * SGU_BLOCK
    ucol = (2 * d + 3 * aw) // sw
    const = lambda shape: pl.BlockSpec(shape, lambda i: (0,) * len(shape), pipeline_mode=pl.Buffered(1))
    n_steps = t // MERGE_TM
    assert n_fill_rows % n_steps == 0
    z_rows = _fill_rows_per_copy(n_fill_rows // n_steps)
    return pl.pallas_call(
        _merge_kernel,
        grid=(t // MERGE_TM,),
        in_specs=[
            pl.BlockSpec((MERGE_TM, d), lambda i: (i, 0)),
            pl.BlockSpec((MERGE_TM, aw), lambda i: (i, 0)),
            pl.BlockSpec((MERGE_TM, sw), lambda i: (i, ucol)),
            pl.BlockSpec((MERGE_TM, sw), lambda i: (i, ucol + 1)),
            pl.BlockSpec((MERGE_TM, d), lambda i: (i, 0)),
            pl.BlockSpec((MERGE_TM, d), lambda i: (i, 1)),
            const((SGU_GROUPS, SGU_BLOCK, SGU_BLOCK)),
            const((SGU_BLOCK, SGU_GROUPS)),
            const((1, sw)),
            const((aw, d)),
            const((sw, d)),
            const((d, d)),
        ],
        out_specs=[pl.BlockSpec((MERGE_TM, d), lambda i: (i, 0)), pl.BlockSpec(memory_space=pl.ANY)],
        out_shape=[jax.ShapeDtypeStruct((t, d), F32),
                   jax.ShapeDtypeStruct((n_fill_rows, LANES), jnp.uint32)],
        scratch_shapes=[pltpu.VMEM((MERGE_TM, sw), BF16), pltpu.VMEM((MERGE_TM, sw), BF16),
                        pltpu.VMEM((z_rows, LANES), jnp.uint32), pltpu.SemaphoreType.DMA(())],
        compiler_params=pltpu.CompilerParams(
            dimension_semantics=("arbitrary",), vmem_limit_bytes=VMEM_LIMIT),
        name="merge",
    )(x2, oa, proj, proj, proj, proj, w_spatial, b_spatial_t, sgu_gain, wa, wb, wo)


def _router_kernel(h_ref, g2_ref, wrt_ref, brt_ref, ut_ref, xp_ref, idx_ref, rank_ref, gate_ref, cnt_ref,
                   carry_ref):
    @pl.when(pl.program_id(0) == 0)
    def _():
        carry_ref[...] = jnp.zeros_like(carry_ref)

    h = h_ref[...]
    half = h.shape[1] // 2
    ms = jnp.mean(h * h, axis=-1, keepdims=True)
    xn = h * lax.rsqrt(ms + EPS) * g2_ref[...]
    xn_hi = xn.astype(BF16)
    xn_lo = (xn - xn_hi.astype(F32)).astype(BF16)
    hi_bits = lax.bitcast_convert_type(xn_hi.astype(F32), jnp.uint32)
    _store_rows_as_tiles(xp_ref, hi_bits[:, half:] | (hi_bits[:, :half] >> 16))

    wr = wrt_ref[...]
    wr_hi = wr.astype(BF16)
    wr_lo = (wr - wr_hi.astype(F32)).astype(BF16)
    nt = (((1,), (1,)), ((), ()))
    logits = (lax.dot_general(wr_hi, xn_hi, nt, preferred_element_type=F32)
              + lax.dot_general(wr_hi, xn_lo, nt, preferred_element_type=F32)
              + lax.dot_general(wr_lo, xn_hi, nt, preferred_element_type=F32)) + brt_ref[...]
    sub = lax.broadcasted_iota(jnp.int32, logits.shape, 0).astype(F32)
    out_row = lax.broadcasted_iota(jnp.int32, idx_ref.shape, 0)
    work = logits
    vals, sels = [], []
    idx_out = jnp.zeros(idx_ref.shape, jnp.int32)
    for k in range(TOP_K):
        m = jnp.max(work, axis=0, keepdims=True)
        idx = jnp.min(jnp.where(work == m, sub, float(N_EXPERTS)), axis=0, keepdims=True)
        sel = sub == idx
        work = jnp.where(sel, -jnp.inf, work)
        vals.append(m)
        sels.append(sel)
        idx_out = jnp.where(out_row == k, idx.astype(jnp.int32), idx_out)
    idx_ref[...] = idx_out

    exps = [jnp.exp(v - vals[0]) for v in vals]
    denom = exps[0] + exps[1] + exps[2] + exps[3]
    gate_out = jnp.zeros(gate_ref.shape, F32)
    for k in range(TOP_K):
        gate_out = jnp.where(out_row == k, exps[k] / denom, gate_out)
    gate_ref[...] = gate_out

    member = (sels[0] | sels[1] | sels[2] | sels[3])
    before = jnp.dot(member.astype(BF16), ut_ref[...], preferred_element_type=F32)
    pos = carry_ref[...] + before
    rank_out = jnp.zeros(rank_ref.shape, jnp.int32)
    for k in range(TOP_K):
        r = jnp.sum(jnp.where(sels[k], pos, 0.0), axis=0, keepdims=True)
        rank_out = jnp.where(out_row == k, r.astype(jnp.int32), rank_out)
    rank_ref[...] = rank_out

    carry_ref[...] = carry_ref[...] + jnp.sum(member.astype(F32), axis=1, keepdims=True)
    cnt_ref[...] = jnp.broadcast_to(carry_ref[...], cnt_ref.shape)


def _router(h1, g2, w_router_t, b_router_t):
    t, d = h1.shape
    const = lambda shape: pl.BlockSpec(shape, lambda i: (0,) * len(shape))
    i = jnp.arange(ROUTER_TM)
    ut = (i[:, None] < i[None, :]).astype(BF16)
    rows, lanes = 8, 128
    tok = lambda: pl.BlockSpec((rows, ROUTER_TM), lambda i: (0, i))
    return pl.pallas_call(
        _router_kernel,
        grid=(t // ROUTER_TM,),
        in_specs=[
            pl.BlockSpec((ROUTER_TM, d), lambda i: (i, 0)),
            const((1, d)),
            const((N_EXPERTS, d)),
            const((N_EXPERTS, 1)),
            const((ROUTER_TM, ROUTER_TM)),
        ],
        out_specs=[
            pl.BlockSpec((ROUTER_TM * ROW_SUBLANES, LANES), lambda i: (i, 0)),
            tok(), tok(), tok(),
            pl.BlockSpec((N_EXPERTS, lanes), lambda i: (0, 0)),
        ],
        out_shape=[
            jax.ShapeDtypeStruct((t * ROW_SUBLANES, LANES), jnp.uint32),
            jax.ShapeDtypeStruct((rows, t), jnp.int32),
            jax.ShapeDtypeStruct((rows, t), jnp.int32),
            jax.ShapeDtypeStruct((rows, t), F32),
            jax.ShapeDtypeStruct((N_EXPERTS, lanes), F32),
        ],
        scratch_shapes=[pltpu.VMEM((N_EXPERTS, 1), F32)],
        compiler_params=pltpu.CompilerParams(
            dimension_semantics=("arbitrary",), vmem_limit_bytes=VMEM_LIMIT),
        name="router",
    )(h1, g2, w_router_t, b_router_t, ut)


def _dispatch_kernel(dest_ref, xp_ref, xs_in_ref, xs_ref, sem):
    del xs_in_ref
    i = pl.program_id(0)

    n_tok = dest_ref.shape[0] // TOP_K

    def row_copy(s, r):
        return pltpu.make_async_copy(_tile_of_row(xp_ref, s), _tile_of_row(xs_ref, r), sem)

    def issue(b, _):
        for j in range(DMA_UNROLL):
            s = b * DMA_UNROLL + j
            for k in range(TOP_K):
                row_copy(s, dest_ref[k * n_tok + i * DISPATCH_TB + s]).start(priority=k % 2)
        return 0

    def drain(b, _):
        for _ in range(DMA_UNROLL * TOP_K):
            row_copy(0, 0).wait()
        return 0

    lax.fori_loop(0, DISPATCH_TB // DMA_UNROLL, issue, 0)
    lax.fori_loop(0, DISPATCH_TB // DMA_UNROLL, drain, 0)


def _dispatch(dest_flat, xp, xs0):
    t = xp.shape[0] // ROW_SUBLANES
    n_rows = xs0.shape[0] // ROW_SUBLANES
    return pl.pallas_call(
        _dispatch_kernel,
        grid_spec=pltpu.PrefetchScalarGridSpec(
            num_scalar_prefetch=1,
            grid=(t // DISPATCH_TB,),
            in_specs=[pl.BlockSpec((DISPATCH_TB * ROW_SUBLANES, LANES), lambda i, dest: (i, 0)),
                      pl.BlockSpec(memory_space=pl.ANY)],
            out_specs=pl.BlockSpec(memory_space=pl.ANY),
            scratch_shapes=[pltpu.SemaphoreType.DMA(())],
        ),
        out_shape=jax.ShapeDtypeStruct((n_rows * ROW_SUBLANES, LANES), jnp.uint32),
        input_output_aliases={2: 0},
        compiler_params=pltpu.CompilerParams(
            dimension_semantics=("arbitrary",), has_side_effects=True),
        name="dispatch",
    )(dest_flat, xp, xs0)


def _expert_pieces(n_rows):
    k = -(-n_rows // EXP_SUB)
    base, extra = divmod(n_rows // EXP_GRAN, k)
    return [(base + (1 if j < extra else 0)) * EXP_GRAN for j in range(k)]


def _expert_kernel(ie_ref, it_ref, nv_ref, xs_ref, wg_ref, wu_ref, wd_ref, bg_ref, bu_ref, bd_ref,
                   y_ref, acc_ref):
    del ie_ref, it_ref
    i = pl.program_id(0)
    c = pl.program_id(1)
    nc = pl.num_programs(1)
    nv = nv_ref[i]
    half = ROW_SUBLANES * LANES

    @pl.when((i == 0) & (c == 0))
    def _():
        acc_ref[...] = jnp.zeros_like(acc_ref)

    def run(n_rows):
        wg = wg_ref[0].astype(BF16)
        wu = wu_ref[0].astype(BF16)
        wd = wd_ref[0].astype(BF16)
        first = c == 0
        lo = 0
        for m in _expert_pieces(n_rows):
            rows = slice(lo, lo + m)
            tiles = lambda s, lo=lo, m=m: pl.ds(lo * ROW_SUBLANES + s, m, stride=ROW_SUBLANES)
            pairs = [_unpack_bf16_pair(xs_ref[tiles(s), :]) for s in range(ROW_SUBLANES)]
            xb = jnp.concatenate([p[0].astype(BF16) for p in pairs] + [p[1].astype(BF16) for p in pairs], axis=1)
            g = jnp.dot(xb, wg, preferred_element_type=F32) + bg_ref[0]
            u = jnp.dot(xb, wu, preferred_element_type=F32) + bu_ref[0]
            g = jnp.minimum(g, SWIGLU_LIMIT)
            u = jnp.clip(u, -SWIGLU_LIMIT, SWIGLU_LIMIT)
            a = g * _sigmoid(SWIGLU_ALPHA * g) * (u + 1.0)
            y = jnp.dot(a.astype(BF16), wd, preferred_element_type=F32)
            acc_ref[rows, :] = jnp.where(first, jnp.broadcast_to(bd_ref[0], y.shape), acc_ref[rows, :]) + y
            lo += m

    for n_rows in range(EXP_GRAN, EXP_TM + 1, EXP_GRAN):
        pl.when((nv > n_rows - EXP_GRAN) & (nv <= n_rows))(lambda n_rows=n_rows: run(n_rows))

    @pl.when((c == nc - 1) & (nv > 0))
    def _():
        for s in range(ROW_SUBLANES):
            lo = acc_ref[:, s * LANES:(s + 1) * LANES]
            hi = acc_ref[:, half + s * LANES:half + (s + 1) * LANES]
            y_ref[pl.ds(s, EXP_TM, stride=ROW_SUBLANES), :] = _pack_bf16_pair(lo, hi)


def _expert(item_e, item_t, item_nv, xs, w_up, b_up, w_down, b_down):
    n_rows = xs.shape[0] // ROW_SUBLANES
    ne, d, two_f = w_up.shape
    assert d == 2 * ROW_SUBLANES * LANES
    tile = (EXP_TM * ROW_SUBLANES, LANES)
    f = two_f // 2
    nc = f // EXP_TC
    n_items = item_e.shape[0]
    b_up3 = b_up.reshape(ne, 1, two_f)
    b_down3 = b_down.reshape(ne, 1, d)

    def chunk(c, nv_ref, i):
        return jnp.where(nv_ref[i] > 0, c, nc - 1)

    return pl.pallas_call(
        _expert_kernel,
        grid_spec=pltpu.PrefetchScalarGridSpec(
            num_scalar_prefetch=3,
            grid=(n_items, nc),
            in_specs=[
                pl.BlockSpec(tile, lambda i, c, ie, it, nv: (it[i], 0)),
                pl.BlockSpec((1, d, EXP_TC), lambda i, c, ie, it, nv: (ie[i], 0, chunk(c, nv, i))),
                pl.BlockSpec((1, d, EXP_TC), lambda i, c, ie, it, nv: (ie[i], 0, nc + chunk(c, nv, i))),
                pl.BlockSpec((1, EXP_TC, d), lambda i, c, ie, it, nv: (ie[i], chunk(c, nv, i), 0)),
                pl.BlockSpec((1, 1, EXP_TC), lambda i, c, ie, it, nv: (ie[i], 0, chunk(c, nv, i))),
                pl.BlockSpec((1, 1, EXP_TC), lambda i, c, ie, it, nv: (ie[i], 0, nc + chunk(c, nv, i))),
                pl.BlockSpec((1, 1, d), lambda i, c, ie, it, nv: (ie[i], 0, 0)),
            ],
            out_specs=pl.BlockSpec(tile, lambda i, c, ie, it, nv: (it[i], 0)),
            scratch_shapes=[pltpu.VMEM((EXP_TM, d), F32)],
        ),
        out_shape=jax.ShapeDtypeStruct((n_rows * ROW_SUBLANES, LANES), jnp.uint32),
        input_output_aliases={3: 0},
        compiler_params=pltpu.CompilerParams(
            dimension_semantics=("arbitrary", "arbitrary"), vmem_limit_bytes=VMEM_LIMIT),
        name="expert",
    )(item_e, item_t, item_nv, xs, w_up, w_up, w_down, b_up3, b_up3, b_down3)


def _combine_kernel(dest_ref, y_ref, h_ref, gate_ref, o_ref, buf_ref, sem):
    i = pl.program_id(0)
    n = pl.num_programs(0)
    n_tok = dest_ref.shape[0] // TOP_K

    def row_copy(r, slot, k, s):
        return pltpu.make_async_copy(_tile_of_row(y_ref, r), _tile_of_row(buf_ref.at[slot, k], s), sem.at[slot])

    def start_rows(step, slot, first, count):
        for j in range(count):
            s = first + j
            for k in range(TOP_K):
                row_copy(dest_ref[k * n_tok + step * COMBINE_TB + s], slot, k, s).start(priority=k % 2)

    def drain(slot):
        def body(b, _):
            for _ in range(DMA_UNROLL * TOP_K):
                row_copy(0, slot, 0, 0).wait()
            return 0
        lax.fori_loop(0, COMBINE_TB // DMA_UNROLL, body, 0)

    @pl.when(i == 0)
    def _():
        lax.fori_loop(0, COMBINE_TB // DMA_UNROLL,
                      lambda b, _: start_rows(0, 0, b * DMA_UNROLL, DMA_UNROLL) or 0, 0)

    nxt = jnp.minimum(i + 1, n - 1)
    per_part = COMBINE_TB // ROW_SUBLANES
    for slot in range(2):
        @pl.when(i % 2 == slot)
        def _():
            drain(slot)
            half = ROW_SUBLANES * LANES
            gates = [jnp.broadcast_to(gate_ref[:, k:k + 1], (COMBINE_TB, LANES)) for k in range(TOP_K)]
            for s in range(ROW_SUBLANES):
                lo_cols = slice(s * LANES, (s + 1) * LANES)
                hi_cols = slice(half + s * LANES, half + (s + 1) * LANES)
                acc_lo = h_ref[:, lo_cols]
                acc_hi = h_ref[:, hi_cols]
                words = [_load_tile_rows(buf_ref.at[slot, k], s, COMBINE_TB) for k in range(TOP_K)]
                start_rows(nxt, 1 - slot, s * per_part, per_part)
                for k in range(TOP_K):
                    lo, hi = _unpack_bf16_pair(words[k])
                    acc_lo = acc_lo + gates[k] * lo
                    acc_hi = acc_hi + gates[k] * hi
                o_ref[:, lo_cols] = acc_lo
                o_ref[:, hi_cols] = acc_hi

        @pl.when((i == n - 1) & (i % 2 == slot))
        def _():
            drain(1 - slot)


def _combine(dest_flat, y, h1, gate):
    t, d = h1.shape
    lanes = gate.shape[1]
    return pl.pallas_call(
        _combine_kernel,
        grid_spec=pltpu.PrefetchScalarGridSpec(
            num_scalar_prefetch=1,
            grid=(t // COMBINE_TB,),
            in_specs=[
                pl.BlockSpec(memory_space=pl.ANY),
                pl.BlockSpec((COMBINE_TB, d), lambda i, dest: (i, 0)),
                pl.BlockSpec((COMBINE_TB, lanes), lambda i, dest: (i, 0)),
            ],
            out_specs=pl.BlockSpec((COMBINE_TB, d), lambda i, dest: (i, 0)),
            scratch_shapes=[pltpu.VMEM((2, TOP_K, COMBINE_TB * ROW_SUBLANES, LANES), jnp.uint32),
                            pltpu.SemaphoreType.DMA((2,))],
        ),
        out_shape=jax.ShapeDtypeStruct((t, d), F32),
        compiler_params=pltpu.CompilerParams(
            dimension_semantics=("arbitrary",), vmem_limit_bytes=VMEM_LIMIT),
        name="combine",
    )(dest_flat, y, h1, gate)


def _plan_items(counts, n_items_max):
    tiles = (counts + EXP_TM - 1) // EXP_TM
    tile_end = jnp.cumsum(tiles)
    tile_start = tile_end - tiles
    n_items = tile_end[-1]
    g = jnp.arange(n_items_max, dtype=jnp.int32)
    live = g < n_items
    gg = jnp.maximum(jnp.minimum(g, n_items - 1), 0)
    e = jnp.minimum(jnp.searchsorted(tile_end, gg, side="right"), N_EXPERTS - 1).astype(jnp.int32)
    nv = jnp.clip(counts[e] - (gg - tile_start[e]) * EXP_TM, 0, EXP_TM)
    nv = jnp.where(live, nv, 0).astype(jnp.int32)
    return tile_start * EXP_TM, e, gg.astype(jnp.int32), nv


def _layer(x2, bsz, seq, norm1_gain, w_in, q_norm_gain, k_norm_gain, sgu_norm_gain, w_spatial, b_spatial,
           w_branch_a, w_branch_b, w_out, norm2_gain, w_router, b_router, w_up, b_up, w_down, b_down):
    t, d = x2.shape
    proj = _proj(x2, norm1_gain[None, :], w_in, q_norm_gain[None, :], k_norm_gain[None, :])
    oa = _attn(proj, bsz, seq, d, q_norm_gain, k_norm_gain)
    n_items_max = -(-(t * TOP_K) // EXP_TM) + N_EXPERTS
    h1, xs0 = _merge(x2, oa, proj, w_spatial, b_spatial.T, sgu_norm_gain[None, :],
                     w_branch_a.astype(BF16), w_branch_b.astype(BF16), w_out.astype(BF16),
                     n_items_max * EXP_TM * ROW_SUBLANES)
    xp, idx, rank, gate, cnt = _router(h1, norm2_gain[None, :], w_router.T, b_router[:, None])

    counts = cnt[:, 0].astype(jnp.int32)
    seg_start, item_e, item_t, item_nv = _plan_items(counts, n_items_max)
    onehot = idx[:TOP_K, :, None] == jnp.arange(N_EXPERTS, dtype=jnp.int32)
    dest = (jnp.sum(jnp.where(onehot, seg_start.astype(jnp.int32), 0), axis=-1)
            + rank[:TOP_K]).reshape(-1)

    xs = _dispatch(dest, xp, xs0)
    y = _expert(item_e, item_t, item_nv, xs, w_up, b_up, w_down, b_down)
    return _combine(dest, y, h1, gate[:TOP_K].T)


def kernel(x, norm1_gain, w_in, q_norm_gain, k_norm_gain, sgu_norm_gain, w_spatial, b_spatial, w_branch_a,
           w_branch_b, w_out, norm2_gain, w_router, b_router, w_up, b_up, w_down, b_down):
    bsz, seq, d = x.shape
    h = x.reshape(bsz * seq, d)
    for l in range(norm1_gain.shape[0]):
        h = _layer(h, bsz, seq, norm1_gain[l], w_in[l], q_norm_gain[l], k_norm_gain[l], sgu_norm_gain[l],
                   w_spatial[l], b_spatial[l], w_branch_a[l], w_branch_b[l], w_out[l], norm2_gain[l],
                   w_router[l], b_router[l], w_up[l], b_up[l], w_down[l], b_down[l])
    return h.reshape(bsz, seq, d)
```

```python
import math

import jax
import jax.numpy as jnp
from jax import lax
from jax.experimental import pallas as pl
from jax.experimental.pallas import tpu as pltpu

F32 = jnp.float32
BF16 = jnp.bfloat16

EPS = 1e-6
N_HEADS = 8
HEAD_DIM = 128
SGU_GROUPS = 8
SGU_BLOCK = 128
CHUNK = 64
N_EXPERTS = 32
TOP_K = 4
SWIGLU_LIMIT = 7.0
SWIGLU_ALPHA = 1.702

F32_EXP_ZERO_BELOW = -105.0
LOG2_E = 1.4426950408889634

VMEM_LIMIT = 56 * 1024 * 1024
PROJ_TM = 1024
PROJ_TN = 1024
PROJ_ROWS = 512
ATT_T = 256
ATT_HP = 8
MERGE_TM = 256
ROUTER_TM = 1024
EXP_TM = 1408
EXP_SUB = 1408
EXP_MIN_ROWS = 512
EXP_GRAN = 128
EXP_TC = 256
DISPATCH_TB = 512
COMBINE_TB = 128
DMA_UNROLL = 8
FILL_MAX_ROWS = 4096


def _sigmoid(x):
    return 1.0 / (1.0 + jnp.exp(-x))


ROW_SUBLANES = 8
LANES = 128


def _store_rows_as_tiles(ref, rows2d):
    n = rows2d.shape[0]
    for s in range(ROW_SUBLANES):
        ref[pl.ds(s, n, stride=ROW_SUBLANES), :] = rows2d[:, s * LANES:(s + 1) * LANES]


def _tile_of_row(ref, r):
    start = r * ROW_SUBLANES
    if not isinstance(r, int):
        start = pl.multiple_of(start, ROW_SUBLANES)
    return ref.at[pl.ds(start, ROW_SUBLANES)]


def _load_tile_rows(ref, s, n):
    return ref[pl.ds(s, n, stride=ROW_SUBLANES), :]


def _pack_bf16_pair(lo, hi):
    lo_bits = lax.bitcast_convert_type(lo.astype(BF16).astype(F32), jnp.uint32)
    hi_bits = lax.bitcast_convert_type(hi.astype(BF16).astype(F32), jnp.uint32)
    return hi_bits | (lo_bits >> 16)


def _unpack_bf16_pair(w):
    lo = lax.bitcast_convert_type(w << 16, F32)
    hi = lax.bitcast_convert_type(w & jnp.uint32(0xFFFF0000), F32)
    return lo, hi


def _gelu_tanh(x):
    c = math.sqrt(2.0 / math.pi)
    return 0.5 * x * (1.0 + jnp.tanh(c * (x + 0.044715 * (x * x * x))))


def _proj_kernel(x_ref, g1_ref, w_ref, qg_ref, kg_ref, o_ref, xn_ref):
    j = pl.program_id(1)
    n_q = (N_HEADS * HEAD_DIM) // PROJ_TN

    @pl.when(j == 0)
    def _():
        x = x_ref[...]
        ms = jnp.mean(x * x, axis=-1, keepdims=True)
        xn_ref[...] = (x * lax.rsqrt(ms + EPS) * g1_ref[...]).astype(BF16)

    def head_norm(a):
        gain = jnp.where(j < n_q, qg_ref[...], kg_ref[...])
        outs = []
        for h in range(PROJ_TN // HEAD_DIM):
            ah = a[:, h * HEAD_DIM:(h + 1) * HEAD_DIM]
            ms = jnp.mean(ah * ah, axis=-1, keepdims=True)
            outs.append(ah * lax.rsqrt(ms + EPS) * gain)
        return jnp.concatenate(outs, axis=1)

    def tile(epilogue):
        w = w_ref[...].astype(BF16)
        for r in range(PROJ_TM // PROJ_ROWS):
            rows = slice(r * PROJ_ROWS, (r + 1) * PROJ_ROWS)
            acc = jnp.dot(xn_ref[rows, :], w, preferred_element_type=F32)
            o_ref[rows, :] = epilogue(acc).astype(BF16)

    pl.when(j < 2 * n_q)(lambda: tile(head_norm))
    pl.when((j >= 2 * n_q) & (j < 3 * n_q))(lambda: tile(lambda a: a))
    pl.when((j >= 3 * n_q) & (j < 5 * n_q))(lambda: tile(_gelu_tanh))
    pl.when(j >= 5 * n_q)(lambda: tile(_sigmoid))


def _proj(x2, g1, w_in, qg, kg):
    t, d = x2.shape
    n = w_in.shape[1]
    nj = n // PROJ_TN
    shift = (2 * d) // PROJ_TN
    return pl.pallas_call(
        _proj_kernel,
        grid=(t // PROJ_TM, nj),
        in_specs=[
            pl.BlockSpec((PROJ_TM, d), lambda i, j: (i, 0)),
            pl.BlockSpec((1, d), lambda i, j: (0, 0)),
            pl.BlockSpec((d, PROJ_TN), lambda i, j: (0, j)),
            pl.BlockSpec((1, HEAD_DIM), lambda i, j: (0, 0)),
            pl.BlockSpec((1, HEAD_DIM), lambda i, j: (0, 0)),
        ],
        out_specs=pl.BlockSpec((PROJ_TM, PROJ_TN), lambda i, j: (i, (j + shift) % nj)),
        out_shape=jax.ShapeDtypeStruct((t, n), BF16),
        scratch_shapes=[pltpu.VMEM((PROJ_TM, d), BF16)],
        compiler_params=pltpu.CompilerParams(
            dimension_semantics=("arbitrary", "arbitrary"), vmem_limit_bytes=VMEM_LIMIT),
        name="proj",
    )(x2, g1, w_in, qg, kg)


def _attn_kernel(stop_ref, q_ref, k_ref, v_ref, u_ref, o_ref):
    qi = pl.program_id(2)
    scale = HEAD_DIM ** -0.5 * LOG2_E
    stop_at = stop_ref[0]

    def block(j, accs, runs, diag):
        start = pl.multiple_of(j * ATT_T, ATT_T)
        tri = u_ref[...]
        if diag:
            row = lax.broadcasted_iota(jnp.int32, (ATT_T, ATT_T), 0)
            col = lax.broadcasted_iota(jnp.int32, (ATT_T, ATT_T), 1)
            past = col < row
        heads = range(ATT_HP)
        cols = [slice(h * HEAD_DIM, (h + 1) * HEAD_DIM) for h in heads]
        zs = [lax.dot_general(q_ref[:, cols[h]], k_ref[pl.ds(start, ATT_T), cols[h]],
                              (((1,), (1,)), ((), ())), preferred_element_type=F32) * scale for h in heads]
        sps = [jnp.maximum(z, 0.0) + jnp.log(1.0 + jnp.exp2(-jnp.abs(z))) * LOG2_E for z in zs]
        if diag:
            sps = [jnp.where(past, sp, 0.0) for sp in sps]
        his = [sp.astype(BF16) for sp in sps]
        los = [(sp - hi.astype(F32)).astype(BF16) for sp, hi in zip(sps, his)]
        css = [jnp.dot(hi, tri, preferred_element_type=F32) + jnp.dot(lo, tri, preferred_element_type=F32)
               for hi, lo in zip(his, los)]
        ws = [jnp.exp2(zs[h] - (runs[h] + css[h])) for h in heads]
        if diag:
            ws = [jnp.where(past, w, 0.0) for w in ws]
        new_accs = [accs[h] + jnp.dot(ws[h].astype(BF16), v_ref[pl.ds(start, ATT_T), cols[h]],
                                      preferred_element_type=F32) for h in heads]
        new_runs = [runs[h] + css[h][:, 0:1] for h in heads]
        return tuple(new_accs), tuple(new_runs)

    def smallest(runs):
        m = runs[0]
        for r in runs[1:]:
            m = jnp.minimum(m, r)
        return jnp.min(m)

    accs = tuple(jnp.zeros((ATT_T, HEAD_DIM), F32) for _ in range(ATT_HP))
    runs = tuple(jnp.zeros((ATT_T, 1), F32) for _ in range(ATT_HP))
    accs, runs = block(qi, accs, runs, True)

    def cond(c):
        j, _, _, low = c
        return (j >= 0) & (low < stop_at)

    def body(c):
        j, accs, runs, _ = c
        accs, runs = block(j, accs, runs, False)
        return j - 1, accs, runs, smallest(runs)

    _, accs, _, _ = lax.while_loop(cond, body, (qi - 1, accs, runs, smallest(runs)))
    for h in range(ATT_HP):
        o_ref[:, h * HEAD_DIM:(h + 1) * HEAD_DIM] = accs[h].astype(BF16)


def _attn(proj, bsz, seq, d, q_gain, k_gain):
    t = bsz * seq
    nq = seq // ATT_T
    width = ATT_HP * HEAD_DIM
    col0 = (2 * d) // width
    seg = (N_HEADS * HEAD_DIM) // width
    i = jnp.arange(ATT_T)
    tri = (i[:, None] >= i[None, :]).astype(BF16)
    zmax = 1.02 * math.sqrt(HEAD_DIM) * jnp.max(jnp.abs(q_gain)) * jnp.max(jnp.abs(k_gain))
    stop_at = ((zmax - F32_EXP_ZERO_BELOW) * LOG2_E).reshape(1).astype(F32)
    return pl.pallas_call(
        _attn_kernel,
        grid_spec=pltpu.PrefetchScalarGridSpec(
            num_scalar_prefetch=1,
            grid=(bsz, seg, nq),
            in_specs=[
                pl.BlockSpec((ATT_T, width), lambda b, h, qi, s: (b * nq + qi, col0 + h)),
                pl.BlockSpec((seq, width), lambda b, h, qi, s: (b, col0 + seg + h)),
                pl.BlockSpec((seq, width), lambda b, h, qi, s: (b, col0 + 2 * seg + h)),
                pl.BlockSpec((ATT_T, ATT_T), lambda b, h, qi, s: (0, 0)),
            ],
            out_specs=pl.BlockSpec((ATT_T, width), lambda b, h, qi, s: (b * nq + qi, h)),
        ),
        out_shape=jax.ShapeDtypeStruct((t, N_HEADS * HEAD_DIM), BF16),
        compiler_params=pltpu.CompilerParams(
            dimension_semantics=("arbitrary", "arbitrary", "arbitrary"), vmem_limit_bytes=VMEM_LIMIT),
        name="attn",
    )(stop_at, proj, proj, proj, tri)


def _merge_kernel(x_ref, oa_ref, u_ref, vv_ref, ga_ref, gb_ref, ws_ref, bst_ref, sg_ref,
                  wa_ref, wb_ref, wo_ref, h_ref, fill_ref, vln_ref, ob_ref, zero_ref, fill_sem):
    i = pl.program_id(0)
    z_rows = zero_ref.shape[0]
    n_fill = fill_ref.shape[0] // (pl.num_programs(0) * z_rows)

    @pl.when(i == 0)
    def _():
        zero_ref[...] = jnp.zeros_like(zero_ref)

    def fill_copy(j):
        start = pl.multiple_of((i * n_fill + j) * z_rows, ROW_SUBLANES)
        return pltpu.make_async_copy(zero_ref, fill_ref.at[pl.ds(start, z_rows)], fill_sem)

    for j in range(n_fill):
        fill_copy(j).start()

    vv = vv_ref[...].astype(F32)
    mu = jnp.mean(vv, axis=-1, keepdims=True)
    xc = vv - mu
    var = jnp.mean(xc * xc, axis=-1, keepdims=True)
    vln_ref[...] = (xc * lax.rsqrt(var + EPS) * sg_ref[...]).astype(BF16)

    row = lax.broadcasted_iota(jnp.int32, (SGU_BLOCK, SGU_BLOCK), 0)
    col = lax.broadcasted_iota(jnp.int32, (SGU_BLOCK, SGU_BLOCK), 1)
    mask = (col // CHUNK) <= (row // CHUNK)
    for g in range(SGU_GROUPS):
        wg = jnp.where(mask, ws_ref[g], 0.0).astype(BF16)
        bg = bst_ref[:, g:g + 1]
        cs = slice(g * SGU_BLOCK, (g + 1) * SGU_BLOCK)
        blocks = [slice(n * SGU_BLOCK, (n + 1) * SGU_BLOCK) for n in range(MERGE_TM // SGU_BLOCK)]
        sgu = jnp.dot(wg, jnp.concatenate([vln_ref[rs, cs] for rs in blocks], axis=1),
                      preferred_element_type=F32) + bg
        for n, rs in enumerate(blocks):
            ob_ref[rs, cs] = (u_ref[rs, cs].astype(F32) * sgu[:, n * SGU_BLOCK:(n + 1) * SGU_BLOCK]).astype(BF16)

    ya = jnp.dot(oa_ref[...], wa_ref[...], preferred_element_type=F32)
    yb = jnp.dot(ob_ref[...], wb_ref[...], preferred_element_type=F32)
    y = ga_ref[...].astype(F32) * ya + gb_ref[...].astype(F32) * yb
    h_ref[...] = x_ref[...] + jnp.dot(y.astype(BF16), wo_ref[...], preferred_element_type=F32)

    for j in range(n_fill):
        fill_copy(j).wait()


def _fill_rows_per_copy(rows_per_step):
    for n in range(1, rows_per_step + 1):
        if rows_per_step % n == 0 and (rows_per_step // n) % ROW_SUBLANES == 0 and rows_per_step // n <= FILL_MAX_ROWS:
            return rows_per_step // n
    raise ValueError(f"no aligned split of {rows_per_step} fill rows")


def _merge(x2, oa, proj, w_spatial, b_spatial_t, sgu_gain, wa, wb, wo, n_fill_rows):
    t, d = x2.shape
    aw = oa.shape[1]
    sw = SGU_GROUPS * SGU_BLOCK
    ucol = (2 * d + 3 * aw) // sw
    const = lambda shape: pl.BlockSpec(shape, lambda i: (0,) * len(shape), pipeline_mode=pl.Buffered(1))
    n_steps = t // MERGE_TM
    assert n_fill_rows % n_steps == 0
    z_rows = _fill_rows_per_copy(n_fill_rows // n_steps)
    return pl.pallas_call(
        _merge_kernel,
        grid=(t // MERGE_TM,),
        in_specs=[
            pl.BlockSpec((MERGE_TM, d), lambda i: (i, 0)),
            pl.BlockSpec((MERGE_TM, aw), lambda i: (i, 0)),
            pl.BlockSpec((MERGE_TM, sw), lambda i: (i, ucol)),
            pl.BlockSpec((MERGE_TM, sw), lambda i: (i, ucol + 1)),
            pl.BlockSpec((MERGE_TM, d), lambda i: (i, 0)),
            pl.BlockSpec((MERGE_TM, d), lambda i: (i, 1)),
            const((SGU_GROUPS, SGU_BLOCK, SGU_BLOCK)),
            const((SGU_BLOCK, SGU_GROUPS)),
            const((1, sw)),
            const((aw, d)),
            const((sw, d)),
            const((d, d)),
        ],
        out_specs=[pl.BlockSpec((MERGE_TM, d), lambda i: (i, 0)), pl.BlockSpec(memory_space=pl.ANY)],
        out_shape=[jax.ShapeDtypeStruct((t, d), F32),
                   jax.ShapeDtypeStruct((n_fill_rows, LANES), jnp.uint32)],
        scratch_shapes=[pltpu.VMEM((MERGE_TM, sw), BF16), pltpu.VMEM((MERGE_TM, sw), BF16),
                        pltpu.VMEM((z_rows, LANES), jnp.uint32), pltpu.SemaphoreType.DMA(())],
        compiler_params=pltpu.CompilerParams(
            dimension_semantics=("arbitrary",), vmem_limit_bytes=VMEM_LIMIT),
        name="merge",
    )(x2, oa, proj, proj, proj, proj, w_spatial, b_spatial_t, sgu_gain, wa, wb, wo)


def _router_kernel(h_ref, g2_ref, wrt_ref, brt_ref, ut_ref, xp_ref, idx_ref, rank_ref, gate_ref, cnt_ref,
                   carry_ref):
    @pl.when(pl.program_id(0) == 0)
    def _():
        carry_ref[...] = jnp.zeros_like(carry_ref)

    h = h_ref[...]
    half = h.shape[1] // 2
    ms = jnp.mean(h * h, axis=-1, keepdims=True)
    xn = h * lax.rsqrt(ms + EPS) * g2_ref[...]
    xn_hi = xn.astype(BF16)
    xn_lo = (xn - xn_hi.astype(F32)).astype(BF16)
    hi_bits = lax.bitcast_convert_type(xn_hi.astype(F32), jnp.uint32)
    _store_rows_as_tiles(xp_ref, hi_bits[:, half:] | (hi_bits[:, :half] >> 16))

    wr = wrt_ref[...]
    wr_hi = wr.astype(BF16)
    wr_lo = (wr - wr_hi.astype(F32)).astype(BF16)
    nt = (((1,), (1,)), ((), ()))
    logits = (lax.dot_general(wr_hi, xn_hi, nt, preferred_element_type=F32)
              + lax.dot_general(wr_hi, xn_lo, nt, preferred_element_type=F32)
              + lax.dot_general(wr_lo, xn_hi, nt, preferred_element_type=F32)) + brt_ref[...]
    sub = lax.broadcasted_iota(jnp.int32, logits.shape, 0).astype(F32)
    out_row = lax.broadcasted_iota(jnp.int32, idx_ref.shape, 0)
    work = logits
    vals, sels = [], []
    idx_out = jnp.zeros(idx_ref.shape, jnp.int32)
    for k in range(TOP_K):
        m = jnp.max(work, axis=0, keepdims=True)
        idx = jnp.min(jnp.where(work == m, sub, float(N_EXPERTS)), axis=0, keepdims=True)
        sel = sub == idx
        work = jnp.where(sel, -jnp.inf, work)
        vals.append(m)
        sels.append(sel)
        idx_out = jnp.where(out_row == k, idx.astype(jnp.int32), idx_out)
    idx_ref[...] = idx_out

    exps = [jnp.exp(v - vals[0]) for v in vals]
    denom = exps[0] + exps[1] + exps[2] + exps[3]
    gate_out = jnp.zeros(gate_ref.shape, F32)
    for k in range(TOP_K):
        gate_out = jnp.where(out_row == k, exps[k] / denom, gate_out)
    gate_ref[...] = gate_out

    member = (sels[0] | sels[1] | sels[2] | sels[3])
    before = jnp.dot(member.astype(BF16), ut_ref[...], preferred_element_type=F32)
    pos = carry_ref[...] + before
    rank_out = jnp.zeros(rank_ref.shape, jnp.int32)
    for k in range(TOP_K):
        r = jnp.sum(jnp.where(sels[k], pos, 0.0), axis=0, keepdims=True)
        rank_out = jnp.where(out_row == k, r.astype(jnp.int32), rank_out)
    rank_ref[...] = rank_out

    carry_ref[...] = carry_ref[...] + jnp.sum(member.astype(F32), axis=1, keepdims=True)
    cnt_ref[...] = jnp.broadcast_to(carry_ref[...], cnt_ref.shape)


def _router(h1, g2, w_router_t, b_router_t):
    t, d = h1.shape
    const = lambda shape: pl.BlockSpec(shape, lambda i: (0,) * len(shape))
    i = jnp.arange(ROUTER_TM)
    ut = (i[:, None] < i[None, :]).astype(BF16)
    rows, lanes = 8, 128
    tok = lambda: pl.BlockSpec((rows, ROUTER_TM), lambda i: (0, i))
    return pl.pallas_call(
        _router_kernel,
        grid=(t // ROUTER_TM,),
        in_specs=[
            pl.BlockSpec((ROUTER_TM, d), lambda i: (i, 0)),
            const((1, d)),
            const((N_EXPERTS, d)),
            const((N_EXPERTS, 1)),
            const((ROUTER_TM, ROUTER_TM)),
        ],
        out_specs=[
            pl.BlockSpec((ROUTER_TM * ROW_SUBLANES, LANES), lambda i: (i, 0)),
            tok(), tok(), tok(),
            pl.BlockSpec((N_EXPERTS, lanes), lambda i: (0, 0)),
        ],
        out_shape=[
            jax.ShapeDtypeStruct((t * ROW_SUBLANES, LANES), jnp.uint32),
            jax.ShapeDtypeStruct((rows, t), jnp.int32),
            jax.ShapeDtypeStruct((rows, t), jnp.int32),
            jax.ShapeDtypeStruct((rows, t), F32),
            jax.ShapeDtypeStruct((N_EXPERTS, lanes), F32),
        ],
        scratch_shapes=[pltpu.VMEM((N_EXPERTS, 1), F32)],
        compiler_params=pltpu.CompilerParams(
            dimension_semantics=("arbitrary",), vmem_limit_bytes=VMEM_LIMIT),
        name="router",
    )(h1, g2, w_router_t, b_router_t, ut)


def _dispatch_kernel(dest_ref, xp_ref, xs_in_ref, xs_ref, sem):
    del xs_in_ref
    i = pl.program_id(0)

    n_tok = dest_ref.shape[0] // TOP_K

    def row_copy(s, r):
        return pltpu.make_async_copy(_tile_of_row(xp_ref, s), _tile_of_row(xs_ref, r), sem)

    def issue(b, _):
        for j in range(DMA_UNROLL):
            s = b * DMA_UNROLL + j
            for k in range(TOP_K):
                row_copy(s, dest_ref[k * n_tok + i * DISPATCH_TB + s]).start(priority=k % 2)
        return 0

    def drain(b, _):
        for _ in range(DMA_UNROLL * TOP_K):
            row_copy(0, 0).wait()
        return 0

    lax.fori_loop(0, DISPATCH_TB // DMA_UNROLL, issue, 0)
    lax.fori_loop(0, DISPATCH_TB // DMA_UNROLL, drain, 0)


def _dispatch(dest_flat, xp, xs0):
    t = xp.shape[0] // ROW_SUBLANES
    n_rows = xs0.shape[0] // ROW_SUBLANES
    return pl.pallas_call(
        _dispatch_kernel,
        grid_spec=pltpu.PrefetchScalarGridSpec(
            num_scalar_prefetch=1,
            grid=(t // DISPATCH_TB,),
            in_specs=[pl.BlockSpec((DISPATCH_TB * ROW_SUBLANES, LANES), lambda i, dest: (i, 0)),
                      pl.BlockSpec(memory_space=pl.ANY)],
            out_specs=pl.BlockSpec(memory_space=pl.ANY),
            scratch_shapes=[pltpu.SemaphoreType.DMA(())],
        ),
        out_shape=jax.ShapeDtypeStruct((n_rows * ROW_SUBLANES, LANES), jnp.uint32),
        input_output_aliases={2: 0},
        compiler_params=pltpu.CompilerParams(
            dimension_semantics=("arbitrary",), has_side_effects=True),
        name="dispatch",
    )(dest_flat, xp, xs0)


def _expert_pieces(n_rows):
    k = -(-n_rows // EXP_SUB)
    base, extra = divmod(n_rows // EXP_GRAN, k)
    return [(base + (1 if j < extra else 0)) * EXP_GRAN for j in range(k)]


def _expert_kernel(ie_ref, it_ref, nv_ref, xs_ref, wg_ref, wu_ref, wd_ref, bg_ref, bu_ref, bd_ref,
                   y_ref, acc_ref):
    del ie_ref, it_ref
    i = pl.program_id(0)
    c = pl.program_id(1)
    nc = pl.num_programs(1)
    nv = nv_ref[i]
    half = ROW_SUBLANES * LANES

    @pl.when((i == 0) & (c == 0))
    def _():
        acc_ref[...] = jnp.zeros_like(acc_ref)

    def run(n_rows):
        wg = wg_ref[0].astype(BF16)
        wu = wu_ref[0].astype(BF16)
        wd = wd_ref[0].astype(BF16)
        first = c == 0
        lo = 0
        for m in _expert_pieces(n_rows):
            rows = slice(lo, lo + m)
            tiles = lambda s, lo=lo, m=m: pl.ds(lo * ROW_SUBLANES + s, m, stride=ROW_SUBLANES)
            pairs = [_unpack_bf16_pair(xs_ref[tiles(s), :]) for s in range(ROW_SUBLANES)]
            xb = jnp.concatenate([p[0].astype(BF16) for p in pairs] + [p[1].astype(BF16) for p in pairs], axis=1)
            g = jnp.dot(xb, wg, preferred_element_type=F32) + bg_ref[0]
            u = jnp.dot(xb, wu, preferred_element_type=F32) + bu_ref[0]
            g = jnp.minimum(g, SWIGLU_LIMIT)
            u = jnp.clip(u, -SWIGLU_LIMIT, SWIGLU_LIMIT)
            a = g * _sigmoid(SWIGLU_ALPHA * g) * (u + 1.0)
            y = jnp.dot(a.astype(BF16), wd, preferred_element_type=F32)
            acc_ref[rows, :] = jnp.where(first, jnp.broadcast_to(bd_ref[0], y.shape), acc_ref[rows, :]) + y
            lo += m

    for n_rows in range(EXP_MIN_ROWS, EXP_TM + 1, EXP_GRAN):
        above = 0 if n_rows == EXP_MIN_ROWS else n_rows - EXP_GRAN
        pl.when((nv > above) & (nv <= n_rows))(lambda n_rows=n_rows: run(n_rows))

    @pl.when((c == nc - 1) & (nv > 0))
    def _():
        for s in range(ROW_SUBLANES):
            lo = acc_ref[:, s * LANES:(s + 1) * LANES]
            hi = acc_ref[:, half + s * LANES:half + (s + 1) * LANES]
            y_ref[pl.ds(s, EXP_TM, stride=ROW_SUBLANES), :] = _pack_bf16_pair(lo, hi)


def _expert(item_e, item_t, item_nv, xs, w_up, b_up, w_down, b_down):
    n_rows = xs.shape[0] // ROW_SUBLANES
    ne, d, two_f = w_up.shape
    assert d == 2 * ROW_SUBLANES * LANES
    tile = (EXP_TM * ROW_SUBLANES, LANES)
    f = two_f // 2
    nc = f // EXP_TC
    n_items = item_e.shape[0]
    b_up3 = b_up.reshape(ne, 1, two_f)
    b_down3 = b_down.reshape(ne, 1, d)

    def chunk(c, nv_ref, i):
        return jnp.where(nv_ref[i] > 0, c, nc - 1)

    return pl.pallas_call(
        _expert_kernel,
        grid_spec=pltpu.PrefetchScalarGridSpec(
            num_scalar_prefetch=3,
            grid=(n_items, nc),
            in_specs=[
                pl.BlockSpec(tile, lambda i, c, ie, it, nv: (it[i], 0)),
                pl.BlockSpec((1, d, EXP_TC), lambda i, c, ie, it, nv: (ie[i], 0, chunk(c, nv, i))),
                pl.BlockSpec((1, d, EXP_TC), lambda i, c, ie, it, nv: (ie[i], 0, nc + chunk(c, nv, i))),
                pl.BlockSpec((1, EXP_TC, d), lambda i, c, ie, it, nv: (ie[i], chunk(c, nv, i), 0)),
                pl.BlockSpec((1, 1, EXP_TC), lambda i, c, ie, it, nv: (ie[i], 0, chunk(c, nv, i))),
                pl.BlockSpec((1, 1, EXP_TC), lambda i, c, ie, it, nv: (ie[i], 0, nc + chunk(c, nv, i))),
                pl.BlockSpec((1, 1, d), lambda i, c, ie, it, nv: (ie[i], 0, 0)),
            ],
            out_specs=pl.BlockSpec(tile, lambda i, c, ie, it, nv: (it[i], 0)),
            scratch_shapes=[pltpu.VMEM((EXP_TM, d), F32)],
        ),
        out_shape=jax.ShapeDtypeStruct((n_rows * ROW_SUBLANES, LANES), jnp.uint32),
        input_output_aliases={3: 0},
        compiler_params=pltpu.CompilerParams(
            dimension_semantics=("arbitrary", "arbitrary"), vmem_limit_bytes=VMEM_LIMIT),
        name="expert",
    )(item_e, item_t, item_nv, xs, w_up, w_up, w_down, b_up3, b_up3, b_down3)


def _combine_kernel(dest_ref, y_ref, h_ref, gate_ref, o_ref, buf_ref, sem):
    i = pl.program_id(0)
    n = pl.num_programs(0)
    n_tok = dest_ref.shape[0] // TOP_K

    def row_copy(r, slot, k, s):
        return pltpu.make_async_copy(_tile_of_row(y_ref, r), _tile_of_row(buf_ref.at[slot, k], s), sem.at[slot])

    def gather(step, slot):
        def issue(b, _):
            for j in range(DMA_UNROLL):
                s = b * DMA_UNROLL + j
                for k in range(TOP_K):
                    row_copy(dest_ref[k * n_tok + step * COMBINE_TB + s], slot, k, s).start(priority=k % 2)
            return 0
        lax.fori_loop(0, COMBINE_TB // DMA_UNROLL, issue, 0)

    @pl.when(i == 0)
    def _():
        gather(0, 0)

    for slot in range(2):
        @pl.when((i + 1 < n) & ((i + 1) % 2 == slot))
        def _():
            gather(i + 1, slot)

    for slot in range(2):
        @pl.when(i % 2 == slot)
        def _():
            def drain(b, _):
                for _ in range(DMA_UNROLL * TOP_K):
                    row_copy(0, slot, 0, 0).wait()
                return 0
            lax.fori_loop(0, COMBINE_TB // DMA_UNROLL, drain, 0)
            half = ROW_SUBLANES * LANES
            gates = [jnp.broadcast_to(gate_ref[:, k:k + 1], (COMBINE_TB, LANES)) for k in range(TOP_K)]
            for s in range(ROW_SUBLANES):
                lo_cols = slice(s * LANES, (s + 1) * LANES)
                hi_cols = slice(half + s * LANES, half + (s + 1) * LANES)
                acc_lo = h_ref[:, lo_cols]
                acc_hi = h_ref[:, hi_cols]
                for k in range(TOP_K):
                    lo, hi = _unpack_bf16_pair(_load_tile_rows(buf_ref.at[slot, k], s, COMBINE_TB))
                    acc_lo = acc_lo + gates[k] * lo
                    acc_hi = acc_hi + gates[k] * hi
                o_ref[:, lo_cols] = acc_lo
                o_ref[:, hi_cols] = acc_hi


def _combine(dest_flat, y, h1, gate):
    t, d = h1.shape
    lanes = gate.shape[1]
    return pl.pallas_call(
        _combine_kernel,
        grid_spec=pltpu.PrefetchScalarGridSpec(
            num_scalar_prefetch=1,
            grid=(t // COMBINE_TB,),
            in_specs=[
                pl.BlockSpec(memory_space=pl.ANY),
                pl.BlockSpec((COMBINE_TB, d), lambda i, dest: (i, 0)),
                pl.BlockSpec((COMBINE_TB, lanes), lambda i, dest: (i, 0)),
            ],
            out_specs=pl.BlockSpec((COMBINE_TB, d), lambda i, dest: (i, 0)),
            scratch_shapes=[pltpu.VMEM((2, TOP_K, COMBINE_TB * ROW_SUBLANES, LANES), jnp.uint32),
                            pltpu.SemaphoreType.DMA((2,))],
        ),
        out_shape=jax.ShapeDtypeStruct((t, d), F32),
        compiler_params=pltpu.CompilerParams(
            dimension_semantics=("arbitrary",), vmem_limit_bytes=VMEM_LIMIT),
        name="combine",
    )(dest_flat, y, h1, gate)


def _plan_items(counts, n_items_max):
    tiles = (counts + EXP_TM - 1) // EXP_TM
    tile_end = jnp.cumsum(tiles)
    tile_start = tile_end - tiles
    n_items = tile_end[-1]
    g = jnp.arange(n_items_max, dtype=jnp.int32)
    live = g < n_items
    gg = jnp.maximum(jnp.minimum(g, n_items - 1), 0)
    e = jnp.minimum(jnp.searchsorted(tile_end, gg, side="right"), N_EXPERTS - 1).astype(jnp.int32)
    nv = jnp.clip(counts[e] - (gg - tile_start[e]) * EXP_TM, 0, EXP_TM)
    nv = jnp.where(live, nv, 0).astype(jnp.int32)
    return tile_start * EXP_TM, e, gg.astype(jnp.int32), nv


def _layer(x2, bsz, seq, norm1_gain, w_in, q_norm_gain, k_norm_gain, sgu_norm_gain, w_spatial, b_spatial,
           w_branch_a, w_branch_b, w_out, norm2_gain, w_router, b_router, w_up, b_up, w_down, b_down):
    t, d = x2.shape
    proj = _proj(x2, norm1_gain[None, :], w_in, q_norm_gain[None, :], k_norm_gain[None, :])
    oa = _attn(proj, bsz, seq, d, q_norm_gain, k_norm_gain)
    n_items_max = -(-(t * TOP_K) // EXP_TM) + N_EXPERTS
    h1, xs0 = _merge(x2, oa, proj, w_spatial, b_spatial.T, sgu_norm_gain[None, :],
                     w_branch_a.astype(BF16), w_branch_b.astype(BF16), w_out.astype(BF16),
                     n_items_max * EXP_TM * ROW_SUBLANES)
    xp, idx, rank, gate, cnt = _router(h1, norm2_gain[None, :], w_router.T, b_router[:, None])

    counts = cnt[:, 0].astype(jnp.int32)
    seg_start, item_e, item_t, item_nv = _plan_items(counts, n_items_max)
    onehot = idx[:TOP_K, :, None] == jnp.arange(N_EXPERTS, dtype=jnp.int32)
    dest = (jnp.sum(jnp.where(onehot, seg_start.astype(jnp.int32), 0), axis=-1)
            + rank[:TOP_K]).reshape(-1)

    xs = _dispatch(dest, xp, xs0)
    y = _expert(item_e, item_t, item_nv, xs, w_up, b_up, w_down, b_down)
    return _combine(dest, y, h1, gate[:TOP_K].T)


def kernel(x, norm1_gain, w_in, q_norm_gain, k_norm_gain, sgu_norm_gain, w_spatial, b_spatial, w_branch_a,
           w_branch_b, w_out, norm2_gain, w_router, b_router, w_up, b_up, w_down, b_down):
    bsz, seq, d = x.shape
    h = x.reshape(bsz * seq, d)
    for l in range(norm1_gain.shape[0]):
        h = _layer(h, bsz, seq, norm1_gain[l], w_in[l], q_norm_gain[l], k_norm_gain[l], sgu_norm_gain[l],
                   w_spatial[l], b_spatial[l], w_branch_a[l], w_branch_b[l], w_out[l], norm2_gain[l],
                   w_router[l], b_router[l], w_up[l], b_up[l], w_down[l], b_down[l])
    return h.reshape(bsz, seq, d)
```

```python
import math

import jax
import jax.numpy as jnp
from jax import lax
from jax.experimental import pallas as pl
from jax.experimental.pallas import tpu as pltpu

F32 = jnp.float32
BF16 = jnp.bfloat16

EPS = 1e-6
N_HEADS = 8
HEAD_DIM = 128
SGU_GROUPS = 8
SGU_BLOCK = 128
CHUNK = 64
N_EXPERTS = 32
TOP_K = 4
SWIGLU_LIMIT = 7.0
SWIGLU_ALPHA = 1.702

F32_EXP_ZERO_BELOW = -105.0
LOG2_E = 1.4426950408889634

VMEM_LIMIT = 56 * 1024 * 1024
PROJ_TM = 1024
PROJ_TN = 1024
PROJ_ROWS = 512
ATT_T = 256
ATT_HP = 8
MERGE_TM = 256
ROUTER_TM = 1024
EXP_TM = 1408
EXP_SUB = 1408
EXP_MIN_ROWS = 512
EXP_GRAN = 128
EXP_TC = 256
DISPATCH_TB = 1024
COMBINE_TB = 256
DMA_UNROLL = 8
FILL_MAX_ROWS = 4096


def _sigmoid(x):
    return 1.0 / (1.0 + jnp.exp(-x))


ROW_SUBLANES = 8
LANES = 128


def _store_rows_as_tiles(ref, rows2d):
    n = rows2d.shape[0]
    for s in range(ROW_SUBLANES):
        ref[pl.ds(s, n, stride=ROW_SUBLANES), :] = rows2d[:, s * LANES:(s + 1) * LANES]


def _tile_of_row(ref, r):
    start = r * ROW_SUBLANES
    if not isinstance(r, int):
        start = pl.multiple_of(start, ROW_SUBLANES)
    return ref.at[pl.ds(start, ROW_SUBLANES)]


def _load_tile_rows(ref, s, n):
    return ref[pl.ds(s, n, stride=ROW_SUBLANES), :]


def _pack_bf16_pair(lo, hi):
    lo_bits = lax.bitcast_convert_type(lo.astype(BF16).astype(F32), jnp.uint32)
    hi_bits = lax.bitcast_convert_type(hi.astype(BF16).astype(F32), jnp.uint32)
    return hi_bits | (lo_bits >> 16)


def _unpack_bf16_pair(w):
    lo = lax.bitcast_convert_type(w << 16, F32)
    hi = lax.bitcast_convert_type(w & jnp.uint32(0xFFFF0000), F32)
    return lo, hi


def _gelu_tanh(x):
    c = math.sqrt(2.0 / math.pi)
    return 0.5 * x * (1.0 + jnp.tanh(c * (x + 0.044715 * (x * x * x))))


def _proj_kernel(x_ref, g1_ref, w_ref, qg_ref, kg_ref, o_ref, xn_ref):
    j = pl.program_id(1)
    n_q = (N_HEADS * HEAD_DIM) // PROJ_TN

    @pl.when(j == 0)
    def _():
        x = x_ref[...]
        ms = jnp.mean(x * x, axis=-1, keepdims=True)
        xn_ref[...] = (x * lax.rsqrt(ms + EPS) * g1_ref[...]).astype(BF16)

    def head_norm(a):
        gain = jnp.where(j < n_q, qg_ref[...], kg_ref[...])
        outs = []
        for h in range(PROJ_TN // HEAD_DIM):
            ah = a[:, h * HEAD_DIM:(h + 1) * HEAD_DIM]
            ms = jnp.mean(ah * ah, axis=-1, keepdims=True)
            outs.append(ah * lax.rsqrt(ms + EPS) * gain)
        return jnp.concatenate(outs, axis=1)

    def tile(epilogue):
        w = w_ref[...].astype(BF16)
        for r in range(PROJ_TM // PROJ_ROWS):
            rows = slice(r * PROJ_ROWS, (r + 1) * PROJ_ROWS)
            acc = jnp.dot(xn_ref[rows, :], w, preferred_element_type=F32)
            o_ref[rows, :] = epilogue(acc).astype(BF16)

    pl.when(j < 2 * n_q)(lambda: tile(head_norm))
    pl.when((j >= 2 * n_q) & (j < 3 * n_q))(lambda: tile(lambda a: a))
    pl.when((j >= 3 * n_q) & (j < 5 * n_q))(lambda: tile(_gelu_tanh))
    pl.when(j >= 5 * n_q)(lambda: tile(_sigmoid))


def _proj(x2, g1, w_in, qg, kg):
    t, d = x2.shape
    n = w_in.shape[1]
    nj = n // PROJ_TN
    shift = (2 * d) // PROJ_TN
    return pl.pallas_call(
        _proj_kernel,
        grid=(t // PROJ_TM, nj),
        in_specs=[
            pl.BlockSpec((PROJ_TM, d), lambda i, j: (i, 0)),
            pl.BlockSpec((1, d), lambda i, j: (0, 0)),
            pl.BlockSpec((d, PROJ_TN), lambda i, j: (0, j)),
            pl.BlockSpec((1, HEAD_DIM), lambda i, j: (0, 0)),
            pl.BlockSpec((1, HEAD_DIM), lambda i, j: (0, 0)),
        ],
        out_specs=pl.BlockSpec((PROJ_TM, PROJ_TN), lambda i, j: (i, (j + shift) % nj)),
        out_shape=jax.ShapeDtypeStruct((t, n), BF16),
        scratch_shapes=[pltpu.VMEM((PROJ_TM, d), BF16)],
        compiler_params=pltpu.CompilerParams(
            dimension_semantics=("arbitrary", "arbitrary"), vmem_limit_bytes=VMEM_LIMIT),
        name="proj",
    )(x2, g1, w_in, qg, kg)


def _attn_kernel(stop_ref, q_ref, k_ref, v_ref, u_ref, o_ref):
    qi = pl.program_id(2)
    scale = HEAD_DIM ** -0.5 * LOG2_E
    stop_at = stop_ref[0]

    def block(j, accs, runs, diag):
        start = pl.multiple_of(j * ATT_T, ATT_T)
        tri = u_ref[...]
        if diag:
            row = lax.broadcasted_iota(jnp.int32, (ATT_T, ATT_T), 0)
            col = lax.broadcasted_iota(jnp.int32, (ATT_T, ATT_T), 1)
            past = col < row
        heads = range(ATT_HP)
        cols = [slice(h * HEAD_DIM, (h + 1) * HEAD_DIM) for h in heads]
        zs = [lax.dot_general(q_ref[:, cols[h]], k_ref[pl.ds(start, ATT_T), cols[h]],
                              (((1,), (1,)), ((), ())), preferred_element_type=F32) * scale for h in heads]
        sps = [jnp.maximum(z, 0.0) + jnp.log(1.0 + jnp.exp2(-jnp.abs(z))) * LOG2_E for z in zs]
        if diag:
            sps = [jnp.where(past, sp, 0.0) for sp in sps]
        his = [sp.astype(BF16) for sp in sps]
        los = [(sp - hi.astype(F32)).astype(BF16) for sp, hi in zip(sps, his)]
        css = [jnp.dot(hi, tri, preferred_element_type=F32) + jnp.dot(lo, tri, preferred_element_type=F32)
               for hi, lo in zip(his, los)]
        ws = [jnp.exp2(zs[h] - (runs[h] + css[h])) for h in heads]
        if diag:
            ws = [jnp.where(past, w, 0.0) for w in ws]
        new_accs = [accs[h] + jnp.dot(ws[h].astype(BF16), v_ref[pl.ds(start, ATT_T), cols[h]],
                                      preferred_element_type=F32) for h in heads]
        new_runs = [runs[h] + css[h][:, 0:1] for h in heads]
        return tuple(new_accs), tuple(new_runs)

    def smallest(runs):
        m = runs[0]
        for r in runs[1:]:
            m = jnp.minimum(m, r)
        return jnp.min(m)

    accs = tuple(jnp.zeros((ATT_T, HEAD_DIM), F32) for _ in range(ATT_HP))
    runs = tuple(jnp.zeros((ATT_T, 1), F32) for _ in range(ATT_HP))
    accs, runs = block(qi, accs, runs, True)

    def cond(c):
        j, _, _, low = c
        return (j >= 0) & (low < stop_at)

    def body(c):
        j, accs, runs, _ = c
        accs, runs = block(j, accs, runs, False)
        return j - 1, accs, runs, smallest(runs)

    _, accs, _, _ = lax.while_loop(cond, body, (qi - 1, accs, runs, smallest(runs)))
    for h in range(ATT_HP):
        o_ref[:, h * HEAD_DIM:(h + 1) * HEAD_DIM] = accs[h].astype(BF16)


def _attn(proj, bsz, seq, d, q_gain, k_gain):
    t = bsz * seq
    nq = seq // ATT_T
    width = ATT_HP * HEAD_DIM
    col0 = (2 * d) // width
    seg = (N_HEADS * HEAD_DIM) // width
    i = jnp.arange(ATT_T)
    tri = (i[:, None] >= i[None, :]).astype(BF16)
    zmax = 1.02 * math.sqrt(HEAD_DIM) * jnp.max(jnp.abs(q_gain)) * jnp.max(jnp.abs(k_gain))
    stop_at = ((zmax - F32_EXP_ZERO_BELOW) * LOG2_E).reshape(1).astype(F32)
    return pl.pallas_call(
        _attn_kernel,
        grid_spec=pltpu.PrefetchScalarGridSpec(
            num_scalar_prefetch=1,
            grid=(bsz, seg, nq),
            in_specs=[
                pl.BlockSpec((ATT_T, width), lambda b, h, qi, s: (b * nq + qi, col0 + h)),
                pl.BlockSpec((seq, width), lambda b, h, qi, s: (b, col0 + seg + h)),
                pl.BlockSpec((seq, width), lambda b, h, qi, s: (b, col0 + 2 * seg + h)),
                pl.BlockSpec((ATT_T, ATT_T), lambda b, h, qi, s: (0, 0)),
            ],
            out_specs=pl.BlockSpec((ATT_T, width), lambda b, h, qi, s: (b * nq + qi, h)),
        ),
        out_shape=jax.ShapeDtypeStruct((t, N_HEADS * HEAD_DIM), BF16),
        compiler_params=pltpu.CompilerParams(
            dimension_semantics=("arbitrary", "arbitrary", "arbitrary"), vmem_limit_bytes=VMEM_LIMIT),
        name="attn",
    )(stop_at, proj, proj, proj, tri)


def _merge_kernel(x_ref, oa_ref, u_ref, vv_ref, ga_ref, gb_ref, ws_ref, bst_ref, sg_ref,
                  wa_ref, wb_ref, wo_ref, h_ref, fill_ref, vln_ref, ob_ref, zero_ref, fill_sem):
    i = pl.program_id(0)
    z_rows = zero_ref.shape[0]
    n_fill = fill_ref.shape[0] // (pl.num_programs(0) * z_rows)

    @pl.when(i == 0)
    def _():
        zero_ref[...] = jnp.zeros_like(zero_ref)

    def fill_copy(j):
        start = pl.multiple_of((i * n_fill + j) * z_rows, ROW_SUBLANES)
        return pltpu.make_async_copy(zero_ref, fill_ref.at[pl.ds(start, z_rows)], fill_sem)

    for j in range(n_fill):
        fill_copy(j).start()

    vv = vv_ref[...].astype(F32)
    mu = jnp.mean(vv, axis=-1, keepdims=True)
    xc = vv - mu
    var = jnp.mean(xc * xc, axis=-1, keepdims=True)
    vln_ref[...] = (xc * lax.rsqrt(var + EPS) * sg_ref[...]).astype(BF16)

    row = lax.broadcasted_iota(jnp.int32, (SGU_BLOCK, SGU_BLOCK), 0)
    col = lax.broadcasted_iota(jnp.int32, (SGU_BLOCK, SGU_BLOCK), 1)
    mask = (col // CHUNK) <= (row // CHUNK)
    for g in range(SGU_GROUPS):
        wg = jnp.where(mask, ws_ref[g], 0.0).astype(BF16)
        bg = bst_ref[:, g:g + 1]
        cs = slice(g * SGU_BLOCK, (g + 1) * SGU_BLOCK)
        blocks = [slice(n * SGU_BLOCK, (n + 1) * SGU_BLOCK) for n in range(MERGE_TM // SGU_BLOCK)]
        sgu = jnp.dot(wg, jnp.concatenate([vln_ref[rs, cs] for rs in blocks], axis=1),
                      preferred_element_type=F32) + bg
        for n, rs in enumerate(blocks):
            ob_ref[rs, cs] = (u_ref[rs, cs].astype(F32) * sgu[:, n * SGU_BLOCK:(n + 1) * SGU_BLOCK]).astype(BF16)

    ya = jnp.dot(oa_ref[...], wa_ref[...], preferred_element_type=F32)
    yb = jnp.dot(ob_ref[...], wb_ref[...], preferred_element_type=F32)
    y = ga_ref[...].astype(F32) * ya + gb_ref[...].astype(F32) * yb
    h_ref[...] = x_ref[...] + jnp.dot(y.astype(BF16), wo_ref[...], preferred_element_type=F32)

    for j in range(n_fill):
        fill_copy(j).wait()


def _fill_rows_per_copy(rows_per_step):
    for n in range(1, rows_per_step + 1):
        if rows_per_step % n == 0 and (rows_per_step // n) % ROW_SUBLANES == 0 and rows_per_step // n <= FILL_MAX_ROWS:
            return rows_per_step // n
    raise ValueError(f"no aligned split of {rows_per_step} fill rows")


def _merge(x2, oa, proj, w_spatial, b_spatial_t, sgu_gain, wa, wb, wo, n_fill_rows):
    t, d = x2.shape
    aw = oa.shape[1]
    sw = SGU_GROUPS * SGU_BLOCK
    ucol = (2 * d + 3 * aw) // sw
    const = lambda shape: pl.BlockSpec(shape, lambda i: (0,) * len(shape), pipeline_mode=pl.Buffered(1))
    n_steps = t // MERGE_TM
    assert n_fill_rows % n_steps == 0
    z_rows = _fill_rows_per_copy(n_fill_rows // n_steps)
    return pl.pallas_call(
        _merge_kernel,
        grid=(t // MERGE_TM,),
        in_specs=[
            pl.BlockSpec((MERGE_TM, d), lambda i: (i, 0)),
            pl.BlockSpec((MERGE_TM, aw), lambda i: (i, 0)),
            pl.BlockSpec((MERGE_TM, sw), lambda i: (i, ucol)),
            pl.BlockSpec((MERGE_TM, sw), lambda i: (i, ucol + 1)),
            pl.BlockSpec((MERGE_TM, d), lambda i: (i, 0)),
            pl.BlockSpec((MERGE_TM, d), lambda i: (i, 1)),
            const((SGU_GROUPS, SGU_BLOCK, SGU_BLOCK)),
            const((SGU_BLOCK, SGU_GROUPS)),
            const((1, sw)),
            const((aw, d)),
            const((sw, d)),
            const((d, d)),
        ],
        out_specs=[pl.BlockSpec((MERGE_TM, d), lambda i: (i, 0)), pl.BlockSpec(memory_space=pl.ANY)],
        out_shape=[jax.ShapeDtypeStruct((t, d), F32),
                   jax.ShapeDtypeStruct((n_fill_rows, LANES), jnp.uint32)],
        scratch_shapes=[pltpu.VMEM((MERGE_TM, sw), BF16), pltpu.VMEM((MERGE_TM, sw), BF16),
                        pltpu.VMEM((z_rows, LANES), jnp.uint32), pltpu.SemaphoreType.DMA(())],
        compiler_params=pltpu.CompilerParams(
            dimension_semantics=("arbitrary",), vmem_limit_bytes=VMEM_LIMIT),
        name="merge",
    )(x2, oa, proj, proj, proj, proj, w_spatial, b_spatial_t, sgu_gain, wa, wb, wo)


def _router_kernel(h_ref, g2_ref, wrt_ref, brt_ref, ut_ref, xp_ref, idx_ref, rank_ref, gate_ref, cnt_ref,
                   carry_ref):
    @pl.when(pl.program_id(0) == 0)
    def _():
        carry_ref[...] = jnp.zeros_like(carry_ref)

    h = h_ref[...]
    half = h.shape[1] // 2
    ms = jnp.mean(h * h, axis=-1, keepdims=True)
    xn = h * lax.rsqrt(ms + EPS) * g2_ref[...]
    xn_hi = xn.astype(BF16)
    xn_lo = (xn - xn_hi.astype(F32)).astype(BF16)
    hi_bits = lax.bitcast_convert_type(xn_hi.astype(F32), jnp.uint32)
    _store_rows_as_tiles(xp_ref, hi_bits[:, half:] | (hi_bits[:, :half] >> 16))

    wr = wrt_ref[...]
    wr_hi = wr.astype(BF16)
    wr_lo = (wr - wr_hi.astype(F32)).astype(BF16)
    nt = (((1,), (1,)), ((), ()))
    logits = (lax.dot_general(wr_hi, xn_hi, nt, preferred_element_type=F32)
              + lax.dot_general(wr_hi, xn_lo, nt, preferred_element_type=F32)
              + lax.dot_general(wr_lo, xn_hi, nt, preferred_element_type=F32)) + brt_ref[...]
    sub = lax.broadcasted_iota(jnp.int32, logits.shape, 0).astype(F32)
    out_row = lax.broadcasted_iota(jnp.int32, idx_ref.shape, 0)
    work = logits
    vals, sels = [], []
    idx_out = jnp.zeros(idx_ref.shape, jnp.int32)
    for k in range(TOP_K):
        m = jnp.max(work, axis=0, keepdims=True)
        idx = jnp.min(jnp.where(work == m, sub, float(N_EXPERTS)), axis=0, keepdims=True)
        sel = sub == idx
        work = jnp.where(sel, -jnp.inf, work)
        vals.append(m)
        sels.append(sel)
        idx_out = jnp.where(out_row == k, idx.astype(jnp.int32), idx_out)
    idx_ref[...] = idx_out

    exps = [jnp.exp(v - vals[0]) for v in vals]
    denom = exps[0] + exps[1] + exps[2] + exps[3]
    gate_out = jnp.zeros(gate_ref.shape, F32)
    for k in range(TOP_K):
        gate_out = jnp.where(out_row == k, exps[k] / denom, gate_out)
    gate_ref[...] = gate_out

    member = (sels[0] | sels[1] | sels[2] | sels[3])
    before = jnp.dot(member.astype(BF16), ut_ref[...], preferred_element_type=F32)
    pos = carry_ref[...] + before
    rank_out = jnp.zeros(rank_ref.shape, jnp.int32)
    for k in range(TOP_K):
        r = jnp.sum(jnp.where(sels[k], pos, 0.0), axis=0, keepdims=True)
        rank_out = jnp.where(out_row == k, r.astype(jnp.int32), rank_out)
    rank_ref[...] = rank_out

    carry_ref[...] = carry_ref[...] + jnp.sum(member.astype(F32), axis=1, keepdims=True)
    cnt_ref[...] = jnp.broadcast_to(carry_ref[...], cnt_ref.shape)


def _router(h1, g2, w_router_t, b_router_t):
    t, d = h1.shape
    const = lambda shape: pl.BlockSpec(shape, lambda i: (0,) * len(shape))
    i = jnp.arange(ROUTER_TM)
    ut = (i[:, None] < i[None, :]).astype(BF16)
    rows, lanes = 8, 128
    tok = lambda: pl.BlockSpec((rows, ROUTER_TM), lambda i: (0, i))
    return pl.pallas_call(
        _router_kernel,
        grid=(t // ROUTER_TM,),
        in_specs=[
            pl.BlockSpec((ROUTER_TM, d), lambda i: (i, 0)),
            const((1, d)),
            const((N_EXPERTS, d)),
            const((N_EXPERTS, 1)),
            const((ROUTER_TM, ROUTER_TM)),
        ],
        out_specs=[
            pl.BlockSpec((ROUTER_TM * ROW_SUBLANES, LANES), lambda i: (i, 0)),
            tok(), tok(), tok(),
            pl.BlockSpec((N_EXPERTS, lanes), lambda i: (0, 0)),
        ],
        out_shape=[
            jax.ShapeDtypeStruct((t * ROW_SUBLANES, LANES), jnp.uint32),
            jax.ShapeDtypeStruct((rows, t), jnp.int32),
            jax.ShapeDtypeStruct((rows, t), jnp.int32),
            jax.ShapeDtypeStruct((rows, t), F32),
            jax.ShapeDtypeStruct((N_EXPERTS, lanes), F32),
        ],
        scratch_shapes=[pltpu.VMEM((N_EXPERTS, 1), F32)],
        compiler_params=pltpu.CompilerParams(
            dimension_semantics=("arbitrary",), vmem_limit_bytes=VMEM_LIMIT),
        name="router",
    )(h1, g2, w_router_t, b_router_t, ut)


def _dispatch_kernel(dest_ref, xp_ref, xs_in_ref, xs_ref, sem):
    del xs_in_ref
    i = pl.program_id(0)

    n_tok = dest_ref.shape[0] // TOP_K

    def row_copy(s, r):
        return pltpu.make_async_copy(_tile_of_row(xp_ref, s), _tile_of_row(xs_ref, r), sem)

    def issue(b, _):
        for j in range(DMA_UNROLL):
            s = b * DMA_UNROLL + j
            for k in range(TOP_K):
                row_copy(s, dest_ref[k * n_tok + i * DISPATCH_TB + s]).start(priority=k % 2)
        return 0

    def drain(b, _):
        for _ in range(DMA_UNROLL * TOP_K):
            row_copy(0, 0).wait()
        return 0

    lax.fori_loop(0, DISPATCH_TB // DMA_UNROLL, issue, 0)
    lax.fori_loop(0, DISPATCH_TB // DMA_UNROLL, drain, 0)


def _dispatch(dest_flat, xp, xs0):
    t = xp.shape[0] // ROW_SUBLANES
    n_rows = xs0.shape[0] // ROW_SUBLANES
    return pl.pallas_call(
        _dispatch_kernel,
        grid_spec=pltpu.PrefetchScalarGridSpec(
            num_scalar_prefetch=1,
            grid=(t // DISPATCH_TB,),
            in_specs=[pl.BlockSpec((DISPATCH_TB * ROW_SUBLANES, LANES), lambda i, dest: (i, 0)),
                      pl.BlockSpec(memory_space=pl.ANY)],
            out_specs=pl.BlockSpec(memory_space=pl.ANY),
            scratch_shapes=[pltpu.SemaphoreType.DMA(())],
        ),
        out_shape=jax.ShapeDtypeStruct((n_rows * ROW_SUBLANES, LANES), jnp.uint32),
        input_output_aliases={2: 0},
        compiler_params=pltpu.CompilerParams(
            dimension_semantics=("arbitrary",), has_side_effects=True),
        name="dispatch",
    )(dest_flat, xp, xs0)


def _expert_pieces(n_rows):
    k = -(-n_rows // EXP_SUB)
    base, extra = divmod(n_rows // EXP_GRAN, k)
    return [(base + (1 if j < extra else 0)) * EXP_GRAN for j in range(k)]


def _expert_kernel(ie_ref, it_ref, nv_ref, xs_ref, wg_ref, wu_ref, wd_ref, bg_ref, bu_ref, bd_ref,
                   y_ref, acc_ref):
    del ie_ref, it_ref
    i = pl.program_id(0)
    c = pl.program_id(1)
    nc = pl.num_programs(1)
    nv = nv_ref[i]
    half = ROW_SUBLANES * LANES

    @pl.when((i == 0) & (c == 0))
    def _():
        acc_ref[...] = jnp.zeros_like(acc_ref)

    def run(n_rows):
        wg = wg_ref[0].astype(BF16)
        wu = wu_ref[0].astype(BF16)
        wd = wd_ref[0].astype(BF16)
        first = c == 0
        lo = 0
        for m in _expert_pieces(n_rows):
            rows = slice(lo, lo + m)
            tiles = lambda s, lo=lo, m=m: pl.ds(lo * ROW_SUBLANES + s, m, stride=ROW_SUBLANES)
            pairs = [_unpack_bf16_pair(xs_ref[tiles(s), :]) for s in range(ROW_SUBLANES)]
            xb = jnp.concatenate([p[0].astype(BF16) for p in pairs] + [p[1].astype(BF16) for p in pairs], axis=1)
            g = jnp.dot(xb, wg, preferred_element_type=F32) + bg_ref[0]
            u = jnp.dot(xb, wu, preferred_element_type=F32) + bu_ref[0]
            g = jnp.minimum(g, SWIGLU_LIMIT)
            u = jnp.clip(u, -SWIGLU_LIMIT, SWIGLU_LIMIT)
            a = g * _sigmoid(SWIGLU_ALPHA * g) * (u + 1.0)
            y = jnp.dot(a.astype(BF16), wd, preferred_element_type=F32)
            acc_ref[rows, :] = jnp.where(first, jnp.broadcast_to(bd_ref[0], y.shape), acc_ref[rows, :]) + y
            lo += m

    for n_rows in range(EXP_MIN_ROWS, EXP_TM + 1, EXP_GRAN):
        above = 0 if n_rows == EXP_MIN_ROWS else n_rows - EXP_GRAN
        pl.when((nv > above) & (nv <= n_rows))(lambda n_rows=n_rows: run(n_rows))

    @pl.when((c == nc - 1) & (nv > 0))
    def _():
        for s in range(ROW_SUBLANES):
            lo = acc_ref[:, s * LANES:(s + 1) * LANES]
            hi = acc_ref[:, half + s * LANES:half + (s + 1) * LANES]
            y_ref[pl.ds(s, EXP_TM, stride=ROW_SUBLANES), :] = _pack_bf16_pair(lo, hi)


def _expert(item_e, item_t, item_nv, xs, w_up, b_up, w_down, b_down):
    n_rows = xs.shape[0] // ROW_SUBLANES
    ne, d, two_f = w_up.shape
    assert d == 2 * ROW_SUBLANES * LANES
    tile = (EXP_TM * ROW_SUBLANES, LANES)
    f = two_f // 2
    nc = f // EXP_TC
    n_items = item_e.shape[0]
    b_up3 = b_up.reshape(ne, 1, two_f)
    b_down3 = b_down.reshape(ne, 1, d)

    def chunk(c, nv_ref, i):
        return jnp.where(nv_ref[i] > 0, c, nc - 1)

    return pl.pallas_call(
        _expert_kernel,
        grid_spec=pltpu.PrefetchScalarGridSpec(
            num_scalar_prefetch=3,
            grid=(n_items, nc),
            in_specs=[
                pl.BlockSpec(tile, lambda i, c, ie, it, nv: (it[i], 0)),
                pl.BlockSpec((1, d, EXP_TC), lambda i, c, ie, it, nv: (ie[i], 0, chunk(c, nv, i))),
                pl.BlockSpec((1, d, EXP_TC), lambda i, c, ie, it, nv: (ie[i], 0, nc + chunk(c, nv, i))),
                pl.BlockSpec((1, EXP_TC, d), lambda i, c, ie, it, nv: (ie[i], chunk(c, nv, i), 0)),
                pl.BlockSpec((1, 1, EXP_TC), lambda i, c, ie, it, nv: (ie[i], 0, chunk(c, nv, i))),
                pl.BlockSpec((1, 1, EXP_TC), lambda i, c, ie, it, nv: (ie[i], 0, nc + chunk(c, nv, i))),
                pl.BlockSpec((1, 1, d), lambda i, c, ie, it, nv: (ie[i], 0, 0)),
            ],
            out_specs=pl.BlockSpec(tile, lambda i, c, ie, it, nv: (it[i], 0)),
            scratch_shapes=[pltpu.VMEM((EXP_TM, d), F32)],
        ),
        out_shape=jax.ShapeDtypeStruct((n_rows * ROW_SUBLANES, LANES), jnp.uint32),
        input_output_aliases={3: 0},
        compiler_params=pltpu.CompilerParams(
            dimension_semantics=("arbitrary", "arbitrary"), vmem_limit_bytes=VMEM_LIMIT),
        name="expert",
    )(item_e, item_t, item_nv, xs, w_up, w_up, w_down, b_up3, b_up3, b_down3)


def _combine_kernel(dest_ref, y_ref, h_ref, gate_ref, o_ref, buf_ref, sem):
    i = pl.program_id(0)
    n = pl.num_programs(0)
    n_tok = dest_ref.shape[0] // TOP_K

    def row_copy(r, slot, k, s):
        return pltpu.make_async_copy(_tile_of_row(y_ref, r), _tile_of_row(buf_ref.at[slot, k], s), sem.at[slot])

    def gather(step, slot):
        def issue(b, _):
            for j in range(DMA_UNROLL):
                s = b * DMA_UNROLL + j
                for k in range(TOP_K):
                    row_copy(dest_ref[k * n_tok + step * COMBINE_TB + s], slot, k, s).start(priority=k % 2)
            return 0
        lax.fori_loop(0, COMBINE_TB // DMA_UNROLL, issue, 0)

    @pl.when(i == 0)
    def _():
        gather(0, 0)

    for slot in range(2):
        @pl.when((i + 1 < n) & ((i + 1) % 2 == slot))
        def _():
            gather(i + 1, slot)

    for slot in range(2):
        @pl.when(i % 2 == slot)
        def _():
            def drain(b, _):
                for _ in range(DMA_UNROLL * TOP_K):
                    row_copy(0, slot, 0, 0).wait()
                return 0
            lax.fori_loop(0, COMBINE_TB // DMA_UNROLL, drain, 0)
            half = ROW_SUBLANES * LANES
            gates = [jnp.broadcast_to(gate_ref[:, k:k + 1], (COMBINE_TB, LANES)) for k in range(TOP_K)]
            for s in range(ROW_SUBLANES):
                lo_cols = slice(s * LANES, (s + 1) * LANES)
                hi_cols = slice(half + s * LANES, half + (s + 1) * LANES)
                acc_lo = h_ref[:, lo_cols]
                acc_hi = h_ref[:, hi_cols]
                for k in range(TOP_K):
                    lo, hi = _unpack_bf16_pair(_load_tile_rows(buf_ref.at[slot, k], s, COMBINE_TB))
                    acc_lo = acc_lo + gates[k] * lo
                    acc_hi = acc_hi + gates[k] * hi
                o_ref[:, lo_cols] = acc_lo
                o_ref[:, hi_cols] = acc_hi


def _combine(dest_flat, y, h1, gate):
    t, d = h1.shape
    lanes = gate.shape[1]
    return pl.pallas_call(
        _combine_kernel,
        grid_spec=pltpu.PrefetchScalarGridSpec(
            num_scalar_prefetch=1,
            grid=(t // COMBINE_TB,),
            in_specs=[
                pl.BlockSpec(memory_space=pl.ANY),
                pl.BlockSpec((COMBINE_TB, d), lambda i, dest: (i, 0)),
                pl.BlockSpec((COMBINE_TB, lanes), lambda i, dest: (i, 0)),
            ],
            out_specs=pl.BlockSpec((COMBINE_TB, d), lambda i, dest: (i, 0)),
            scratch_shapes=[pltpu.VMEM((2, TOP_K, COMBINE_TB * ROW_SUBLANES, LANES), jnp.uint32),
                            pltpu.SemaphoreType.DMA((2,))],
        ),
        out_shape=jax.ShapeDtypeStruct((t, d), F32),
        compiler_params=pltpu.CompilerParams(
            dimension_semantics=("arbitrary",), vmem_limit_bytes=VMEM_LIMIT),
        name="combine",
    )(dest_flat, y, h1, gate)


def _plan_items(counts, n_items_max):
    tiles = (counts + EXP_TM - 1) // EXP_TM
    tile_end = jnp.cumsum(tiles)
    tile_start = tile_end - tiles
    n_items = tile_end[-1]
    g = jnp.arange(n_items_max, dtype=jnp.int32)
    live = g < n_items
    gg = jnp.maximum(jnp.minimum(g, n_items - 1), 0)
    e = jnp.minimum(jnp.searchsorted(tile_end, gg, side="right"), N_EXPERTS - 1).astype(jnp.int32)
    nv = jnp.clip(counts[e] - (gg - tile_start[e]) * EXP_TM, 0, EXP_TM)
    nv = jnp.where(live, nv, 0).astype(jnp.int32)
    return tile_start * EXP_TM, e, gg.astype(jnp.int32), nv


def _layer(x2, bsz, seq, norm1_gain, w_in, q_norm_gain, k_norm_gain, sgu_norm_gain, w_spatial, b_spatial,
           w_branch_a, w_branch_b, w_out, norm2_gain, w_router, b_router, w_up, b_up, w_down, b_down):
    t, d = x2.shape
    proj = _proj(x2, norm1_gain[None, :], w_in, q_norm_gain[None, :], k_norm_gain[None, :])
    oa = _attn(proj, bsz, seq, d, q_norm_gain, k_norm_gain)
    n_items_max = -(-(t * TOP_K) // EXP_TM) + N_EXPERTS
    h1, xs0 = _merge(x2, oa, proj, w_spatial, b_spatial.T, sgu_norm_gain[None, :],
                     w_branch_a.astype(BF16), w_branch_b.astype(BF16), w_out.astype(BF16),
                     n_items_max * EXP_TM * ROW_SUBLANES)
    xp, idx, rank, gate, cnt = _router(h1, norm2_gain[None, :], w_router.T, b_router[:, None])

    counts = cnt[:, 0].astype(jnp.int32)
    seg_start, item_e, item_t, item_nv = _plan_items(counts, n_items_max)
    onehot = idx[:TOP_K, :, None] == jnp.arange(N_EXPERTS, dtype=jnp.int32)
    dest = (jnp.sum(jnp.where(onehot, seg_start.astype(jnp.int32), 0), axis=-1)
            + rank[:TOP_K]).reshape(-1)

    xs = _dispatch(dest, xp, xs0)
    y = _expert(item_e, item_t, item_nv, xs, w_up, b_up, w_down, b_down)
    return _combine(dest, y, h1, gate[:TOP_K].T)


def kernel(x, norm1_gain, w_in, q_norm_gain, k_norm_gain, sgu_norm_gain, w_spatial, b_spatial, w_branch_a,
           w_branch_b, w_out, norm2_gain, w_router, b_router, w_up, b_up, w_down, b_down):
    bsz, seq, d = x.shape
    h = x.reshape(bsz * seq, d)
    for l in range(norm1_gain.shape[0]):
        h = _layer(h, bsz, seq, norm1_gain[l], w_in[l], q_norm_gain[l], k_norm_gain[l], sgu_norm_gain[l],
                   w_spatial[l], b_spatial[l], w_branch_a[l], w_branch_b[l], w_out[l], norm2_gain[l],
                   w_router[l], b_router[l], w_up[l], b_up[l], w_down[l], b_down[l])
    return h.reshape(bsz, seq, d)
```

```python
import math

import jax
import jax.numpy as jnp
from jax import lax
from jax.experimental import pallas as pl
from jax.experimental.pallas import tpu as pltpu

F32 = jnp.float32
BF16 = jnp.bfloat16

EPS = 1e-6
N_HEADS = 8
HEAD_DIM = 128
SGU_GROUPS = 8
SGU_BLOCK = 128
CHUNK = 64
N_EXPERTS = 32
TOP_K = 4
SWIGLU_LIMIT = 7.0
SWIGLU_ALPHA = 1.702

F32_EXP_ZERO_BELOW = -105.0
LOG2_E = 1.4426950408889634

V7X_VMEM_BYTES = 64 * 1024 * 1024
VMEM_LIMIT = V7X_VMEM_BYTES - 8 * 1024 * 1024
PROJ_TM = 1024
PROJ_TN = 1024
PROJ_ROWS = 512
ATT_T = 256
ATT_HP = 8
MERGE_TM = 256
ROUTER_TM = 1024
EXP_TM = 1408
EXP_SUB = 1408
EXP_MIN_ROWS = 512
EXP_GRAN = 128
EXP_TC = 256
DISPATCH_TB = 1024
COMBINE_TB = 128
DMA_UNROLL = 8
FILL_MAX_ROWS = 4096


def _sigmoid(x):
    return 1.0 / (1.0 + jnp.exp(-x))


ROW_SUBLANES = 8
LANES = 128


def _store_rows_as_tiles(ref, rows2d):
    n = rows2d.shape[0]
    for s in range(ROW_SUBLANES):
        ref[pl.ds(s, n, stride=ROW_SUBLANES), :] = rows2d[:, s * LANES:(s + 1) * LANES]


def _tile_of_row(ref, r):
    start = r * ROW_SUBLANES
    if not isinstance(r, int):
        start = pl.multiple_of(start, ROW_SUBLANES)
    return ref.at[pl.ds(start, ROW_SUBLANES)]


def _load_tile_rows(ref, s, n):
    return ref[pl.ds(s, n, stride=ROW_SUBLANES), :]


def _pack_bf16_pair(lo, hi):
    lo_bits = lax.bitcast_convert_type(lo.astype(BF16).astype(F32), jnp.uint32)
    hi_bits = lax.bitcast_convert_type(hi.astype(BF16).astype(F32), jnp.uint32)
    return hi_bits | (lo_bits >> 16)


def _unpack_bf16_pair(w):
    lo = lax.bitcast_convert_type(w << 16, F32)
    hi = lax.bitcast_convert_type(w & jnp.uint32(0xFFFF0000), F32)
    return lo, hi


def _gelu_tanh(x):
    c = math.sqrt(2.0 / math.pi)
    return 0.5 * x * (1.0 + jnp.tanh(c * (x + 0.044715 * (x * x * x))))


def _proj_kernel(x_ref, g1_ref, w_ref, qg_ref, kg_ref, o_ref, xn_ref):
    j = pl.program_id(1)
    n_q = (N_HEADS * HEAD_DIM) // PROJ_TN

    @pl.when(j == 0)
    def _():
        x = x_ref[...]
        ms = jnp.mean(x * x, axis=-1, keepdims=True)
        xn_ref[...] = (x * lax.rsqrt(ms + EPS) * g1_ref[...]).astype(BF16)

    def head_norm(a):
        gain = jnp.where(j < n_q, qg_ref[...], kg_ref[...])
        outs = []
        for h in range(PROJ_TN // HEAD_DIM):
            ah = a[:, h * HEAD_DIM:(h + 1) * HEAD_DIM]
            ms = jnp.mean(ah * ah, axis=-1, keepdims=True)
            outs.append(ah * lax.rsqrt(ms + EPS) * gain)
        return jnp.concatenate(outs, axis=1)

    def tile(epilogue):
        w = w_ref[...].astype(BF16)
        for r in range(PROJ_TM // PROJ_ROWS):
            rows = slice(r * PROJ_ROWS, (r + 1) * PROJ_ROWS)
            acc = jnp.dot(xn_ref[rows, :], w, preferred_element_type=F32)
            o_ref[rows, :] = epilogue(acc).astype(BF16)

    pl.when(j < 2 * n_q)(lambda: tile(head_norm))
    pl.when((j >= 2 * n_q) & (j < 3 * n_q))(lambda: tile(lambda a: a))
    pl.when((j >= 3 * n_q) & (j < 5 * n_q))(lambda: tile(_gelu_tanh))
    pl.when(j >= 5 * n_q)(lambda: tile(_sigmoid))


def _proj(x2, g1, w_in, qg, kg):
    t, d = x2.shape
    n = w_in.shape[1]
    nj = n // PROJ_TN
    shift = (2 * d) // PROJ_TN
    return pl.pallas_call(
        _proj_kernel,
        grid=(t // PROJ_TM, nj),
        in_specs=[
            pl.BlockSpec((PROJ_TM, d), lambda i, j: (i, 0)),
            pl.BlockSpec((1, d), lambda i, j: (0, 0)),
            pl.BlockSpec((d, PROJ_TN), lambda i, j: (0, j)),
            pl.BlockSpec((1, HEAD_DIM), lambda i, j: (0, 0)),
            pl.BlockSpec((1, HEAD_DIM), lambda i, j: (0, 0)),
        ],
        out_specs=pl.BlockSpec((PROJ_TM, PROJ_TN), lambda i, j: (i, (j + shift) % nj)),
        out_shape=jax.ShapeDtypeStruct((t, n), BF16),
        scratch_shapes=[pltpu.VMEM((PROJ_TM, d), BF16)],
        compiler_params=pltpu.CompilerParams(
            dimension_semantics=("arbitrary", "arbitrary"), vmem_limit_bytes=VMEM_LIMIT),
        name="proj",
    )(x2, g1, w_in, qg, kg)


def _attn_kernel(stop_ref, q_ref, k_ref, v_ref, u_ref, o_ref):
    qi = pl.program_id(2)
    scale = HEAD_DIM ** -0.5 * LOG2_E
    stop_at = stop_ref[0]

    def block(j, accs, runs, diag):
        start = pl.multiple_of(j * ATT_T, ATT_T)
        tri = u_ref[...]
        if diag:
            row = lax.broadcasted_iota(jnp.int32, (ATT_T, ATT_T), 0)
            col = lax.broadcasted_iota(jnp.int32, (ATT_T, ATT_T), 1)
            past = col < row
        heads = range(ATT_HP)
        cols = [slice(h * HEAD_DIM, (h + 1) * HEAD_DIM) for h in heads]
        zs = [lax.dot_general(q_ref[:, cols[h]], k_ref[pl.ds(start, ATT_T), cols[h]],
                              (((1,), (1,)), ((), ())), preferred_element_type=F32) * scale for h in heads]
        sps = [jnp.maximum(z, 0.0) + jnp.log(1.0 + jnp.exp2(-jnp.abs(z))) * LOG2_E for z in zs]
        if diag:
            sps = [jnp.where(past, sp, 0.0) for sp in sps]
        his = [sp.astype(BF16) for sp in sps]
        los = [(sp - hi.astype(F32)).astype(BF16) for sp, hi in zip(sps, his)]
        css = [jnp.dot(hi, tri, preferred_element_type=F32) + jnp.dot(lo, tri, preferred_element_type=F32)
               for hi, lo in zip(his, los)]
        ws = [jnp.exp2(zs[h] - (runs[h] + css[h])) for h in heads]
        if diag:
            ws = [jnp.where(past, w, 0.0) for w in ws]
        new_accs = [accs[h] + jnp.dot(ws[h].astype(BF16), v_ref[pl.ds(start, ATT_T), cols[h]],
                                      preferred_element_type=F32) for h in heads]
        new_runs = [runs[h] + css[h][:, 0:1] for h in heads]
        return tuple(new_accs), tuple(new_runs)

    def smallest(runs):
        m = runs[0]
        for r in runs[1:]:
            m = jnp.minimum(m, r)
        return jnp.min(m)

    accs = tuple(jnp.zeros((ATT_T, HEAD_DIM), F32) for _ in range(ATT_HP))
    runs = tuple(jnp.zeros((ATT_T, 1), F32) for _ in range(ATT_HP))
    accs, runs = block(qi, accs, runs, True)

    def cond(c):
        j, _, _, low = c
        return (j >= 0) & (low < stop_at)

    def body(c):
        j, accs, runs, _ = c
        accs, runs = block(j, accs, runs, False)
        return j - 1, accs, runs, smallest(runs)

    _, accs, _, _ = lax.while_loop(cond, body, (qi - 1, accs, runs, smallest(runs)))
    for h in range(ATT_HP):
        o_ref[:, h * HEAD_DIM:(h + 1) * HEAD_DIM] = accs[h].astype(BF16)


def _attn(proj, bsz, seq, d, q_gain, k_gain):
    t = bsz * seq
    nq = seq // ATT_T
    width = ATT_HP * HEAD_DIM
    col0 = (2 * d) // width
    seg = (N_HEADS * HEAD_DIM) // width
    i = jnp.arange(ATT_T)
    tri = (i[:, None] >= i[None, :]).astype(BF16)
    zmax = 1.02 * math.sqrt(HEAD_DIM) * jnp.max(jnp.abs(q_gain)) * jnp.max(jnp.abs(k_gain))
    stop_at = ((zmax - F32_EXP_ZERO_BELOW) * LOG2_E).reshape(1).astype(F32)
    return pl.pallas_call(
        _attn_kernel,
        grid_spec=pltpu.PrefetchScalarGridSpec(
            num_scalar_prefetch=1,
            grid=(bsz, seg, nq),
            in_specs=[
                pl.BlockSpec((ATT_T, width), lambda b, h, qi, s: (b * nq + qi, col0 + h)),
                pl.BlockSpec((seq, width), lambda b, h, qi, s: (b, col0 + seg + h)),
                pl.BlockSpec((seq, width), lambda b, h, qi, s: (b, col0 + 2 * seg + h)),
                pl.BlockSpec((ATT_T, ATT_T), lambda b, h, qi, s: (0, 0)),
            ],
            out_specs=pl.BlockSpec((ATT_T, width), lambda b, h, qi, s: (b * nq + qi, h)),
        ),
        out_shape=jax.ShapeDtypeStruct((t, N_HEADS * HEAD_DIM), BF16),
        compiler_params=pltpu.CompilerParams(
            dimension_semantics=("arbitrary", "arbitrary", "arbitrary"), vmem_limit_bytes=VMEM_LIMIT),
        name="attn",
    )(stop_at, proj, proj, proj, tri)


def _merge_kernel(x_ref, oa_ref, u_ref, vv_ref, ga_ref, gb_ref, ws_ref, bst_ref, sg_ref,
                  wa_ref, wb_ref, wo_ref, h_ref, fill_ref, vln_ref, ob_ref, zero_ref, fill_sem):
    i = pl.program_id(0)
    z_rows = zero_ref.shape[0]
    n_fill = fill_ref.shape[0] // (pl.num_programs(0) * z_rows)

    @pl.when(i == 0)
    def _():
        zero_ref[...] = jnp.zeros_like(zero_ref)

    def fill_copy(j):
        start = pl.multiple_of((i * n_fill + j) * z_rows, ROW_SUBLANES)
        return pltpu.make_async_copy(zero_ref, fill_ref.at[pl.ds(start, z_rows)], fill_sem)

    for j in range(n_fill):
        fill_copy(j).start()

    vv = vv_ref[...].astype(F32)
    mu = jnp.mean(vv, axis=-1, keepdims=True)
    xc = vv - mu
    var = jnp.mean(xc * xc, axis=-1, keepdims=True)
    vln_ref[...] = (xc * lax.rsqrt(var + EPS) * sg_ref[...]).astype(BF16)

    row = lax.broadcasted_iota(jnp.int32, (SGU_BLOCK, SGU_BLOCK), 0)
    col = lax.broadcasted_iota(jnp.int32, (SGU_BLOCK, SGU_BLOCK), 1)
    mask = (col // CHUNK) <= (row // CHUNK)
    for g in range(SGU_GROUPS):
        wg = jnp.where(mask, ws_ref[g], 0.0).astype(BF16)
        bg = bst_ref[:, g:g + 1]
        cs = slice(g * SGU_BLOCK, (g + 1) * SGU_BLOCK)
        blocks = [slice(n * SGU_BLOCK, (n + 1) * SGU_BLOCK) for n in range(MERGE_TM // SGU_BLOCK)]
        sgu = jnp.dot(wg, jnp.concatenate([vln_ref[rs, cs] for rs in blocks], axis=1),
                      preferred_element_type=F32) + bg
        for n, rs in enumerate(blocks):
            ob_ref[rs, cs] = (u_ref[rs, cs].astype(F32) * sgu[:, n * SGU_BLOCK:(n + 1) * SGU_BLOCK]).astype(BF16)

    ya = jnp.dot(oa_ref[...], wa_ref[...], preferred_element_type=F32)
    yb = jnp.dot(ob_ref[...], wb_ref[...], preferred_element_type=F32)
    y = ga_ref[...].astype(F32) * ya + gb_ref[...].astype(F32) * yb
    h_ref[...] = x_ref[...] + jnp.dot(y.astype(BF16), wo_ref[...], preferred_element_type=F32)

    for j in range(n_fill):
        fill_copy(j).wait()


def _fill_rows_per_copy(rows_per_step):
    for n in range(1, rows_per_step + 1):
        if rows_per_step % n == 0 and (rows_per_step // n) % ROW_SUBLANES == 0 and rows_per_step // n <= FILL_MAX_ROWS:
            return rows_per_step // n
    raise ValueError(f"no aligned split of {rows_per_step} fill rows")


def _merge(x2, oa, proj, w_spatial, b_spatial_t, sgu_gain, wa, wb, wo, n_fill_rows):
    t, d = x2.shape
    aw = oa.shape[1]
    sw = SGU_GROUPS * SGU_BLOCK
    ucol = (2 * d + 3 * aw) // sw
    const = lambda shape: pl.BlockSpec(shape, lambda i: (0,) * len(shape), pipeline_mode=pl.Buffered(1))
    n_steps = t // MERGE_TM
    assert n_fill_rows % n_steps == 0
    z_rows = _fill_rows_per_copy(n_fill_rows // n_steps)
    return pl.pallas_call(
        _merge_kernel,
        grid=(t // MERGE_TM,),
        in_specs=[
            pl.BlockSpec((MERGE_TM, d), lambda i: (i, 0)),
            pl.BlockSpec((MERGE_TM, aw), lambda i: (i, 0)),
            pl.BlockSpec((MERGE_TM, sw), lambda i: (i, ucol)),
            pl.BlockSpec((MERGE_TM, sw), lambda i: (i, ucol + 1)),
            pl.BlockSpec((MERGE_TM, d), lambda i: (i, 0)),
            pl.BlockSpec((MERGE_TM, d), lambda i: (i, 1)),
            const((SGU_GROUPS, SGU_BLOCK, SGU_BLOCK)),
            const((SGU_BLOCK, SGU_GROUPS)),
            const((1, sw)),
            const((aw, d)),
            const((sw, d)),
            const((d, d)),
        ],
        out_specs=[pl.BlockSpec((MERGE_TM, d), lambda i: (i, 0)), pl.BlockSpec(memory_space=pl.ANY)],
        out_shape=[jax.ShapeDtypeStruct((t, d), F32),
                   jax.ShapeDtypeStruct((n_fill_rows, LANES), jnp.uint32)],
        scratch_shapes=[pltpu.VMEM((MERGE_TM, sw), BF16), pltpu.VMEM((MERGE_TM, sw), BF16),
                        pltpu.VMEM((z_rows, LANES), jnp.uint32), pltpu.SemaphoreType.DMA(())],
        compiler_params=pltpu.CompilerParams(
            dimension_semantics=("arbitrary",), vmem_limit_bytes=VMEM_LIMIT),
        name="merge",
    )(x2, oa, proj, proj, proj, proj, w_spatial, b_spatial_t, sgu_gain, wa, wb, wo)


def _router_kernel(h_ref, g2_ref, wrt_ref, brt_ref, ut_ref, xp_ref, idx_ref, rank_ref, gate_ref, cnt_ref,
                   carry_ref):
    @pl.when(pl.program_id(0) == 0)
    def _():
        carry_ref[...] = jnp.zeros_like(carry_ref)

    h = h_ref[...]
    half = h.shape[1] // 2
    ms = jnp.mean(h * h, axis=-1, keepdims=True)
    xn = h * lax.rsqrt(ms + EPS) * g2_ref[...]
    xn_hi = xn.astype(BF16)
    xn_lo = (xn - xn_hi.astype(F32)).astype(BF16)
    hi_bits = lax.bitcast_convert_type(xn_hi.astype(F32), jnp.uint32)
    _store_rows_as_tiles(xp_ref, hi_bits[:, half:] | (hi_bits[:, :half] >> 16))

    wr = wrt_ref[...]
    wr_hi = wr.astype(BF16)
    wr_lo = (wr - wr_hi.astype(F32)).astype(BF16)
    nt = (((1,), (1,)), ((), ()))
    logits = (lax.dot_general(wr_hi, xn_hi, nt, preferred_element_type=F32)
              + lax.dot_general(wr_hi, xn_lo, nt, preferred_element_type=F32)
              + lax.dot_general(wr_lo, xn_hi, nt, preferred_element_type=F32)) + brt_ref[...]
    sub = lax.broadcasted_iota(jnp.int32, logits.shape, 0).astype(F32)
    out_row = lax.broadcasted_iota(jnp.int32, idx_ref.shape, 0)
    work = logits
    vals, sels = [], []
    idx_out = jnp.zeros(idx_ref.shape, jnp.int32)
    for k in range(TOP_K):
        m = jnp.max(work, axis=0, keepdims=True)
        idx = jnp.min(jnp.where(work == m, sub, float(N_EXPERTS)), axis=0, keepdims=True)
        sel = sub == idx
        work = jnp.where(sel, -jnp.inf, work)
        vals.append(m)
        sels.append(sel)
        idx_out = jnp.where(out_row == k, idx.astype(jnp.int32), idx_out)
    idx_ref[...] = idx_out

    exps = [jnp.exp(v - vals[0]) for v in vals]
    denom = exps[0] + exps[1] + exps[2] + exps[3]
    gate_out = jnp.zeros(gate_ref.shape, F32)
    for k in range(TOP_K):
        gate_out = jnp.where(out_row == k, exps[k] / denom, gate_out)
    gate_ref[...] = gate_out

    member = (sels[0] | sels[1] | sels[2] | sels[3])
    before = jnp.dot(member.astype(BF16), ut_ref[...], preferred_element_type=F32)
    pos = carry_ref[...] + before
    rank_out = jnp.zeros(rank_ref.shape, jnp.int32)
    for k in range(TOP_K):
        r = jnp.sum(jnp.where(sels[k], pos, 0.0), axis=0, keepdims=True)
        rank_out = jnp.where(out_row == k, r.astype(jnp.int32), rank_out)
    rank_ref[...] = rank_out

    carry_ref[...] = carry_ref[...] + jnp.sum(member.astype(F32), axis=1, keepdims=True)
    cnt_ref[...] = jnp.broadcast_to(carry_ref[...], cnt_ref.shape)


def _router(h1, g2, w_router_t, b_router_t):
    t, d = h1.shape
    const = lambda shape: pl.BlockSpec(shape, lambda i: (0,) * len(shape))
    i = jnp.arange(ROUTER_TM)
    ut = (i[:, None] < i[None, :]).astype(BF16)
    rows, lanes = ROW_SUBLANES, LANES
    tok = lambda: pl.BlockSpec((rows, ROUTER_TM), lambda i: (0, i))
    return pl.pallas_call(
        _router_kernel,
        grid=(t // ROUTER_TM,),
        in_specs=[
            pl.BlockSpec((ROUTER_TM, d), lambda i: (i, 0)),
            const((1, d)),
            const((N_EXPERTS, d)),
            const((N_EXPERTS, 1)),
            const((ROUTER_TM, ROUTER_TM)),
        ],
        out_specs=[
            pl.BlockSpec((ROUTER_TM * ROW_SUBLANES, LANES), lambda i: (i, 0)),
            tok(), tok(), tok(),
            pl.BlockSpec((N_EXPERTS, lanes), lambda i: (0, 0)),
        ],
        out_shape=[
            jax.ShapeDtypeStruct((t * ROW_SUBLANES, LANES), jnp.uint32),
            jax.ShapeDtypeStruct((rows, t), jnp.int32),
            jax.ShapeDtypeStruct((rows, t), jnp.int32),
            jax.ShapeDtypeStruct((rows, t), F32),
            jax.ShapeDtypeStruct((N_EXPERTS, lanes), F32),
        ],
        scratch_shapes=[pltpu.VMEM((N_EXPERTS, 1), F32)],
        compiler_params=pltpu.CompilerParams(
            dimension_semantics=("arbitrary",), vmem_limit_bytes=VMEM_LIMIT),
        name="router",
    )(h1, g2, w_router_t, b_router_t, ut)


def _dispatch_kernel(dest_ref, xp_ref, xs_in_ref, xs_ref, sem):
    del xs_in_ref
    i = pl.program_id(0)

    n_tok = dest_ref.shape[0] // TOP_K

    def row_copy(s, r):
        return pltpu.make_async_copy(_tile_of_row(xp_ref, s), _tile_of_row(xs_ref, r), sem)

    def issue(b, _):
        for j in range(DMA_UNROLL):
            s = b * DMA_UNROLL + j
            for k in range(TOP_K):
                row_copy(s, dest_ref[k * n_tok + i * DISPATCH_TB + s]).start(priority=k % 2)
        return 0

    def drain(b, _):
        for _ in range(DMA_UNROLL * TOP_K):
            row_copy(0, 0).wait()
        return 0

    lax.fori_loop(0, DISPATCH_TB // DMA_UNROLL, issue, 0)
    lax.fori_loop(0, DISPATCH_TB // DMA_UNROLL, drain, 0)


def _dispatch(dest_flat, xp, xs0):
    t = xp.shape[0] // ROW_SUBLANES
    n_rows = xs0.shape[0] // ROW_SUBLANES
    return pl.pallas_call(
        _dispatch_kernel,
        grid_spec=pltpu.PrefetchScalarGridSpec(
            num_scalar_prefetch=1,
            grid=(t // DISPATCH_TB,),
            in_specs=[pl.BlockSpec((DISPATCH_TB * ROW_SUBLANES, LANES), lambda i, dest: (i, 0)),
                      pl.BlockSpec(memory_space=pl.ANY)],
            out_specs=pl.BlockSpec(memory_space=pl.ANY),
            scratch_shapes=[pltpu.SemaphoreType.DMA(())],
        ),
        out_shape=jax.ShapeDtypeStruct((n_rows * ROW_SUBLANES, LANES), jnp.uint32),
        input_output_aliases={2: 0},
        compiler_params=pltpu.CompilerParams(
            dimension_semantics=("arbitrary",), has_side_effects=True),
        name="dispatch",
    )(dest_flat, xp, xs0)


def _expert_pieces(n_rows):
    k = -(-n_rows // EXP_SUB)
    base, extra = divmod(n_rows // EXP_GRAN, k)
    return [(base + (1 if j < extra else 0)) * EXP_GRAN for j in range(k)]


def _expert_kernel(ie_ref, it_ref, nv_ref, xs_ref, wg_ref, wu_ref, wd_ref, bg_ref, bu_ref, bd_ref,
                   y_ref, acc_ref):
    del ie_ref, it_ref
    i = pl.program_id(0)
    c = pl.program_id(1)
    nc = pl.num_programs(1)
    nv = nv_ref[i]
    half = ROW_SUBLANES * LANES

    @pl.when((i == 0) & (c == 0))
    def _():
        acc_ref[...] = jnp.zeros_like(acc_ref)

    def run(n_rows):
        wg = wg_ref[0].astype(BF16)
        wu = wu_ref[0].astype(BF16)
        wd = wd_ref[0].astype(BF16)
        first = c == 0
        lo = 0
        for m in _expert_pieces(n_rows):
            rows = slice(lo, lo + m)
            tiles = lambda s, lo=lo, m=m: pl.ds(lo * ROW_SUBLANES + s, m, stride=ROW_SUBLANES)
            pairs = [_unpack_bf16_pair(xs_ref[tiles(s), :]) for s in range(ROW_SUBLANES)]
            xb = jnp.concatenate([p[0].astype(BF16) for p in pairs] + [p[1].astype(BF16) for p in pairs], axis=1)
            g = jnp.dot(xb, wg, preferred_element_type=F32) + bg_ref[0]
            u = jnp.dot(xb, wu, preferred_element_type=F32) + bu_ref[0]
            g = jnp.minimum(g, SWIGLU_LIMIT)
            u = jnp.clip(u, -SWIGLU_LIMIT, SWIGLU_LIMIT)
            a = g * _sigmoid(SWIGLU_ALPHA * g) * (u + 1.0)
            y = jnp.dot(a.astype(BF16), wd, preferred_element_type=F32)
            acc_ref[rows, :] = jnp.where(first, jnp.broadcast_to(bd_ref[0], y.shape), acc_ref[rows, :]) + y
            lo += m

    for n_rows in range(EXP_MIN_ROWS, EXP_TM + 1, EXP_GRAN):
        above = 0 if n_rows == EXP_MIN_ROWS else n_rows - EXP_GRAN
        pl.when((nv > above) & (nv <= n_rows))(lambda n_rows=n_rows: run(n_rows))

    @pl.when((c == nc - 1) & (nv > 0))
    def _():
        for s in range(ROW_SUBLANES):
            lo = acc_ref[:, s * LANES:(s + 1) * LANES]
            hi = acc_ref[:, half + s * LANES:half + (s + 1) * LANES]
            y_ref[pl.ds(s, EXP_TM, stride=ROW_SUBLANES), :] = _pack_bf16_pair(lo, hi)


def _expert(item_e, item_t, item_nv, xs, w_up, b_up, w_down, b_down):
    n_rows = xs.shape[0] // ROW_SUBLANES
    ne, d, two_f = w_up.shape
    assert d == 2 * ROW_SUBLANES * LANES
    tile = (EXP_TM * ROW_SUBLANES, LANES)
    f = two_f // 2
    nc = f // EXP_TC
    n_items = item_e.shape[0]
    b_up3 = b_up.reshape(ne, 1, two_f)
    b_down3 = b_down.reshape(ne, 1, d)

    def chunk(c, nv_ref, i):
        return jnp.where(nv_ref[i] > 0, c, nc - 1)

    return pl.pallas_call(
        _expert_kernel,
        grid_spec=pltpu.PrefetchScalarGridSpec(
            num_scalar_prefetch=3,
            grid=(n_items, nc),
            in_specs=[
                pl.BlockSpec(tile, lambda i, c, ie, it, nv: (it[i], 0)),
                pl.BlockSpec((1, d, EXP_TC), lambda i, c, ie, it, nv: (ie[i], 0, chunk(c, nv, i))),
                pl.BlockSpec((1, d, EXP_TC), lambda i, c, ie, it, nv: (ie[i], 0, nc + chunk(c, nv, i))),
                pl.BlockSpec((1, EXP_TC, d), lambda i, c, ie, it, nv: (ie[i], chunk(c, nv, i), 0)),
                pl.BlockSpec((1, 1, EXP_TC), lambda i, c, ie, it, nv: (ie[i], 0, chunk(c, nv, i))),
                pl.BlockSpec((1, 1, EXP_TC), lambda i, c, ie, it, nv: (ie[i], 0, nc + chunk(c, nv, i))),
                pl.BlockSpec((1, 1, d), lambda i, c, ie, it, nv: (ie[i], 0, 0)),
            ],
            out_specs=pl.BlockSpec(tile, lambda i, c, ie, it, nv: (it[i], 0)),
            scratch_shapes=[pltpu.VMEM((EXP_TM, d), F32)],
        ),
        out_shape=jax.ShapeDtypeStruct((n_rows * ROW_SUBLANES, LANES), jnp.uint32),
        input_output_aliases={3: 0},
        compiler_params=pltpu.CompilerParams(
            dimension_semantics=("arbitrary", "arbitrary"), vmem_limit_bytes=VMEM_LIMIT),
        name="expert",
    )(item_e, item_t, item_nv, xs, w_up, w_up, w_down, b_up3, b_up3, b_down3)


def _combine_kernel(dest_ref, y_ref, h_ref, gate_ref, o_ref, buf_ref, sem):
    i = pl.program_id(0)
    n = pl.num_programs(0)
    n_tok = dest_ref.shape[0] // TOP_K

    def row_copy(r, slot, k, s):
        return pltpu.make_async_copy(_tile_of_row(y_ref, r), _tile_of_row(buf_ref.at[slot, k], s), sem.at[slot])

    def gather(step, slot):
        def issue(b, _):
            for j in range(DMA_UNROLL):
                s = b * DMA_UNROLL + j
                for k in range(TOP_K):
                    row_copy(dest_ref[k * n_tok + step * COMBINE_TB + s], slot, k, s).start(priority=k % 2)
            return 0
        lax.fori_loop(0, COMBINE_TB // DMA_UNROLL, issue, 0)

    @pl.when(i == 0)
    def _():
        gather(0, 0)

    for slot in range(2):
        @pl.when((i + 1 < n) & ((i + 1) % 2 == slot))
        def _():
            gather(i + 1, slot)

    for slot in range(2):
        @pl.when(i % 2 == slot)
        def _():
            def drain(b, _):
                for _ in range(DMA_UNROLL * TOP_K):
                    row_copy(0, slot, 0, 0).wait()
                return 0
            lax.fori_loop(0, COMBINE_TB // DMA_UNROLL, drain, 0)
            half = ROW_SUBLANES * LANES
            gates = [jnp.broadcast_to(gate_ref[:, k:k + 1], (COMBINE_TB, LANES)) for k in range(TOP_K)]
            for s in range(ROW_SUBLANES):
                lo_cols = slice(s * LANES, (s + 1) * LANES)
                hi_cols = slice(half + s * LANES, half + (s + 1) * LANES)
                acc_lo = h_ref[:, lo_cols]
                acc_hi = h_ref[:, hi_cols]
                for k in range(TOP_K):
                    lo, hi = _unpack_bf16_pair(_load_tile_rows(buf_ref.at[slot, k], s, COMBINE_TB))
                    acc_lo = acc_lo + gates[k] * lo
                    acc_hi = acc_hi + gates[k] * hi
                o_ref[:, lo_cols] = acc_lo
                o_ref[:, hi_cols] = acc_hi


def _combine(dest_flat, y, h1, gate):
    t, d = h1.shape
    lanes = gate.shape[1]
    return pl.pallas_call(
        _combine_kernel,
        grid_spec=pltpu.PrefetchScalarGridSpec(
            num_scalar_prefetch=1,
            grid=(t // COMBINE_TB,),
            in_specs=[
                pl.BlockSpec(memory_space=pl.ANY),
                pl.BlockSpec((COMBINE_TB, d), lambda i, dest: (i, 0)),
                pl.BlockSpec((COMBINE_TB, lanes), lambda i, dest: (i, 0)),
            ],
            out_specs=pl.BlockSpec((COMBINE_TB, d), lambda i, dest: (i, 0)),
            scratch_shapes=[pltpu.VMEM((2, TOP_K, COMBINE_TB * ROW_SUBLANES, LANES), jnp.uint32),
                            pltpu.SemaphoreType.DMA((2,))],
        ),
        out_shape=jax.ShapeDtypeStruct((t, d), F32),
        compiler_params=pltpu.CompilerParams(
            dimension_semantics=("arbitrary",), vmem_limit_bytes=VMEM_LIMIT),
        name="combine",
    )(dest_flat, y, h1, gate)


def _plan_items(counts, n_items_max):
    tiles = (counts + EXP_TM - 1) // EXP_TM
    tile_end = jnp.cumsum(tiles)
    tile_start = tile_end - tiles
    n_items = tile_end[-1]
    g = jnp.arange(n_items_max, dtype=jnp.int32)
    live = g < n_items
    gg = jnp.maximum(jnp.minimum(g, n_items - 1), 0)
    e = jnp.minimum(jnp.searchsorted(tile_end, gg, side="right"), N_EXPERTS - 1).astype(jnp.int32)
    nv = jnp.clip(counts[e] - (gg - tile_start[e]) * EXP_TM, 0, EXP_TM)
    nv = jnp.where(live, nv, 0).astype(jnp.int32)
    return tile_start * EXP_TM, e, gg.astype(jnp.int32), nv


def _layer(x2, bsz, seq, norm1_gain, w_in, q_norm_gain, k_norm_gain, sgu_norm_gain, w_spatial, b_spatial,
           w_branch_a, w_branch_b, w_out, norm2_gain, w_router, b_router, w_up, b_up, w_down, b_down):
    t, d = x2.shape
    proj = _proj(x2, norm1_gain[None, :], w_in, q_norm_gain[None, :], k_norm_gain[None, :])
    oa = _attn(proj, bsz, seq, d, q_norm_gain, k_norm_gain)
    n_items_max = -(-(t * TOP_K) // EXP_TM) + N_EXPERTS
    h1, xs0 = _merge(x2, oa, proj, w_spatial, b_spatial.T, sgu_norm_gain[None, :],
                     w_branch_a.astype(BF16), w_branch_b.astype(BF16), w_out.astype(BF16),
                     n_items_max * EXP_TM * ROW_SUBLANES)
    xp, idx, rank, gate, cnt = _router(h1, norm2_gain[None, :], w_router.T, b_router[:, None])

    counts = cnt[:, 0].astype(jnp.int32)
    seg_start, item_e, item_t, item_nv = _plan_items(counts, n_items_max)
    onehot = idx[:TOP_K, :, None] == jnp.arange(N_EXPERTS, dtype=jnp.int32)
    dest = (jnp.sum(jnp.where(onehot, seg_start.astype(jnp.int32), 0), axis=-1)
            + rank[:TOP_K]).reshape(-1)

    xs = _dispatch(dest, xp, xs0)
    y = _expert(item_e, item_t, item_nv, xs, w_up, b_up, w_down, b_down)
    return _combine(dest, y, h1, gate[:TOP_K].T)


def kernel(x, norm1_gain, w_in, q_norm_gain, k_norm_gain, sgu_norm_gain, w_spatial, b_spatial, w_branch_a,
           w_branch_b, w_out, norm2_gain, w_router, b_router, w_up, b_up, w_down, b_down):
    bsz, seq, d = x.shape
    h = x.reshape(bsz * seq, d)
    for l in range(norm1_gain.shape[0]):
        h = _layer(h, bsz, seq, norm1_gain[l], w_in[l], q_norm_gain[l], k_norm_gain[l], sgu_norm_gain[l],
                   w_spatial[l], b_spatial[l], w_branch_a[l], w_branch_b[l], w_out[l], norm2_gain[l],
                   w_router[l], b_router[l], w_up[l], b_up[l], w_down[l], b_down[l])
    return h.reshape(bsz, seq, d)
```
